```python
import jax, jax.numpy as jnp
from jax import lax
import numpy as np

D_MODEL = 1024
BATCH = 4
SEQ = 4096
DEPTH = 4

GRID_W = 64
EPS = 1e-6
D_FF = 4 * D_MODEL
N_BRANCH = 4

HG_HEADS = 4
HG_DK = 128
HG_DV = (D_MODEL // 2) // HG_HEADS
HG_WIDTH = HG_HEADS * HG_DV
HG_KEY = HG_HEADS * HG_DK
HG_CHUNK = 16

NA_HEADS = 8
NA_DH = 64
NA_WIDTH = NA_HEADS * NA_DH
NA_ROWS = 8
NA_COLS = 16

LRU_WIDTH = D_MODEL // 2
LRU_BLOCKS = 8
LRU_BW = LRU_WIDTH // LRU_BLOCKS
LRU_CONV = 4
LRU_C = 8.0

SSD_INNER = D_MODEL
SSD_HEADDIM = 64
SSD_HEADS = SSD_INNER // SSD_HEADDIM
SSD_GROUPS = 4
SSD_STATE = 128
SSD_CONV = 4
SSD_CHUNK = 128
SSD_CONV_CH = SSD_INNER + 2 * SSD_GROUPS * SSD_STATE

IN_SIZES = (HG_KEY, HG_KEY, HG_KEY, HG_WIDTH, HG_WIDTH,
            NA_WIDTH, NA_WIDTH, NA_WIDTH,
            LRU_WIDTH, LRU_WIDTH,
            SSD_INNER, SSD_CONV_CH, 2 * SSD_HEADS,
            N_BRANCH * D_MODEL)
N_IN = sum(IN_SIZES)

kernel_name = "hybrid_hgrn2_natten_rglru_ssd_encoder"


def rmsnorm(x, g):
    x32 = x.astype(jnp.float32)
    y = x32 * lax.rsqrt(jnp.mean(x32 * x32, axis=-1, keepdims=True) + EPS)
    return y.astype(x.dtype) * g.astype(x.dtype)


def depthwise_conv(x, w, b):
    k_w, ch = w.shape
    left = k_w // 2
    y = lax.conv_general_dilated(x, w[:, None, :].astype(x.dtype), window_strides=(1,),
                                 padding=[(left, k_w - 1 - left)],
                                 dimension_numbers=('NWC', 'WIO', 'NWC'), feature_group_count=ch)
    return y + b.astype(x.dtype)


def _linear_combine(e1, e2):
    a1, u1 = e1
    a2, u2 = e2
    return a1 * a2, a2 * u1 + u2


def _gated_linear_scan(q, logf, k, v):
    bsz, s, h, dk = q.shape
    dv = v.shape[-1]
    n = s // HG_CHUNK

    def to_chunks(t):
        return t.reshape(bsz, n, HG_CHUNK, h, t.shape[-1]).transpose(1, 0, 3, 2, 4)

    causal = jnp.tril(jnp.ones((HG_CHUNK, HG_CHUNK), bool))

    def step(state, inp):
        qc, lfc, kc, vc = inp
        b = jnp.cumsum(lfc, axis=-2)
        diff = b[..., :, None, :] - b[..., None, :, :]
        dec = jnp.exp(jnp.where(causal[:, :, None], diff, -jnp.inf))
        scores = jnp.einsum('bhtk,bhsk,bhtsk->bhts', qc, kc, dec)
        o = (jnp.einsum('bhts,bhsv->bhtv', scores, vc)
             + jnp.einsum('bhtk,bhkv->bhtv', qc * jnp.exp(b), state))
        b_last = b[..., -1:, :]
        new_state = (state * jnp.exp(b_last[..., 0, :])[..., None]
                     + jnp.einsum('bhsk,bhsv->bhkv', kc * jnp.exp(b_last - b), vc))
        return new_state, o

    state0 = jnp.zeros((bsz, h, dk, dv), jnp.float32)
    _, o = lax.scan(step, state0, (to_chunks(q), to_chunks(logf), to_chunks(k), to_chunks(v)))
    return o.transpose(1, 0, 3, 2, 4).reshape(bsz, s, h, dv)


def hgrn2_mixer(q, zf_fwd, zf_bwd, i, g, lb_fwd, lb_bwd, norm_g):
    bsz, s, _ = q.shape

    def heads(t, d):
        return t.astype(jnp.float32).reshape(bsz, s, HG_HEADS, d)

    def forget(z, lb):
        z = z.astype(jnp.float32)
        lb = lb.astype(jnp.float32)
        logf = jnp.logaddexp(jnp.log(lb), jnp.log1p(-lb) + jax.nn.log_sigmoid(z))
        key = (1.0 - lb) * jax.nn.sigmoid(-z)
        return heads(logf, HG_DK), heads(key, HG_DK)

    qh = heads(q, HG_DK)
    vh = heads(i, HG_DV)
    lf_f, k_f = forget(zf_fwd, lb_fwd)
    lf_b, k_b = forget(zf_bwd, lb_bwd)
    o_f = _gated_linear_scan(qh, lf_f, k_f, vh)
    o_b = _gated_linear_scan(qh[:, ::-1], lf_b[:, ::-1], k_b[:, ::-1], vh[:, ::-1])[:, ::-1]
    o = o_f + o_b
    o = o * lax.rsqrt(jnp.mean(o * o, axis=-1, keepdims=True) + EPS)
    o = o.reshape(bsz, s, HG_WIDTH) * norm_g.astype(jnp.float32)
    return o * jax.nn.silu(g.astype(jnp.float32))


def neighbourhood_attention(q, k, v, rpb):
    bsz, s, h, dh = q.shape
    rows = s // GRID_W
    wr = min(NA_ROWS, rows)
    n_cb = GRID_W // NA_COLS
    kbw = min(2 * NA_COLS, GRID_W)
    r = np.arange(rows)
    krow = np.clip(r - NA_ROWS // 2, 0, rows - wr)[:, None] + np.arange(wr)[None, :]
    qcol = np.arange(n_cb)[:, None] * NA_COLS + np.arange(NA_COLS)[None, :]
    kcol = (np.clip(np.arange(n_cb) * NA_COLS - NA_COLS // 2, 0, GRID_W - kbw)[:, None]
            + np.arange(kbw)[None, :])
    cstart = np.clip(qcol - NA_COLS // 2, 0, GRID_W - NA_COLS)
    col_ok = ((kcol[:, None, :] >= cstart[:, :, None])
              & (kcol[:, None, :] < cstart[:, :, None] + NA_COLS))
    dr = krow - r[:, None] + NA_ROWS - 1
    dc = np.clip(kcol[:, None, :] - qcol[:, :, None], 1 - NA_COLS, NA_COLS - 1) + NA_COLS - 1
    bias = rpb.astype(jnp.float32)[:, dr[:, None, None, :, None], dc[None, :, :, None, :]]
    bias = jnp.where(col_ok[None, None, :, :, None, :], bias, -jnp.inf)
    key_idx = krow[:, None, :, None] * GRID_W + kcol[None, :, None, :]
    qb = q.reshape(bsz, rows, n_cb, NA_COLS, h, dh)
    kb = k[:, key_idx]
    vb = v[:, key_idx]
    sc = jnp.einsum('brmqhd,brmakhd->bhrmqak', qb, kb).astype(jnp.float32) * (dh ** -0.5) + bias[None]
    p = jax.nn.softmax(sc, axis=(-2, -1)).astype(v.dtype)
    o = jnp.einsum('bhrmqak,brmakhd->brmqhd', p, vb)
    return o.reshape(bsz, s, h * dh)


def rglru_mixer(xb, gate, conv_w, conv_b, gate_w, gate_b, lam):
    bsz, s, _ = xb.shape
    xf = depthwise_conv(xb, conv_w, conv_b).astype(jnp.float32)

    def direction(xs, w, b, lam_d):
        blk = xs.reshape(bsz, s, LRU_BLOCKS, LRU_BW)
        gates = (jnp.einsum('bsnd,gnde->gbsne', blk, w.astype(jnp.float32)).reshape(2, bsz, s, LRU_WIDTH)
                 + b.astype(jnp.float32)[:, None, None, :])
        rg = jax.nn.sigmoid(gates[0])
        ig = jax.nn.sigmoid(gates[1])
        log_a = -LRU_C * rg * jax.nn.softplus(-lam_d.astype(jnp.float32))
        a = jnp.exp(log_a)
        u = jnp.sqrt(-jnp.expm1(2.0 * log_a)) * (ig * xs)
        _, hs = lax.associative_scan(_linear_combine, (a, u), axis=1)
        return hs

    h_f = direction(xf, gate_w[0], gate_b[0], lam[0])
    h_b = direction(xf[:, ::-1], gate_w[1], gate_b[1], lam[1])[:, ::-1]
    return (h_f + h_b) * jax.nn.gelu(gate.astype(jnp.float32))


def _segsum(a):
    t = a.shape[-1]
    x = jnp.repeat(a[..., None], t, axis=-1)
    x = jnp.where(jnp.tril(jnp.ones((t, t), bool), -1), x, 0.0)
    sm = jnp.cumsum(x, axis=-2)
    return jnp.where(jnp.tril(jnp.ones((t, t), bool)), sm, -jnp.inf)


def _ssd_scan(x, dt, a_head, bm, cm):
    bsz, s, h, p = x.shape
    g, nst = bm.shape[2], bm.shape[3]
    rep = h // g
    q = SSD_CHUNK
    n = s // q
    xd = (x * dt[..., None]).reshape(bsz, n, q, g, rep, p)
    a = (dt * a_head).reshape(bsz, n, q, g, rep).transpose(0, 3, 4, 1, 2)
    bc = bm.reshape(bsz, n, q, g, nst)
    cc = cm.reshape(bsz, n, q, g, nst)
    a_cum = jnp.cumsum(a, axis=-1)
    lmat = jnp.exp(_segsum(a))
    cb = jnp.einsum('bnqgs,bnkgs->bgnqk', cc, bc)
    y_diag = jnp.einsum('bgnqk,bgrnqk,bnkgrp->bnqgrp', cb, lmat, xd)
    decay_states = jnp.exp(a_cum[..., -1:] - a_cum)
    states = jnp.einsum('bnkgs,bgrnk,bnkgrp->bngrps', bc, decay_states, xd)
    chunk_a = jnp.pad(a_cum[..., -1], ((0, 0), (0, 0), (0, 0), (1, 0)))
    decay_chunk = jnp.exp(_segsum(chunk_a))
    states = jnp.concatenate([jnp.zeros_like(states[:, :1]), states], axis=1)
    states = jnp.einsum('bgrzc,bcgrps->bzgrps', decay_chunk, states)[:, :-1]
    y_off = jnp.einsum('bnqgs,bngrps,bgrnq->bnqgrp', cc, states, jnp.exp(a_cum))
    return (y_diag + y_off).reshape(bsz, s, h, p)


def ssd_mixer(z, xbc, dt_raw, conv_w, conv_b, dt_bias, a_log, skip, norm_g):
    bsz, s, _ = z.shape
    xbc = jax.nn.silu(depthwise_conv(xbc, conv_w, conv_b)).astype(jnp.float32)
    xs, bm, cm = jnp.split(xbc, [SSD_INNER, SSD_INNER + SSD_GROUPS * SSD_STATE], axis=-1)
    xs = xs.reshape(bsz, s, SSD_HEADS, SSD_HEADDIM)
    bm = bm.reshape(bsz, s, SSD_GROUPS, SSD_STATE)
    cm = cm.reshape(bsz, s, SSD_GROUPS, SSD_STATE)
    dt = jax.nn.softplus(dt_raw.astype(jnp.float32).reshape(bsz, s, 2, SSD_HEADS) + dt_bias.astype(jnp.float32))
    a_head = -jnp.exp(a_log.astype(jnp.float32))
    y_f = _ssd_scan(xs, dt[:, :, 0], a_head[0], bm, cm)
    y_b = _ssd_scan(xs[:, ::-1], dt[:, ::-1, 1], a_head[1], bm[:, ::-1], cm[:, ::-1])[:, ::-1]
    y = y_f + y_b + skip.astype(jnp.float32)[:, None] * xs
    y = y.reshape(bsz, s, SSD_INNER) * jax.nn.silu(z.astype(jnp.float32))
    y = y * lax.rsqrt(jnp.mean(y * y, axis=-1, keepdims=True) + EPS)
    return y * norm_g.astype(jnp.float32)


def token_mixers(h, w_in, lb_fwd, lb_bwd, hg_norm, na_rpb, lru_conv_w, lru_conv_b, lru_gate_w, lru_gate_b,
                 lru_lambda, ssd_conv_w, ssd_conv_b, ssd_dt_bias, ssd_a_log, ssd_skip, ssd_norm,
                 w_br_hg, w_br_na, w_br_lru, w_br_ssd, w_out):
    bsz, s, _ = h.shape
    u = h @ w_in
    splits = np.cumsum(IN_SIZES)[:-1].tolist()
    (hg_q, hg_zf, hg_zb, hg_i, hg_g, na_q, na_k, na_v, lru_x, lru_g,
     ssd_z, ssd_xbc, ssd_dt, merge) = jnp.split(u, splits, axis=-1)
    y_hg = hgrn2_mixer(hg_q, hg_zf, hg_zb, hg_i, hg_g, lb_fwd, lb_bwd, hg_norm)
    y_na = neighbourhood_attention(na_q.reshape(bsz, s, NA_HEADS, NA_DH), na_k.reshape(bsz, s, NA_HEADS, NA_DH),
                                   na_v.reshape(bsz, s, NA_HEADS, NA_DH), na_rpb)
    y_lru = rglru_mixer(lru_x, lru_g, lru_conv_w, lru_conv_b, lru_gate_w, lru_gate_b, lru_lambda)
    y_ssd = ssd_mixer(ssd_z, ssd_xbc, ssd_dt, ssd_conv_w, ssd_conv_b, ssd_dt_bias, ssd_a_log, ssd_skip, ssd_norm)
    gt = jax.nn.sigmoid(merge.astype(jnp.float32)).astype(h.dtype).reshape(bsz, s, N_BRANCH, D_MODEL)
    merged = (gt[:, :, 0] * (y_hg.astype(h.dtype) @ w_br_hg)
              + gt[:, :, 1] * (y_na.astype(h.dtype) @ w_br_na)
              + gt[:, :, 2] * (y_lru.astype(h.dtype) @ w_br_lru)
              + gt[:, :, 3] * (y_ssd.astype(h.dtype) @ w_br_ssd))
    return merged @ w_out


def setup_inputs(seed: int = 0) -> dict:
    key = jax.random.key(seed)
    ks = iter(jax.random.split(key, 40))
    f32 = jnp.float32
    L = DEPTH

    def nrm(shape, scale):
        return jax.random.normal(next(ks), shape, f32) * scale

    x = nrm((BATCH, SEQ, D_MODEL), 1.0)
    c = nrm((BATCH, D_MODEL), 1.0)
    w_ada = nrm((L, D_MODEL, 6 * D_MODEL), 0.5 * D_MODEL ** -0.5)
    b_ada = nrm((L, 6 * D_MODEL), 0.02)
    g_mix = 1.0 + nrm((L, D_MODEL), 0.02)
    g_ffn = 1.0 + nrm((L, D_MODEL), 0.02)
    w_in = nrm((L, D_MODEL, N_IN), D_MODEL ** -0.5)
    hg_lb = nrm((2, L, HG_KEY), 0.5)
    hg_norm = 1.0 + nrm((L, HG_WIDTH), 0.02)
    na_rpb = nrm((L, NA_HEADS, 2 * NA_ROWS - 1, 2 * NA_COLS - 1), 0.1)
    lru_conv_w = nrm((L, LRU_CONV, LRU_WIDTH), LRU_CONV ** -0.5)
    lru_conv_b = nrm((L, LRU_WIDTH), 0.02)
    lru_gate_w = nrm((L, 2, 2, LRU_BLOCKS, LRU_BW, LRU_BW), LRU_BW ** -0.5)
    lru_gate_b = nrm((L, 2, 2, LRU_WIDTH), 0.02)
    a_pow = jax.random.uniform(next(ks), (L, 2, LRU_WIDTH), f32, minval=0.9, maxval=0.999)
    a0 = a_pow ** (1.0 / LRU_C)
    lru_lambda = jnp.log(a0) - jnp.log1p(-a0)
    ssd_conv_w = nrm((L, SSD_CONV, SSD_CONV_CH), SSD_CONV ** -0.5)
    ssd_conv_b = nrm((L, SSD_CONV_CH), 0.02)
    dt0 = jnp.exp(jax.random.uniform(next(ks), (L, 2, SSD_HEADS), f32,
                                     minval=float(np.log(1e-3)), maxval=float(np.log(1e-1))))
    ssd_dt_bias = dt0 + jnp.log(-jnp.expm1(-dt0))
    ssd_a_log = jnp.log(jax.random.uniform(next(ks), (L, 2, SSD_HEADS), f32, minval=1.0, maxval=16.0))
    ssd_skip = 1.0 + nrm((L, SSD_HEADS), 0.1)
    ssd_norm = 1.0 + nrm((L, SSD_INNER), 0.02)
    w_br_hg = nrm((L, HG_WIDTH, D_MODEL), HG_WIDTH ** -0.5)
    w_br_na = nrm((L, NA_WIDTH, D_MODEL), NA_WIDTH ** -0.5)
    w_br_lru = nrm((L, LRU_WIDTH, D_MODEL), LRU_WIDTH ** -0.5)
    w_br_ssd = nrm((L, SSD_INNER, D_MODEL), SSD_INNER ** -0.5)
    w_out = nrm((L, D_MODEL, D_MODEL), D_MODEL ** -0.5)
    w_ff1 = nrm((L, D_MODEL, D_FF), D_MODEL ** -0.5)
    w_ff2 = nrm((L, D_FF, D_MODEL), D_FF ** -0.5)
    g_final = 1.0 + nrm((D_MODEL,), 0.02)
    return {"x": x, "c": c, "w_ada": w_ada, "b_ada": b_ada, "g_mix": g_mix, "g_ffn": g_ffn, "w_in": w_in,
            "hg_lb": hg_lb, "hg_norm": hg_norm, "na_rpb": na_rpb,
            "lru_conv_w": lru_conv_w, "lru_conv_b": lru_conv_b, "lru_gate_w": lru_gate_w,
            "lru_gate_b": lru_gate_b, "lru_lambda": lru_lambda,
            "ssd_conv_w": ssd_conv_w, "ssd_conv_b": ssd_conv_b, "ssd_dt_bias": ssd_dt_bias,
            "ssd_a_log": ssd_a_log, "ssd_skip": ssd_skip, "ssd_norm": ssd_norm,
            "w_br_hg": w_br_hg, "w_br_na": w_br_na, "w_br_lru": w_br_lru, "w_br_ssd": w_br_ssd,
            "w_out": w_out, "w_ff1": w_ff1, "w_ff2": w_ff2, "g_final": g_final}


def reference(x, c, w_ada, b_ada, g_mix, g_ffn, w_in, hg_lb, hg_norm, na_rpb,
              lru_conv_w, lru_conv_b, lru_gate_w, lru_gate_b, lru_lambda,
              ssd_conv_w, ssd_conv_b, ssd_dt_bias, ssd_a_log, ssd_skip, ssd_norm,
              w_br_hg, w_br_na, w_br_lru, w_br_ssd, w_out, w_ff1, w_ff2, g_final):
    lb = jnp.cumsum(jax.nn.softmax(hg_lb.astype(jnp.float32), axis=1), axis=1)
    lb = lb - lb[:, :1]
    c_act = jax.nn.silu(c)
    for l in range(DEPTH):
        mod = (c_act @ w_ada[l] + b_ada[l])[:, None, :]
        shift_m, scale_m, gate_m, shift_f, scale_f, gate_f = jnp.split(mod.astype(x.dtype), 6, axis=-1)
        h = rmsnorm(x, g_mix[l]) * (1 + scale_m) + shift_m
        x = x + gate_m * token_mixers(h, w_in[l], lb[0, l], lb[1, l], hg_norm[l], na_rpb[l],
                                      lru_conv_w[l], lru_conv_b[l], lru_gate_w[l], lru_gate_b[l], lru_lambda[l],
                                      ssd_conv_w[l], ssd_conv_b[l], ssd_dt_bias[l], ssd_a_log[l], ssd_skip[l],
                                      ssd_norm[l], w_br_hg[l], w_br_na[l], w_br_lru[l], w_br_ssd[l], w_out[l])
        h = rmsnorm(x, g_ffn[l]) * (1 + scale_f) + shift_f
        x = x + gate_f * (jnp.square(jax.nn.relu(h @ w_ff1[l])) @ w_ff2[l])
    return rmsnorm(x, g_final)
```

```python
import functools

import numpy as np
import jax
import jax.numpy as jnp
from jax import lax
from jax.experimental import pallas as pl
from jax.experimental.pallas import tpu as pltpu

F32 = jnp.float32
BF16 = jnp.bfloat16

LANE = 128
D_MODEL = 1024
D_FF = 4 * D_MODEL
GRID_W = 64
EPS = 1e-6

HG_HEADS = 4
HG_CHUNK = 128
HG_LEVELS = (64, 32, 16, 8, 4)
HG_BASE = 4

NA_ROWS = 8
NA_COLS = 16
NA_HEADS = 8
NA_DH = 64

LRU_WIDTH = 512
LRU_BLOCKS = 8
LRU_BW = 64
LRU_C = 8.0
LRU_CW = 256
LRU_TC = 512

SSD_HEADS = 16
SSD_P = 64
SSD_GROUPS = 4
SSD_N = 128
SSD_Q = 128
HALO = 8

BLK_HG_Q, BLK_HG_ZF, BLK_HG_ZB, BLK_HG_I, BLK_HG_G = 0, 4, 8, 12, 16
BLK_NA_Q, BLK_NA_K, BLK_NA_V = 20, 24, 28
BLK_LRU_X, BLK_LRU_G = 32, 36
BLK_SSD_Z, BLK_SSD_X, BLK_SSD_B, BLK_SSD_C = 40, 48, 56, 60
BLK_MERGE = 64
N_BLK = 96
N_MAIN = N_BLK * LANE
DT_COL0 = 8192
N_DT = 2 * SSD_HEADS

VMEM_LIMIT = 48 * 1024 * 1024


def _cparams(sem):
    return pltpu.CompilerParams(dimension_semantics=sem, vmem_limit_bytes=VMEM_LIMIT)


def _sigmoid(x):
    return 1.0 / (1.0 + jnp.exp(-x))


def _softplus(x):
    return jnp.maximum(x, 0.0) + jnp.log1p(jnp.exp(-jnp.abs(x)))


def _cumsum_rows(x, rev):
    n = x.shape[0]
    rowi = lax.broadcasted_iota(jnp.int32, x.shape, 0)
    d = 1
    while d < n:
        if not rev:
            x = x + jnp.where(rowi >= d, pltpu.roll(x, d, 0), 0.0)
        else:
            x = x + jnp.where(rowi < n - d, pltpu.roll(x, n - d, 0), 0.0)
        d *= 2
    return x


def _ada_kernel(c_ref, w_ref, b_ref, o_ref):
    c = c_ref[...]
    ca = c * _sigmoid(c)
    o_ref[0] = jnp.dot(ca, w_ref[0], preferred_element_type=F32,
                       precision=lax.Precision.HIGHEST) + b_ref[0]


def _ada_mod(c, w_ada, b_ada):
    depth, d, d6 = w_ada.shape
    bsz = c.shape[0]
    rows = 8
    cp = jnp.zeros((rows, d), F32).at[:bsz].set(c)
    out = pl.pallas_call(
        _ada_kernel,
        grid=(depth, d6 // d),
        in_specs=[pl.BlockSpec((rows, d), lambda l, j: (0, 0)),
                  pl.BlockSpec((1, d, d), lambda l, j: (l, 0, j)),
                  pl.BlockSpec((1, 1, d), lambda l, j: (l, 0, j))],
        out_specs=pl.BlockSpec((1, rows, d), lambda l, j: (l, 0, j)),
        out_shape=jax.ShapeDtypeStruct((depth, rows, d6), F32),
        compiler_params=_cparams(("parallel", "parallel")),
    )(cp, w_ada, b_ada.reshape(depth, 1, d6))
    return out[:, :bsz]


def _lb_kernel(hg_ref, o_ref):
    depth = hg_ref.shape[0]
    xs = [hg_ref[l] for l in range(depth)]
    m = xs[0]
    for x in xs[1:]:
        m = jnp.maximum(m, x)
    es = [jnp.exp(x - m) for x in xs]
    tot = es[0]
    for e in es[1:]:
        tot = tot + e
    cs = None
    first = None
    for l in range(depth):
        sm = es[l] / tot
        cs = sm if cs is None else cs + sm
        if l == 0:
            first = cs
        lb = cs - first
        o_ref[0, l] = lb
        o_ref[1, l] = jnp.log(lb)
        o_ref[2, l] = jnp.log1p(-lb)


def _hg_lower_bounds(hg_lb):
    two, depth, k = hg_lb.shape
    return pl.pallas_call(
        _lb_kernel,
        out_shape=jax.ShapeDtypeStruct((3, depth, two, k), F32),
    )(hg_lb.transpose(1, 0, 2))


def _inproj_kernel(x_ref, mod_ref, g_ref, w_ref, wdt_ref, u_ref, udt_ref, h_scr, *, tn, gate_tile0):
    j = pl.program_id(1)

    @pl.when(j == 0)
    def _():
        x = x_ref[...]
        ms = jnp.mean(x * x, axis=-1, keepdims=True)
        y = x * lax.rsqrt(ms + EPS) * g_ref[...]
        mod = mod_ref[0]
        shift = mod[:, 0:D_MODEL]
        scale = mod[:, D_MODEL:2 * D_MODEL]
        hb = (y * (1.0 + scale) + shift).astype(BF16)
        h_scr[...] = hb
        udt_ref[...] = jnp.dot(hb, wdt_ref[...], preferred_element_type=F32)

    acc = jnp.dot(h_scr[...], w_ref[...], preferred_element_type=F32)

    @pl.when(j < gate_tile0)
    def _():
        for k in range(tn // LANE):
            u_ref[k] = acc[:, k * LANE:(k + 1) * LANE]

    @pl.when(j >= gate_tile0)
    def _():
        gt = _sigmoid(acc)
        for k in range(tn // LANE):
            u_ref[k] = gt[:, k * LANE:(k + 1) * LANE]


def _inproj(x2, mod_l, g, w_main, w_dt, seq, tm=1024, tn=1024):
    t = x2.shape[0]
    tm = min(tm, seq)
    nb = tn // LANE
    kern = functools.partial(_inproj_kernel, tn=tn, gate_tile0=BLK_MERGE // nb)
    return pl.pallas_call(
        kern,
        grid=(t // tm, N_MAIN // tn),
        in_specs=[pl.BlockSpec((tm, D_MODEL), lambda i, j: (i, 0)),
                  pl.BlockSpec((1, 1, 6 * D_MODEL), lambda i, j: ((i * tm) // seq, 0, 0)),
                  pl.BlockSpec((1, D_MODEL), lambda i, j: (0, 0)),
                  pl.BlockSpec((D_MODEL, tn), lambda i, j: (0, j)),
                  pl.BlockSpec((D_MODEL, LANE), lambda i, j: (0, 0))],
        out_specs=[pl.BlockSpec((nb, tm, LANE), lambda i, j: (j, i, 0)),
                   pl.BlockSpec((tm, LANE), lambda i, j: (i, 0))],
        out_shape=[jax.ShapeDtypeStruct((N_BLK, t, LANE), F32),
                   jax.ShapeDtypeStruct((t, LANE), F32)],
        scratch_shapes=[pltpu.VMEM((tm, D_MODEL), BF16)],
        compiler_params=_cparams(("parallel", "arbitrary")),
    )(x2, mod_l, g, w_main, w_dt)


def _hgrn_masks():
    c = HG_CHUNK
    t = np.arange(c)[:, None]
    s = np.arange(c)[None, :]
    out = np.zeros((2, len(HG_LEVELS), c, c), np.float32)
    for li, m in enumerate(HG_LEVELS):
        same = (t // (2 * m)) == (s // (2 * m))
        fwd = same & ((t % (2 * m)) >= m) & ((s % (2 * m)) < m)
        out[0, li] = fwd
        out[1, li] = fwd.T
    return out


def _hgrn_chunk(q, z, v, lb, llb, l1m, st, msk_ref, rev):
    c = HG_CHUNK
    d = 1 if rev else 0
    e = jnp.exp(-jnp.abs(z))
    logsig = jnp.minimum(z, 0.0) - jnp.log1p(e)
    key = (1.0 - lb) * (jnp.where(z >= 0, e, 1.0) / (1.0 + e))
    y2 = l1m + logsig
    lf = jnp.maximum(llb, y2) + jnp.log1p(jnp.exp(-jnp.abs(llb - y2)))
    b = _cumsum_rows(lf, rev)

    rowi = lax.broadcasted_iota(jnp.int32, (c, LANE), 0)
    nt = (((1,), (1,)), ((), ()))
    a = jnp.zeros((c, c), F32)
    for li, m in enumerate(HG_LEVELS):
        groups = c // (2 * m)
        off = m if rev else m - 1
        b3 = b.reshape(groups, 2 * m, LANE)
        bref = jnp.broadcast_to(b3[:, off:off + 1, :], (groups, 2 * m, LANE)).reshape(c, LANE)
        ex = jnp.exp(-jnp.abs(b - bref))
        upper = (rowi & (2 * m - 1)) >= m
        is_q = jnp.logical_not(upper) if rev else upper
        qe = jnp.where(is_q, q * ex, 0.0).astype(BF16)
        ke = jnp.where(is_q, 0.0, key * ex).astype(BF16)
        sl = lax.dot_general(qe, ke, nt, preferred_element_type=F32)
        a = a + sl * msk_ref[d, li]
    o = jnp.dot(a.astype(BF16), v.astype(BF16), preferred_element_type=F32)

    sub = rowi & (HG_BASE - 1)
    for dl in range(HG_BASE):
        if dl == 0:
            w = q * key
            vs = v
            ok = None
        else:
            sh = (c - dl) if rev else dl
            ks = pltpu.roll(key, sh, 0)
            bs = pltpu.roll(b, sh, 0)
            vs = pltpu.roll(v, sh, 0)
            w = q * ks * jnp.exp(jnp.minimum(b - bs, 0.0))
            ok = (sub <= HG_BASE - 1 - dl) if rev else (sub >= dl)
        r = jnp.sum(w, axis=-1, keepdims=True)
        term = r * vs
        o = o + (term if ok is None else jnp.where(ok, term, 0.0))

    qs = (q * jnp.exp(b)).astype(BF16)
    o = o + lax.dot_general(qs, st.astype(BF16), nt, preferred_element_type=F32)
    bedge = b[0:1, :] if rev else b[c - 1:c, :]
    ks_ = (key * jnp.exp(bedge - b)).astype(BF16)
    st_new = st * jnp.exp(bedge) + jnp.dot(v.T.astype(BF16), ks_, preferred_element_type=F32)
    return o, st_new


def _hgrn_kernel(q_ref, zf_ref, zb_ref, v_ref, g_ref, lbp_ref, norm_ref, msk_ref, y_ref,
                 of_scr, st_scr, *, seq):
    c = HG_CHUNK
    n = seq // c

    def load(ref, r0):
        return ref[0, pl.ds(r0, c), :]

    def fwd(i, carry):
        r0 = pl.multiple_of(i * c, c)
        o, st = _hgrn_chunk(load(q_ref, r0), load(zf_ref, r0), load(v_ref, r0),
                            lbp_ref[0, 0:1, :], lbp_ref[1, 0:1, :], lbp_ref[2, 0:1, :],
                            st_scr[...], msk_ref, False)
        of_scr[pl.ds(r0, c), :] = o
        st_scr[...] = st
        return carry

    def bwd(ii, carry):
        r0 = pl.multiple_of((n - 1 - ii) * c, c)
        o, st = _hgrn_chunk(load(q_ref, r0), load(zb_ref, r0), load(v_ref, r0),
                            lbp_ref[0, 1:2, :], lbp_ref[1, 1:2, :], lbp_ref[2, 1:2, :],
                            st_scr[...], msk_ref, True)
        st_scr[...] = st
        o = o + of_scr[pl.ds(r0, c), :]
        o = o * lax.rsqrt(jnp.mean(o * o, axis=-1, keepdims=True) + EPS) * norm_ref[...]
        g = load(g_ref, r0)
        y_ref[0, pl.ds(r0, c), :] = (o * (g * _sigmoid(g))).astype(BF16)
        return carry

    st_scr[...] = jnp.zeros_like(st_scr)
    lax.fori_loop(0, n, fwd, 0)
    st_scr[...] = jnp.zeros_like(st_scr)
    lax.fori_loop(0, n, bwd, 0)


def _hgrn(u3, lbp_l, hg_norm_l, masks, bsz, seq):
    def ublk(base):
        return pl.BlockSpec((1, seq, LANE), lambda b, h, base=base: (base + h, b, 0))

    kern = functools.partial(_hgrn_kernel, seq=seq)
    nlev = len(HG_LEVELS)
    return pl.pallas_call(
        kern,
        grid=(bsz, HG_HEADS),
        in_specs=[ublk(BLK_HG_Q), ublk(BLK_HG_ZF), ublk(BLK_HG_ZB), ublk(BLK_HG_I), ublk(BLK_HG_G),
                  pl.BlockSpec((3, 2, LANE), lambda b, h: (0, 0, h)),
                  pl.BlockSpec((1, LANE), lambda b, h: (0, h)),
                  pl.BlockSpec((2, nlev, HG_CHUNK, HG_CHUNK), lambda b, h: (0, 0, 0, 0))],
        out_specs=pl.BlockSpec((1, seq, LANE), lambda b, h: (h, b, 0)),
        out_shape=jax.ShapeDtypeStruct((HG_HEADS, bsz * seq, LANE), BF16),
        scratch_shapes=[pltpu.VMEM((seq, LANE), F32), pltpu.VMEM((LANE, LANE), F32)],
        compiler_params=_cparams(("parallel", "parallel")),
    )(u3, u3, u3, u3, u3, lbp_l, hg_norm_l, masks)


NA_NEG = -1e30


def _na_index_tables():
    kw = NA_ROWS * GRID_W
    qc = np.arange(GRID_W)[:, None]
    kc = np.arange(GRID_W)[None, :]
    cstart = np.clip(qc - NA_COLS // 2, 0, GRID_W - NA_COLS)
    col_ok = (kc >= cstart) & (kc < cstart + NA_COLS)
    dc = np.clip(kc - qc, 1 - NA_COLS, NA_COLS - 1) + NA_COLS - 1
    sft = np.arange(NA_ROWS)[:, None]
    a = np.arange(NA_ROWS)[None, :]
    dr = np.clip(a - sft + NA_ROWS - 1, 0, 2 * NA_ROWS - 2)
    dr_full = np.broadcast_to(dr[:, None, :, None], (NA_ROWS, GRID_W, NA_ROWS, GRID_W)).reshape(NA_ROWS, GRID_W, kw)
    dc_full = np.broadcast_to(dc[None, :, None, :], (NA_ROWS, GRID_W, NA_ROWS, GRID_W)).reshape(NA_ROWS, GRID_W, kw)
    ok_full = np.broadcast_to(col_ok[None, :, None, :], (NA_ROWS, GRID_W, NA_ROWS, GRID_W)).reshape(NA_ROWS, GRID_W, kw)
    return dr_full, dc_full, ok_full


def _na_bias_table(rpb_l):
    dr, dc, ok = _na_index_tables()
    tab = rpb_l.astype(F32)[:, dr, dc]
    return jnp.where(ok[None], tab, NA_NEG)


def _na_kernel(q_ref, k_ref, v_ref, bias_ref, o_ref, *, seq):
    rows = seq // GRID_W
    kw = NA_ROWS * GRID_W
    lane = lax.broadcasted_iota(jnp.int32, (GRID_W, LANE), 1)
    head0 = lane < NA_DH
    nt = (((1,), (1,)), ((), ()))

    def body(r, carry):
        kr0 = jnp.clip(r - NA_ROWS // 2, 0, rows - NA_ROWS)
        sft = r - kr0
        q0 = pl.multiple_of(r * GRID_W, GRID_W)
        k0 = pl.multiple_of(kr0 * GRID_W, GRID_W)
        q = q_ref[0, pl.ds(q0, GRID_W), :] * (NA_DH ** -0.5)
        kb = k_ref[0, pl.ds(k0, kw), :].astype(BF16)
        vb = v_ref[0, pl.ds(k0, kw), :].astype(BF16)
        outs = []
        for h in range(2):
            sel = head0 if h == 0 else jnp.logical_not(head0)
            qh = jnp.where(sel, q, 0.0).astype(BF16)
            s = lax.dot_general(qh, kb, nt, preferred_element_type=F32) + bias_ref[h, sft]
            m = jnp.max(s, axis=-1, keepdims=True)
            p = jnp.exp(s - m)
            l = jnp.sum(p, axis=-1, keepdims=True)
            outs.append(jnp.dot(p.astype(BF16), vb, preferred_element_type=F32) / l)
        o_ref[0, pl.ds(q0, GRID_W), :] = jnp.where(head0, outs[0], outs[1]).astype(BF16)
        return carry

    lax.fori_loop(0, rows, body, 0)


def _na(u3, bias, bsz, seq):
    def ublk(base):
        return pl.BlockSpec((1, seq, LANE), lambda b, hp, base=base: (base + hp, b, 0))

    kern = functools.partial(_na_kernel, seq=seq)
    kw = NA_ROWS * GRID_W
    return pl.pallas_call(
        kern,
        grid=(bsz, NA_HEADS // 2),
        in_specs=[ublk(BLK_NA_Q), ublk(BLK_NA_K), ublk(BLK_NA_V),
                  pl.BlockSpec((2, NA_ROWS, GRID_W, kw), lambda b, hp: (hp, 0, 0, 0))],
        out_specs=pl.BlockSpec((1, seq, LANE), lambda b, hp: (hp, b, 0)),
        out_shape=jax.ShapeDtypeStruct((NA_HEADS // 2, bsz * seq, LANE), BF16),
        compiler_params=_cparams(("parallel", "parallel")),
    )(u3, u3, u3, bias)


def _gelu_tanh(x):
    return 0.5 * x * (1.0 + jnp.tanh(0.7978845608028654 * (x + 0.044715 * (x * x * x))))


def _lru_kernel(x_ref, gate_ref, cw_ref, cb_ref, gw_ref, gb_ref, lam_ref, y_ref,
                xpad, a_scr, u_scr, hf_scr, *, seq):
    w = LRU_CW
    tc = LRU_TC
    nch = seq // tc
    ngrp = seq // 8
    nb = w // LANE

    zeros8 = jnp.zeros((HALO, w), F32)
    xpad[0:HALO, :] = zeros8
    xpad[seq + HALO:seq + 2 * HALO, :] = zeros8
    for k in range(nb):
        xpad[HALO:seq + HALO, k * LANE:(k + 1) * LANE] = x_ref[k]

    rowi = lax.broadcasted_iota(jnp.int32, (tc, w), 0)
    sub = rowi & 7

    for d in range(2):
        rev = d == 1
        sp = _softplus(-lam_ref[0, d])

        def pass_a(i, carry, d=d, rev=rev, sp=sp):
            r0 = pl.multiple_of(i * tc, tc)
            win = xpad[pl.ds(r0, tc + 2 * HALO), :]
            xf = cb_ref[...]
            for j in range(4):
                lo = HALO - 2 + j
                xf = xf + cw_ref[j:j + 1, :] * win[lo:lo + tc]
            gts = jnp.dot(xf.astype(BF16), gw_ref[0, d], preferred_element_type=F32) + gb_ref[0, d]
            rg = _sigmoid(gts[:, 0:w])
            ig = _sigmoid(gts[:, w:2 * w])
            log_a = (-LRU_C) * rg * sp
            a = jnp.exp(log_a)
            u = jnp.sqrt(-jnp.tanh(log_a) * (a * a + 1.0)) * (ig * xf)
            for dd in (1, 2, 4):
                if not rev:
                    ok = sub >= dd
                    a_s = pltpu.roll(a, dd, 0)
                    u_s = pltpu.roll(u, dd, 0)
                else:
                    ok = sub < 8 - dd
                    a_s = pltpu.roll(a, tc - dd, 0)
                    u_s = pltpu.roll(u, tc - dd, 0)
                u = jnp.where(ok, a * u_s + u, u)
                a = jnp.where(ok, a * a_s, a)
            a_scr[pl.ds(r0, tc), :] = a
            u_scr[pl.ds(r0, tc), :] = u
            return carry

        lax.fori_loop(0, nch, pass_a, 0)

        def pass_b(g, h, rev=rev):
            gi = (ngrp - 1 - g) if rev else g
            r0 = pl.multiple_of(gi * 8, 8)
            hp = h[0:1, :] if rev else h[7:8, :]
            hn = u_scr[pl.ds(r0, 8), :] + a_scr[pl.ds(r0, 8), :] * hp
            if rev:
                u_scr[pl.ds(r0, 8), :] = hn
            else:
                hf_scr[pl.ds(r0, 8), :] = hn
            return hn

        lax.fori_loop(0, ngrp, pass_b, jnp.zeros((8, w), F32), unroll=8)

    def pass_c(i, carry):
        r0 = pl.multiple_of(i * tc, tc)
        hsum = hf_scr[pl.ds(r0, tc), :] + u_scr[pl.ds(r0, tc), :]
        for k in range(nb):
            g = gate_ref[k, pl.ds(r0, tc), :]
            y_ref[k, pl.ds(r0, tc), :] = (hsum[:, k * LANE:(k + 1) * LANE] * _gelu_tanh(g)).astype(BF16)
        return carry

    lax.fori_loop(0, nch, pass_c, 0)


def _lru_weights(gate_w_l, gate_b_l, lam_l):
    ncb = LRU_WIDTH // LRU_CW
    per = LRU_CW // LRU_BW
    gw = gate_w_l.astype(F32)
    eye = jnp.eye(per, dtype=F32)
    blocks = gw.reshape(2, 2, ncb, per, LRU_BW, LRU_BW)
    dense = jnp.einsum('dgcpio,pq->dgcpiqo', blocks, eye).reshape(2, 2, ncb, LRU_CW, LRU_CW)
    dense = dense.transpose(2, 0, 3, 1, 4).reshape(ncb, 2, LRU_CW, 2 * LRU_CW).astype(BF16)
    gb = gate_b_l.astype(F32).reshape(2, 2, ncb, LRU_CW).transpose(2, 0, 1, 3).reshape(ncb, 2, 1, 2 * LRU_CW)
    lam = lam_l.astype(F32).reshape(2, ncb, LRU_CW).transpose(1, 0, 2).reshape(ncb, 2, 1, LRU_CW)
    return dense, gb, lam


def _lru(u3, conv_w_l, conv_b_l, gwd, gbd, lamd, bsz, seq):
    ncb = LRU_WIDTH // LRU_CW
    nb = LRU_CW // LANE
    kern = functools.partial(_lru_kernel, seq=seq)

    def ublk(base):
        return pl.BlockSpec((nb, seq, LANE), lambda b, cb, base=base: (base // nb + cb, b, 0))

    return pl.pallas_call(
        kern,
        grid=(bsz, ncb),
        in_specs=[ublk(BLK_LRU_X), ublk(BLK_LRU_G),
                  pl.BlockSpec((4, LRU_CW), lambda b, cb: (0, cb)),
                  pl.BlockSpec((1, LRU_CW), lambda b, cb: (0, cb)),
                  pl.BlockSpec((1, 2, LRU_CW, 2 * LRU_CW), lambda b, cb: (cb, 0, 0, 0)),
                  pl.BlockSpec((1, 2, 1, 2 * LRU_CW), lambda b, cb: (cb, 0, 0, 0)),
                  pl.BlockSpec((1, 2, 1, LRU_CW), lambda b, cb: (cb, 0, 0, 0))],
        out_specs=pl.BlockSpec((nb, seq, LANE), lambda b, cb: (cb, b, 0)),
        out_shape=jax.ShapeDtypeStruct((LRU_WIDTH // LANE, bsz * seq, LANE), BF16),
        scratch_shapes=[pltpu.VMEM((seq + 2 * HALO, LRU_CW), F32),
                        pltpu.VMEM((seq, LRU_CW), F32),
                        pltpu.VMEM((seq, LRU_CW), F32),
                        pltpu.VMEM((seq, LRU_CW), F32)],
        compiler_params=_cparams(("parallel", "parallel")),
    )(u3, u3, conv_w_l, conv_b_l.reshape(1, LRU_WIDTH), gwd, gbd, lamd)


def _ssd_front(cur_ref, prev_ref, next_ref, dt_ref, cw_ref, cb_ref, dtb_ref, alog_ref, n, nlast):
    q = SSD_Q
    nblk = cur_ref.shape[0]
    keep_prev = (n > 0).astype(F32)
    keep_next = (n < nlast).astype(F32)
    xs = []
    for k in range(nblk):
        full = jnp.concatenate([prev_ref[k] * keep_prev, cur_ref[k], next_ref[k] * keep_next], axis=0)
        acc = cb_ref[k]
        for j in range(4):
            lo = HALO - 2 + j
            acc = acc + cw_ref[j, k] * full[lo:lo + q]
        xs.append(acc * _sigmoid(acc))
    dt = _softplus(dt_ref[...] + dtb_ref[...])
    a = dt * (-jnp.exp(alog_ref[...]))
    lane = lax.broadcasted_iota(jnp.int32, (q, LANE), 1)
    acc_ = jnp.where(lane < SSD_HEADS, _cumsum_rows(a, False), _cumsum_rows(a, True))
    return xs, dt, acc_


def _pair(col, j, off, lane_lo):
    c0 = off + 2 * j
    return jnp.where(lane_lo, col[:, c0:c0 + 1], col[:, c0 + 1:c0 + 2])


def _ssd_state_step(xh, bm, cm, dt, acc_, ex, h_scr, rev, lane_lo):
    q = SSD_Q
    off = SSD_HEADS if rev else 0
    edge = acc_[0:1, :] if rev else acc_[q - 1:q, :]
    wt = jnp.exp(edge - acc_) * dt
    eedge = jnp.exp(edge)
    y_off = []
    per_grp = SSD_HEADS // SSD_GROUPS // 2
    for j in range(SSD_HEADS // 2):
        g = j // per_grp
        hj = h_scr[j]
        y_off.append(_pair(ex, j, off, lane_lo)
                     * jnp.dot(cm[g], hj.astype(BF16), preferred_element_type=F32))
        xt = (xh[j] * _pair(wt, j, off, lane_lo)).astype(BF16)
        h_scr[j] = hj * _pair(eedge, j, off, lane_lo[0:1]) + lax.dot_general(
            bm[g], xt, (((0,), (0,)), ((), ())), preferred_element_type=F32)
    return y_off


def _ssd_fwd_kernel(cur_ref, prev_ref, next_ref, dt_ref, cw_ref, cb_ref, dtb_ref, alog_ref,
                    yp_ref, h_scr, *, nchunks):
    q = SSD_Q
    n = pl.program_id(1)

    @pl.when(n == 0)
    def _():
        h_scr[...] = jnp.zeros_like(h_scr)

    xs, dt, acc_ = _ssd_front(cur_ref, prev_ref, next_ref, dt_ref, cw_ref, cb_ref, dtb_ref,
                              alog_ref, n, nchunks - 1)
    nx = SSD_HEADS // 2
    xh = xs[0:nx]
    bm = [t.astype(BF16) for t in xs[nx:nx + SSD_GROUPS]]
    cm = [t.astype(BF16) for t in xs[nx + SSD_GROUPS:nx + 2 * SSD_GROUPS]]
    lane_lo = lax.broadcasted_iota(jnp.int32, (q, LANE), 1) < SSD_P
    ex = jnp.exp(acc_)

    acc_t = acc_.T
    dt_t = dt.T
    ti = lax.broadcasted_iota(jnp.int32, (q, q), 0)
    si = lax.broadcasted_iota(jnp.int32, (q, q), 1)
    lower = si <= ti
    upper = si >= ti
    nt = (((1,), (1,)), ((), ()))
    heads_per_grp = SSD_HEADS // SSD_GROUPS
    y = []
    for j in range(nx):
        g = (2 * j) // heads_per_grp
        cb = lax.dot_general(cm[g], bm[g], nt, preferred_element_type=F32)
        parts = []
        for h in (2 * j, 2 * j + 1):
            hb = SSD_HEADS + h
            lf = jnp.where(lower, jnp.exp(jnp.minimum(acc_[:, h:h + 1] - acc_t[h:h + 1, :], 0.0)), 0.0)
            lb = jnp.where(upper, jnp.exp(jnp.minimum(acc_[:, hb:hb + 1] - acc_t[hb:hb + 1, :], 0.0)), 0.0)
            mh = (cb * (lf * dt_t[h:h + 1, :] + lb * dt_t[hb:hb + 1, :])).astype(BF16)
            parts.append(jnp.dot(mh, xh[j].astype(BF16), preferred_element_type=F32))
        y.append(jnp.where(lane_lo, parts[0], parts[1]))

    y_off = _ssd_state_step(xh, bm, cm, dt, acc_, ex, h_scr, False, lane_lo)
    for j in range(nx):
        yp_ref[j] = y[j] + y_off[j]


def _ssd_bwd_kernel(cur_ref, prev_ref, next_ref, dt_ref, cw_ref, cb_ref, dtb_ref, alog_ref,
                    yp_ref, z_ref, skip_ref, norm_ref, y_ref, h_scr, *, nchunks):
    q = SSD_Q
    n = pl.program_id(1)

    @pl.when(n == 0)
    def _():
        h_scr[...] = jnp.zeros_like(h_scr)

    xs, dt, acc_ = _ssd_front(cur_ref, prev_ref, next_ref, dt_ref, cw_ref, cb_ref, dtb_ref,
                              alog_ref, nchunks - 1 - n, nchunks - 1)
    nx = SSD_HEADS // 2
    xh = xs[0:nx]
    bm = [t.astype(BF16) for t in xs[nx:nx + SSD_GROUPS]]
    cm = [t.astype(BF16) for t in xs[nx + SSD_GROUPS:nx + 2 * SSD_GROUPS]]
    lane_lo = lax.broadcasted_iota(jnp.int32, (q, LANE), 1) < SSD_P
    ex = jnp.exp(acc_)
    y_off = _ssd_state_step(xh, bm, cm, dt, acc_, ex, h_scr, True, lane_lo)

    ys = []
    ssq = jnp.zeros((q, 1), F32)
    for j in range(nx):
        z = z_ref[j]
        yj = (yp_ref[j] + y_off[j] + skip_ref[j] * xh[j]) * (z * _sigmoid(z))
        ssq = ssq + jnp.sum(yj * yj, axis=-1, keepdims=True)
        ys.append(yj)
    r = lax.rsqrt(ssq * (1.0 / (nx * LANE)) + EPS)
    for j in range(nx):
        y_ref[j] = (ys[j] * r * norm_ref[j]).astype(BF16)


def _ssd(u3, udt, conv_w_l, conv_b_l, dt_bias_l, a_log_l, skip_l, norm_l, bsz, seq):
    q = SSD_Q
    nchunks = seq // q
    nxbc = BLK_MERGE - BLK_SSD_X
    nx = SSD_HEADS // 2
    hb = q // HALO
    tb = (bsz * seq) // HALO

    cw = conv_w_l.astype(F32).reshape(4, nxbc, 1, LANE)
    cb = conv_b_l.astype(F32).reshape(nxbc, 1, LANE)
    dtb = jnp.zeros((1, LANE), F32).at[0, :N_DT].set(dt_bias_l.astype(F32).reshape(-1))
    alog = jnp.zeros((1, LANE), F32).at[0, :N_DT].set(a_log_l.astype(F32).reshape(-1))
    skip = jnp.repeat(skip_l.astype(F32), SSD_P).reshape(nx, 1, LANE)
    norm = norm_l.astype(F32).reshape(nx, 1, LANE)

    def specs(chunk_of):
        def cur(b, n):
            return (BLK_SSD_X // nxbc, b * nchunks + chunk_of(n), 0)

        def prev(b, n):
            return (BLK_SSD_X // nxbc, jnp.maximum((b * nchunks + chunk_of(n)) * hb - 1, 0), 0)

        def nxt(b, n):
            return (BLK_SSD_X // nxbc, jnp.minimum((b * nchunks + chunk_of(n) + 1) * hb, tb - 1), 0)

        return [pl.BlockSpec((nxbc, q, LANE), cur),
                pl.BlockSpec((nxbc, HALO, LANE), prev),
                pl.BlockSpec((nxbc, HALO, LANE), nxt),
                pl.BlockSpec((q, LANE), lambda b, n: (b * nchunks + chunk_of(n), 0)),
                pl.BlockSpec((4, nxbc, 1, LANE), lambda b, n: (0, 0, 0, 0)),
                pl.BlockSpec((nxbc, 1, LANE), lambda b, n: (0, 0, 0)),
                pl.BlockSpec((1, LANE), lambda b, n: (0, 0)),
                pl.BlockSpec((1, LANE), lambda b, n: (0, 0))]

    state = pltpu.VMEM((nx, SSD_N, LANE), F32)
    ident = lambda n: n
    ypart = pl.pallas_call(
        functools.partial(_ssd_fwd_kernel, nchunks=nchunks),
        grid=(bsz, nchunks),
        in_specs=specs(ident),
        out_specs=pl.BlockSpec((nx, q, LANE), lambda b, n: (0, b * nchunks + n, 0)),
        out_shape=jax.ShapeDtypeStruct((nx, bsz * seq, LANE), F32),
        scratch_shapes=[state],
        compiler_params=_cparams(("parallel", "arbitrary")),
    )(u3, u3, u3, udt, cw, cb, dtb, alog)

    flip = lambda n: nchunks - 1 - n
    blk = lambda b, n: (0, b * nchunks + flip(n), 0)
    return pl.pallas_call(
        functools.partial(_ssd_bwd_kernel, nchunks=nchunks),
        grid=(bsz, nchunks),
        in_specs=specs(flip) + [
            pl.BlockSpec((nx, q, LANE), blk),
            pl.BlockSpec((nx, q, LANE), lambda b, n: (BLK_SSD_Z // nx, b * nchunks + flip(n), 0)),
            pl.BlockSpec((nx, 1, LANE), lambda b, n: (0, 0, 0)),
            pl.BlockSpec((nx, 1, LANE), lambda b, n: (0, 0, 0))],
        out_specs=pl.BlockSpec((nx, q, LANE), blk),
        out_shape=jax.ShapeDtypeStruct((nx, bsz * seq, LANE), BF16),
        scratch_shapes=[state],
        compiler_params=_cparams(("parallel", "arbitrary")),
    )(u3, u3, u3, udt, cw, cb, dtb, alog, ypart, u3, skip, norm)


def _cat_blocks(ref, lo, n):
    return jnp.concatenate([ref[lo + k] for k in range(n)], axis=-1)


def _merge_kernel(yhg_ref, yna_ref, ylru_ref, yssd_ref, gt_ref, x_ref, mod_ref,
                  whg_ref, wna_ref, wlru_ref, wssd_ref, wout_ref, o_ref):
    nd = D_MODEL // LANE
    merged = None
    branches = ((yhg_ref, whg_ref), (yna_ref, wna_ref), (ylru_ref, wlru_ref), (yssd_ref, wssd_ref))
    for i, (y_ref, w_ref) in enumerate(branches):
        p = jnp.dot(_cat_blocks(y_ref, 0, y_ref.shape[0]), w_ref[...], preferred_element_type=F32)
        term = _cat_blocks(gt_ref, nd * i, nd) * p
        merged = term if merged is None else merged + term
    out = jnp.dot(merged.astype(BF16), wout_ref[...], preferred_element_type=F32)
    gate = mod_ref[0][:, 2 * D_MODEL:3 * D_MODEL]
    o_ref[...] = x_ref[...] + gate * out


def _merge(y_hg, y_na, y_lru, y_ssd, u3, x2, mod_l, w_hg, w_na, w_lru, w_ssd, w_out, seq, tm=256):
    t = x2.shape[0]
    ngate = N_BLK - BLK_MERGE

    def yspec(arr):
        return pl.BlockSpec((arr.shape[0], tm, LANE), lambda i: (0, i, 0))

    def wspec(arr):
        return pl.BlockSpec(arr.shape, lambda i: (0, 0))

    return pl.pallas_call(
        _merge_kernel,
        grid=(t // tm,),
        in_specs=[yspec(y_hg), yspec(y_na), yspec(y_lru), yspec(y_ssd),
                  pl.BlockSpec((ngate, tm, LANE), lambda i: (BLK_MERGE // ngate, i, 0)),
                  pl.BlockSpec((tm, D_MODEL), lambda i: (i, 0)),
                  pl.BlockSpec((1, 1, 6 * D_MODEL), lambda i: ((i * tm) // seq, 0, 0)),
                  wspec(w_hg), wspec(w_na), wspec(w_lru), wspec(w_ssd), wspec(w_out)],
        out_specs=pl.BlockSpec((tm, D_MODEL), lambda i: (i, 0)),
        out_shape=jax.ShapeDtypeStruct((t, D_MODEL), F32),
        compiler_params=_cparams(("parallel",)),
    )(y_hg, y_na, y_lru, y_ssd, u3, x2, mod_l, w_hg, w_na, w_lru, w_ssd, w_out)


def _ffn_kernel(x_ref, mod_ref, g_ref, w1_ref, w2_ref, gfin_ref, o_ref, h_scr, acc_scr, *, final_norm):
    j = pl.program_id(1)
    nj = pl.num_programs(1)

    @pl.when(j == 0)
    def _():
        x = x_ref[...]
        ms = jnp.mean(x * x, axis=-1, keepdims=True)
        y = x * lax.rsqrt(ms + EPS) * g_ref[...]
        mod = mod_ref[0]
        shift = mod[:, 3 * D_MODEL:4 * D_MODEL]
        scale = mod[:, 4 * D_MODEL:5 * D_MODEL]
        h_scr[...] = (y * (1.0 + scale) + shift).astype(BF16)

    hid = jnp.maximum(jnp.dot(h_scr[...], w1_ref[...], preferred_element_type=F32), 0.0)
    part = jnp.dot((hid * hid).astype(BF16), w2_ref[...], preferred_element_type=F32)

    @pl.when(j == 0)
    def _():
        acc_scr[...] = part

    @pl.when(j > 0)
    def _():
        acc_scr[...] += part

    @pl.when(j == nj - 1)
    def _():
        gate = mod_ref[0][:, 5 * D_MODEL:6 * D_MODEL]
        xo = x_ref[...] + gate * acc_scr[...]
        if final_norm:
            ms = jnp.mean(xo * xo, axis=-1, keepdims=True)
            xo = xo * lax.rsqrt(ms + EPS) * gfin_ref[...]
        o_ref[...] = xo


def _ffn(x2, mod_l, g, w1, w2, g_final, seq, final_norm, tm=1024, tf=1024):
    t = x2.shape[0]
    tm = min(tm, seq)
    return pl.pallas_call(
        functools.partial(_ffn_kernel, final_norm=final_norm),
        grid=(t // tm, D_FF // tf),
        in_specs=[pl.BlockSpec((tm, D_MODEL), lambda i, j: (i, 0)),
                  pl.BlockSpec((1, 1, 6 * D_MODEL), lambda i, j: ((i * tm) // seq, 0, 0)),
                  pl.BlockSpec((1, D_MODEL), lambda i, j: (0, 0)),
                  pl.BlockSpec((D_MODEL, tf), lambda i, j: (0, j)),
                  pl.BlockSpec((tf, D_MODEL), lambda i, j: (j, 0)),
                  pl.BlockSpec((1, D_MODEL), lambda i, j: (0, 0))],
        out_specs=pl.BlockSpec((tm, D_MODEL), lambda i, j: (i, 0)),
        out_shape=jax.ShapeDtypeStruct((t, D_MODEL), F32),
        scratch_shapes=[pltpu.VMEM((tm, D_MODEL), BF16), pltpu.VMEM((tm, D_MODEL), F32)],
        compiler_params=_cparams(("parallel", "arbitrary")),
    )(x2, mod_l, g, w1, w2, g_final)


def _split_w_in(w_in_l):
    w = w_in_l
    main = jnp.concatenate([w[:, :DT_COL0], w[:, DT_COL0 + N_DT:]], axis=1).astype(BF16)
    dt = jnp.zeros((D_MODEL, LANE), BF16).at[:, :N_DT].set(w[:, DT_COL0:DT_COL0 + N_DT].astype(BF16))
    return main, dt


def kernel(x, c, w_ada, b_ada, g_mix, g_ffn, w_in, hg_lb, hg_norm, na_rpb, lru_conv_w, lru_conv_b, lru_gate_w, lru_gate_b, lru_lambda, ssd_conv_w, ssd_conv_b, ssd_dt_bias, ssd_a_log, ssd_skip, ssd_norm, w_br_hg, w_br_na, w_br_lru, w_br_ssd, w_out, w_ff1, w_ff2, g_final):
    bsz, seq, d = x.shape
    depth = w_in.shape[0]
    assert d == D_MODEL and seq % LRU_TC == 0 and seq // GRID_W >= NA_ROWS

    mod = _ada_mod(c, w_ada, b_ada).reshape(depth, bsz, 1, 6 * D_MODEL)
    lbp = _hg_lower_bounds(hg_lb)
    masks = jnp.asarray(_hgrn_masks())
    gfin = g_final.reshape(1, D_MODEL)

    x2 = x.reshape(bsz * seq, D_MODEL)
    for l in range(depth):
        w_main, w_dt = _split_w_in(w_in[l])
        u3, udt = _inproj(x2, mod[l], g_mix[l].reshape(1, D_MODEL), w_main, w_dt, seq)
        y_hg = _hgrn(u3, lbp[:, l], hg_norm[l].reshape(1, -1), masks, bsz, seq)
        y_na = _na(u3, _na_bias_table(na_rpb[l]), bsz, seq)
        gwd, gbd, lamd = _lru_weights(lru_gate_w[l], lru_gate_b[l], lru_lambda[l])
        y_lru = _lru(u3, lru_conv_w[l], lru_conv_b[l], gwd, gbd, lamd, bsz, seq)
        y_ssd = _ssd(u3, udt, ssd_conv_w[l], ssd_conv_b[l], ssd_dt_bias[l], ssd_a_log[l],
                     ssd_skip[l], ssd_norm[l], bsz, seq)
        x2 = _merge(y_hg, y_na, y_lru, y_ssd, u3, x2, mod[l],
                    w_br_hg[l].astype(BF16), w_br_na[l].astype(BF16), w_br_lru[l].astype(BF16),
                    w_br_ssd[l].astype(BF16), w_out[l].astype(BF16), seq)
        x2 = _ffn(x2, mod[l], g_ffn[l].reshape(1, D_MODEL), w_ff1[l].astype(BF16), w_ff2[l].astype(BF16),
                  gfin, seq, final_norm=(l == depth - 1))
    return x2.reshape(bsz, seq, D_MODEL)
```

```python
import functools

import numpy as np
import jax
import jax.numpy as jnp
from jax import lax
from jax.experimental import pallas as pl
from jax.experimental.pallas import tpu as pltpu

F32 = jnp.float32
BF16 = jnp.bfloat16

LANE = 128
D_MODEL = 1024
D_FF = 4 * D_MODEL
GRID_W = 64
EPS = 1e-6

HG_HEADS = 4
HG_CHUNK = 128
HG_LEVELS = (64, 32, 16, 8, 4, 2, 1)

NA_ROWS = 8
NA_COLS = 16
NA_HEADS = 8
NA_DH = 64
NA_G = 8

LRU_WIDTH = 512
LRU_BLOCKS = 8
LRU_BW = 64
LRU_C = 8.0
LRU_CW = 256
LRU_TC = 512

SSD_HEADS = 16
SSD_P = 64
SSD_GROUPS = 4
SSD_N = 128
SSD_Q = 128
HALO = 8

BLK_HG_Q, BLK_HG_ZF, BLK_HG_ZB, BLK_HG_I, BLK_HG_G = 0, 4, 8, 12, 16
BLK_NA_Q, BLK_NA_K, BLK_NA_V = 20, 24, 28
BLK_LRU_X, BLK_LRU_G = 32, 36
BLK_SSD_Z, BLK_SSD_X, BLK_SSD_B, BLK_SSD_C = 40, 48, 56, 60
BLK_MERGE = 64
N_BLK = 96
N_MAIN = N_BLK * LANE
DT_COL0 = 8192
N_DT = 2 * SSD_HEADS

VMEM_LIMIT = 48 * 1024 * 1024


def _cparams(sem):
    return pltpu.CompilerParams(dimension_semantics=sem, vmem_limit_bytes=VMEM_LIMIT)


def _sigmoid(x):
    return 1.0 / (1.0 + jnp.exp(-x))


def _softplus(x):
    return jnp.maximum(x, 0.0) + jnp.log1p(jnp.exp(-jnp.abs(x)))


def _cumsum_rows(x, rev):
    n = x.shape[0]
    rowi = lax.broadcasted_iota(jnp.int32, x.shape, 0)
    d = 1
    while d < n:
        if not rev:
            x = x + jnp.where(rowi >= d, pltpu.roll(x, d, 0), 0.0)
        else:
            x = x + jnp.where(rowi < n - d, pltpu.roll(x, n - d, 0), 0.0)
        d *= 2
    return x


def _ada_kernel(c_ref, w_ref, b_ref, o_ref):
    c = c_ref[...]
    ca = c * _sigmoid(c)
    o_ref[0] = jnp.dot(ca, w_ref[0], preferred_element_type=F32,
                       precision=lax.Precision.HIGHEST) + b_ref[0]


def _ada_mod(c, w_ada, b_ada):
    depth, d, d6 = w_ada.shape
    bsz = c.shape[0]
    rows = 8
    cp = jnp.zeros((rows, d), F32).at[:bsz].set(c)
    out = pl.pallas_call(
        _ada_kernel,
        grid=(depth, d6 // d),
        in_specs=[pl.BlockSpec((rows, d), lambda l, j: (0, 0)),
                  pl.BlockSpec((1, d, d), lambda l, j: (l, 0, j)),
                  pl.BlockSpec((1, 1, d), lambda l, j: (l, 0, j))],
        out_specs=pl.BlockSpec((1, rows, d), lambda l, j: (l, 0, j)),
        out_shape=jax.ShapeDtypeStruct((depth, rows, d6), F32),
        compiler_params=_cparams(("parallel", "parallel")),
    )(cp, w_ada, b_ada.reshape(depth, 1, d6))
    return out[:, :bsz]


def _lb_kernel(hg_ref, o_ref):
    depth = hg_ref.shape[0]
    xs = [hg_ref[l] for l in range(depth)]
    m = xs[0]
    for x in xs[1:]:
        m = jnp.maximum(m, x)
    es = [jnp.exp(x - m) for x in xs]
    tot = es[0]
    for e in es[1:]:
        tot = tot + e
    cs = None
    first = None
    for l in range(depth):
        sm = es[l] / tot
        cs = sm if cs is None else cs + sm
        if l == 0:
            first = cs
        lb = cs - first
        o_ref[0, l] = lb
        o_ref[1, l] = jnp.log(lb)
        o_ref[2, l] = jnp.log1p(-lb)


def _hg_lower_bounds(hg_lb):
    two, depth, k = hg_lb.shape
    return pl.pallas_call(
        _lb_kernel,
        out_shape=jax.ShapeDtypeStruct((3, depth, two, k), F32),
    )(hg_lb.transpose(1, 0, 2))


def _inproj_kernel(x_ref, mod_ref, g_ref, w_ref, wdt_ref, u_ref, udt_ref, h_scr, *, tn, gate_tile0):
    j = pl.program_id(1)

    @pl.when(j == 0)
    def _():
        x = x_ref[...]
        ms = jnp.mean(x * x, axis=-1, keepdims=True)
        y = x * lax.rsqrt(ms + EPS) * g_ref[...]
        mod = mod_ref[0]
        shift = mod[:, 0:D_MODEL]
        scale = mod[:, D_MODEL:2 * D_MODEL]
        hb = (y * (1.0 + scale) + shift).astype(BF16)
        h_scr[...] = hb
        udt_ref[...] = jnp.dot(hb, wdt_ref[...], preferred_element_type=F32)

    acc = jnp.dot(h_scr[...], w_ref[...], preferred_element_type=F32)

    @pl.when(j < gate_tile0)
    def _():
        for k in range(tn // LANE):
            u_ref[k] = acc[:, k * LANE:(k + 1) * LANE]

    @pl.when(j >= gate_tile0)
    def _():
        gt = _sigmoid(acc)
        for k in range(tn // LANE):
            u_ref[k] = gt[:, k * LANE:(k + 1) * LANE]


def _inproj(x2, mod_l, g, w_main, w_dt, seq, tm=1024, tn=1024):
    t = x2.shape[0]
    tm = min(tm, seq)
    nb = tn // LANE
    kern = functools.partial(_inproj_kernel, tn=tn, gate_tile0=BLK_MERGE // nb)
    return pl.pallas_call(
        kern,
        grid=(t // tm, N_MAIN // tn),
        in_specs=[pl.BlockSpec((tm, D_MODEL), lambda i, j: (i, 0)),
                  pl.BlockSpec((1, 1, 6 * D_MODEL), lambda i, j: ((i * tm) // seq, 0, 0)),
                  pl.BlockSpec((1, D_MODEL), lambda i, j: (0, 0)),
                  pl.BlockSpec((D_MODEL, tn), lambda i, j: (0, j)),
                  pl.BlockSpec((D_MODEL, LANE), lambda i, j: (0, 0))],
        out_specs=[pl.BlockSpec((nb, tm, LANE), lambda i, j: (j, i, 0)),
                   pl.BlockSpec((tm, LANE), lambda i, j: (i, 0))],
        out_shape=[jax.ShapeDtypeStruct((N_BLK, t, LANE), F32),
                   jax.ShapeDtypeStruct((t, LANE), F32)],
        scratch_shapes=[pltpu.VMEM((tm, D_MODEL), BF16)],
        compiler_params=_cparams(("parallel", "arbitrary")),
    )(x2, mod_l, g, w_main, w_dt)


def _hgrn_consts():
    c = HG_CHUNK
    nlev = len(HG_LEVELS)
    t = np.arange(c)[:, None]
    r = np.arange(c)[None, :]
    sums = np.zeros((2, 1 + nlev, c, c), np.float32)
    masks = np.zeros((2, nlev + 1, c, c), np.float32)
    sums[0, 0] = r <= t
    sums[1, 0] = r >= t
    for li, m in enumerate(HG_LEVELS):
        grp = t // (2 * m)
        ref = grp * 2 * m + m - 1
        sums[0, 1 + li] = ((t > ref) & (r > ref) & (r <= t)) | ((t <= ref) & (r > t) & (r <= ref))
        ref = grp * 2 * m + m
        sums[1, 1 + li] = ((t < ref) & (r >= t) & (r < ref)) | ((t >= ref) & (r >= ref) & (r < t))
        own = ((t // (2 * m)) == (r // (2 * m))) & ((t % (2 * m)) >= m) & ((r % (2 * m)) < m)
        masks[0, li] = own
        masks[1, li] = own.T
    masks[:, nlev] = np.eye(c)
    return sums.reshape(2, (1 + nlev) * c, c), masks


def _hgrn_chunks(streams, sums_ref, msk_ref):
    c = HG_CHUNK
    nlev = len(HG_LEVELS)
    nt = (((1,), (1,)), ((), ()))

    keys, pieces = [], []
    for (q, z, v, lb, llb, l1m, st, rev) in streams:
        e = jnp.exp(-jnp.abs(z))
        logsig = jnp.minimum(z, 0.0) - jnp.log1p(e)
        keys.append((1.0 - lb) * (jnp.where(z >= 0, e, 1.0) / (1.0 + e)))
        y2 = l1m + logsig
        lf = jnp.maximum(llb, y2) + jnp.log1p(jnp.exp(-jnp.abs(llb - y2)))
        hi = lf.astype(BF16)
        lo = (lf - hi.astype(F32)).astype(BF16)
        pieces.append(jnp.concatenate([hi, lo], axis=1))

    ex_alls = []
    for (q, z, v, lb, llb, l1m, st, rev), p in zip(streams, pieces):
        r = jnp.dot(sums_ref[1 if rev else 0], p, preferred_element_type=F32)
        ex_alls.append(r[:, 0:LANE] + r[:, LANE:2 * LANE])

    accs = []
    for (q, z, v, lb, llb, l1m, st, rev), key in zip(streams, keys):
        d = 1 if rev else 0
        accs.append(lax.dot_general(q.astype(BF16), key.astype(BF16), nt, preferred_element_type=F32)
                    * msk_ref[d, nlev])
    for li in range(nlev):
        for si, ((q, z, v, lb, llb, l1m, st, rev), key) in enumerate(zip(streams, keys)):
            d = 1 if rev else 0
            ex = jnp.exp(ex_alls[si][(1 + li) * c:(2 + li) * c])
            sl = lax.dot_general((q * ex).astype(BF16), (key * ex).astype(BF16), nt,
                                 preferred_element_type=F32)
            accs[si] = accs[si] + sl * msk_ref[d, li]

    outs = []
    for si, ((q, z, v, lb, llb, l1m, st, rev), key) in enumerate(zip(streams, keys)):
        b = ex_alls[si][0:c]
        o = jnp.dot(accs[si].astype(BF16), v.astype(BF16), preferred_element_type=F32)
        qs = (q * jnp.exp(b)).astype(BF16)
        o = o + lax.dot_general(qs, st.astype(BF16), nt, preferred_element_type=F32)
        bedge = b[0:1, :] if rev else b[c - 1:c, :]
        ks_ = (key * jnp.exp(bedge - b)).astype(BF16)
        st_new = st * jnp.exp(bedge) + jnp.dot(v.T.astype(BF16), ks_, preferred_element_type=F32)
        outs.append((o, st_new))
    return outs


def _hgrn_kernel(q_ref, zf_ref, zb_ref, v_ref, g_ref, lbp_ref, norm_ref, sums_ref, msk_ref, y_ref,
                 of_scr, ob_scr, st_scr, *, seq):
    c = HG_CHUNK
    n = seq // c

    def load(ref, r0):
        return ref[0, pl.ds(r0, c), :]

    def scan(i, carry):
        rf = pl.multiple_of(i * c, c)
        rb = pl.multiple_of((n - 1 - i) * c, c)
        fwd = (load(q_ref, rf), load(zf_ref, rf), load(v_ref, rf),
               lbp_ref[0, 0:1, :], lbp_ref[1, 0:1, :], lbp_ref[2, 0:1, :], st_scr[0], False)
        bwd = (load(q_ref, rb), load(zb_ref, rb), load(v_ref, rb),
               lbp_ref[0, 1:2, :], lbp_ref[1, 1:2, :], lbp_ref[2, 1:2, :], st_scr[1], True)
        (o_f, st_f), (o_b, st_b) = _hgrn_chunks([fwd, bwd], sums_ref, msk_ref)
        of_scr[pl.ds(rf, c), :] = o_f
        st_scr[0] = st_f
        ob_scr[pl.ds(rb, c), :] = o_b
        st_scr[1] = st_b
        return carry

    def finish(i, carry):
        r0 = pl.multiple_of(i * c, c)
        o = of_scr[pl.ds(r0, c), :] + ob_scr[pl.ds(r0, c), :]
        o = o * lax.rsqrt(jnp.mean(o * o, axis=-1, keepdims=True) + EPS) * norm_ref[...]
        g = load(g_ref, r0)
        y_ref[0, pl.ds(r0, c), :] = (o * (g * _sigmoid(g))).astype(BF16)
        return carry

    st_scr[...] = jnp.zeros_like(st_scr)
    lax.fori_loop(0, n, scan, 0)
    lax.fori_loop(0, n, finish, 0)


def _hgrn(u3, lbp_l, hg_norm_l, sums, masks, bsz, seq):
    def ublk(base):
        return pl.BlockSpec((1, seq, LANE), lambda b, h, base=base: (base + h, b, 0))

    kern = functools.partial(_hgrn_kernel, seq=seq)
    nlev = len(HG_LEVELS)
    return pl.pallas_call(
        kern,
        grid=(bsz, HG_HEADS),
        in_specs=[ublk(BLK_HG_Q), ublk(BLK_HG_ZF), ublk(BLK_HG_ZB), ublk(BLK_HG_I), ublk(BLK_HG_G),
                  pl.BlockSpec((3, 2, LANE), lambda b, h: (0, 0, h)),
                  pl.BlockSpec((1, LANE), lambda b, h: (0, h)),
                  pl.BlockSpec((2, (1 + nlev) * HG_CHUNK, HG_CHUNK), lambda b, h: (0, 0, 0)),
                  pl.BlockSpec((2, nlev + 1, HG_CHUNK, HG_CHUNK), lambda b, h: (0, 0, 0, 0))],
        out_specs=pl.BlockSpec((1, seq, LANE), lambda b, h: (h, b, 0)),
        out_shape=jax.ShapeDtypeStruct((HG_HEADS, bsz * seq, LANE), BF16),
        scratch_shapes=[pltpu.VMEM((seq, LANE), F32), pltpu.VMEM((seq, LANE), F32),
                        pltpu.VMEM((2, LANE, LANE), F32)],
        compiler_params=_cparams(("parallel", "parallel")),
    )(u3, u3, u3, u3, u3, lbp_l, hg_norm_l, sums, masks)


NA_NEG = -1e30


def _na_bias_table(rpb_l):
    qc = np.arange(GRID_W)[:, None]
    kc = np.arange(GRID_W)[None, :]
    cstart = np.clip(qc - NA_COLS // 2, 0, GRID_W - NA_COLS)
    col_ok = (kc >= cstart) & (kc < cstart + NA_COLS)
    dc = np.clip(kc - qc, 1 - NA_COLS, NA_COLS - 1) + NA_COLS - 1
    onehot = (dc[None] == np.arange(2 * NA_COLS - 1)[:, None, None]).astype(np.float32)
    cols = jnp.einsum('hrc,cqk->hrqk', rpb_l.astype(F32), jnp.asarray(onehot),
                      precision=lax.Precision.HIGHEST)
    cols = jnp.where(col_ok[None, None], cols, NA_NEG)
    tab = jnp.stack([cols[:, NA_ROWS - 1 - s:2 * NA_ROWS - 1 - s] for s in range(NA_ROWS)], axis=1)
    return tab.transpose(0, 1, 3, 2, 4).reshape(NA_HEADS, NA_ROWS, GRID_W, NA_ROWS * GRID_W)


def _na_slot(variant, i):
    half = NA_ROWS // 2
    if variant == 0:
        return max(i - half, 0), min(i, half)
    if variant == 1:
        return i, half
    return (half + i, half) if i < half else (NA_G, i)


def _na_kernel(q_ref, k_ref, v_ref, tab_ref, o_ref, bias_scr, *, seq):
    rows = seq // GRID_W
    ngrp = rows // NA_G
    gq = NA_G * GRID_W
    kwin = 2 * NA_G * GRID_W
    kw = NA_ROWS * GRID_W
    nt = (((1,), (1,)), ((), ()))

    @pl.when(pl.program_id(1) == 0)
    def _():
        neg = jnp.full((GRID_W, LANE), NA_NEG, F32)
        for variant in range(3):
            for i in range(NA_G):
                a0, sft = _na_slot(variant, i)
                base = (a0 // 2) * LANE
                wid = min(kw + LANE, kwin - base)
                for h in range(2):
                    slab = jnp.concatenate([tab_ref[h, sft], neg], axis=1)
                    if a0 % 2:
                        slab = pltpu.roll(slab, GRID_W, 1)
                    pieces = [neg] * (base // LANE) + [slab[:, :wid]] + [neg] * ((kwin - base - wid) // LANE)
                    bias_scr[variant, h, i * GRID_W:(i + 1) * GRID_W, :] = jnp.concatenate(pieces, axis=1)

    head0 = lax.broadcasted_iota(jnp.int32, (gq, LANE), 1) < NA_DH

    def body(j, carry):
        variant = jnp.where(j == 0, 0, jnp.where(j == ngrp - 1, 2, 1))
        k0r = jnp.clip(j * NA_G - NA_ROWS // 2, 0, rows - 2 * NA_G)
        q0 = pl.multiple_of(j * gq, gq)
        k0 = pl.multiple_of(k0r * GRID_W, GRID_W)
        q = q_ref[0, pl.ds(q0, gq), :] * (NA_DH ** -0.5)
        kb = k_ref[0, pl.ds(k0, kwin), :].astype(BF16)
        vb = v_ref[0, pl.ds(k0, kwin), :].astype(BF16)
        outs = []
        for h in range(2):
            sel = head0 if h == 0 else jnp.logical_not(head0)
            qh = jnp.where(sel, q, 0.0).astype(BF16)
            s = lax.dot_general(qh, kb, nt, preferred_element_type=F32) + bias_scr[variant, h]
            m = jnp.max(s, axis=-1, keepdims=True)
            p = jnp.exp(s - m)
            l = jnp.sum(p, axis=-1, keepdims=True)
            outs.append(jnp.dot(p.astype(BF16), vb, preferred_element_type=F32) / l)
        o_ref[0, pl.ds(q0, gq), :] = jnp.where(head0, outs[0], outs[1]).astype(BF16)
        return carry

    lax.fori_loop(0, ngrp, body, 0)


def _na(u3, tab, bsz, seq):
    def ublk(base):
        return pl.BlockSpec((1, seq, LANE), lambda hp, b, base=base: (base + hp, b, 0))

    kern = functools.partial(_na_kernel, seq=seq)
    kw = NA_ROWS * GRID_W
    return pl.pallas_call(
        kern,
        grid=(NA_HEADS // 2, bsz),
        in_specs=[ublk(BLK_NA_Q), ublk(BLK_NA_K), ublk(BLK_NA_V),
                  pl.BlockSpec((2, NA_ROWS, GRID_W, kw), lambda hp, b: (hp, 0, 0, 0))],
        out_specs=pl.BlockSpec((1, seq, LANE), lambda hp, b: (hp, b, 0)),
        out_shape=jax.ShapeDtypeStruct((NA_HEADS // 2, bsz * seq, LANE), BF16),
        scratch_shapes=[pltpu.VMEM((3, 2, NA_G * GRID_W, 2 * NA_G * GRID_W), F32)],
        compiler_params=_cparams(("parallel", "arbitrary")),
    )(u3, u3, u3, tab)


def _gelu_tanh(x):
    return 0.5 * x * (1.0 + jnp.tanh(0.7978845608028654 * (x + 0.044715 * (x * x * x))))


def _lru_kernel(x_ref, gate_ref, cw_ref, cb_ref, gw_ref, gb_ref, lam_ref, y_ref,
                xpad, a_scr, u_scr, hf_scr, *, seq):
    w = LRU_CW
    tc = LRU_TC
    nch = seq // tc
    ngrp = seq // 8
    nb = w // LANE

    zeros8 = jnp.zeros((HALO, w), F32)
    xpad[0:HALO, :] = zeros8
    xpad[seq + HALO:seq + 2 * HALO, :] = zeros8
    for k in range(nb):
        xpad[HALO:seq + HALO, k * LANE:(k + 1) * LANE] = x_ref[k]

    rowi = lax.broadcasted_iota(jnp.int32, (tc, w), 0)
    sub = rowi & 7

    for d in range(2):
        rev = d == 1
        sp = _softplus(-lam_ref[0, d])

        def pass_a(i, carry, d=d, rev=rev, sp=sp):
            r0 = pl.multiple_of(i * tc, tc)
            win = xpad[pl.ds(r0, tc + 2 * HALO), :]
            xf = cb_ref[...]
            for j in range(4):
                lo = HALO - 2 + j
                xf = xf + cw_ref[j:j + 1, :] * win[lo:lo + tc]
            gts = jnp.dot(xf.astype(BF16), gw_ref[0, d], preferred_element_type=F32) + gb_ref[0, d]
            rg = _sigmoid(gts[:, 0:w])
            ig = _sigmoid(gts[:, w:2 * w])
            log_a = (-LRU_C) * rg * sp
            a = jnp.exp(log_a)
            u = jnp.sqrt(-jnp.tanh(log_a) * (a * a + 1.0)) * (ig * xf)
            for dd in (1, 2, 4):
                if not rev:
                    ok = sub >= dd
                    a_s = pltpu.roll(a, dd, 0)
                    u_s = pltpu.roll(u, dd, 0)
                else:
                    ok = sub < 8 - dd
                    a_s = pltpu.roll(a, tc - dd, 0)
                    u_s = pltpu.roll(u, tc - dd, 0)
                u = jnp.where(ok, a * u_s + u, u)
                a = jnp.where(ok, a * a_s, a)
            a_scr[pl.ds(r0, tc), :] = a
            u_scr[pl.ds(r0, tc), :] = u
            return carry

        lax.fori_loop(0, nch, pass_a, 0)

        def pass_b(g, h, rev=rev):
            gi = (ngrp - 1 - g) if rev else g
            r0 = pl.multiple_of(gi * 8, 8)
            hp = h[0:1, :] if rev else h[7:8, :]
            hn = u_scr[pl.ds(r0, 8), :] + a_scr[pl.ds(r0, 8), :] * hp
            if rev:
                u_scr[pl.ds(r0, 8), :] = hn
            else:
                hf_scr[pl.ds(r0, 8), :] = hn
            return hn

        lax.fori_loop(0, ngrp, pass_b, jnp.zeros((8, w), F32), unroll=8)

    def pass_c(i, carry):
        r0 = pl.multiple_of(i * tc, tc)
        hsum = hf_scr[pl.ds(r0, tc), :] + u_scr[pl.ds(r0, tc), :]
        for k in range(nb):
            g = gate_ref[k, pl.ds(r0, tc), :]
            y_ref[k, pl.ds(r0, tc), :] = (hsum[:, k * LANE:(k + 1) * LANE] * _gelu_tanh(g)).astype(BF16)
        return carry

    lax.fori_loop(0, nch, pass_c, 0)


def _lru_weights(gate_w_l, gate_b_l, lam_l):
    ncb = LRU_WIDTH // LRU_CW
    per = LRU_CW // LRU_BW
    gw = gate_w_l.astype(F32)
    eye = jnp.eye(per, dtype=F32)
    blocks = gw.reshape(2, 2, ncb, per, LRU_BW, LRU_BW)
    dense = jnp.einsum('dgcpio,pq->dgcpiqo', blocks, eye).reshape(2, 2, ncb, LRU_CW, LRU_CW)
    dense = dense.transpose(2, 0, 3, 1, 4).reshape(ncb, 2, LRU_CW, 2 * LRU_CW).astype(BF16)
    gb = gate_b_l.astype(F32).reshape(2, 2, ncb, LRU_CW).transpose(2, 0, 1, 3).reshape(ncb, 2, 1, 2 * LRU_CW)
    lam = lam_l.astype(F32).reshape(2, ncb, LRU_CW).transpose(1, 0, 2).reshape(ncb, 2, 1, LRU_CW)
    return dense, gb, lam


def _lru(u3, conv_w_l, conv_b_l, gwd, gbd, lamd, bsz, seq):
    ncb = LRU_WIDTH // LRU_CW
    nb = LRU_CW // LANE
    kern = functools.partial(_lru_kernel, seq=seq)

    def ublk(base):
        return pl.BlockSpec((nb, seq, LANE), lambda b, cb, base=base: (base // nb + cb, b, 0))

    return pl.pallas_call(
        kern,
        grid=(bsz, ncb),
        in_specs=[ublk(BLK_LRU_X), ublk(BLK_LRU_G),
                  pl.BlockSpec((4, LRU_CW), lambda b, cb: (0, cb)),
                  pl.BlockSpec((1, LRU_CW), lambda b, cb: (0, cb)),
                  pl.BlockSpec((1, 2, LRU_CW, 2 * LRU_CW), lambda b, cb: (cb, 0, 0, 0)),
                  pl.BlockSpec((1, 2, 1, 2 * LRU_CW), lambda b, cb: (cb, 0, 0, 0)),
                  pl.BlockSpec((1, 2, 1, LRU_CW), lambda b, cb: (cb, 0, 0, 0))],
        out_specs=pl.BlockSpec((nb, seq, LANE), lambda b, cb: (cb, b, 0)),
        out_shape=jax.ShapeDtypeStruct((LRU_WIDTH // LANE, bsz * seq, LANE), BF16),
        scratch_shapes=[pltpu.VMEM((seq + 2 * HALO, LRU_CW), F32),
                        pltpu.VMEM((seq, LRU_CW), F32),
                        pltpu.VMEM((seq, LRU_CW), F32),
                        pltpu.VMEM((seq, LRU_CW), F32)],
        compiler_params=_cparams(("parallel", "parallel")),
    )(u3, u3, conv_w_l, conv_b_l.reshape(1, LRU_WIDTH), gwd, gbd, lamd)


def _ssd_front(cur_ref, prev_ref, next_ref, dt_ref, cw_ref, cb_ref, dtb_ref, alog_ref, n, nlast):
    q = SSD_Q
    nblk = cur_ref.shape[0]
    keep_prev = (n > 0).astype(F32)
    keep_next = (n < nlast).astype(F32)
    xs = []
    for k in range(nblk):
        full = jnp.concatenate([prev_ref[k] * keep_prev, cur_ref[k], next_ref[k] * keep_next], axis=0)
        acc = cb_ref[k]
        for j in range(4):
            lo = HALO - 2 + j
            acc = acc + cw_ref[j, k] * full[lo:lo + q]
        xs.append(acc * _sigmoid(acc))
    dt = _softplus(dt_ref[...] + dtb_ref[...])
    a = dt * (-jnp.exp(alog_ref[...]))
    lane = lax.broadcasted_iota(jnp.int32, (q, LANE), 1)
    acc_ = jnp.where(lane < SSD_HEADS, _cumsum_rows(a, False), _cumsum_rows(a, True))
    return xs, dt, acc_


def _pair(col, j, off, lane_lo):
    c0 = off + 2 * j
    return jnp.where(lane_lo, col[:, c0:c0 + 1], col[:, c0 + 1:c0 + 2])


def _ssd_state_step(xh, bm, cm, dt, acc_, ex, h_scr, rev, lane_lo):
    q = SSD_Q
    off = SSD_HEADS if rev else 0
    edge = acc_[0:1, :] if rev else acc_[q - 1:q, :]
    wt = jnp.exp(edge - acc_) * dt
    eedge = jnp.exp(edge)
    y_off = []
    per_grp = SSD_HEADS // SSD_GROUPS // 2
    for j in range(SSD_HEADS // 2):
        g = j // per_grp
        hj = h_scr[j]
        y_off.append(_pair(ex, j, off, lane_lo)
                     * jnp.dot(cm[g], hj.astype(BF16), preferred_element_type=F32))
        xt = (xh[j] * _pair(wt, j, off, lane_lo)).astype(BF16)
        h_scr[j] = hj * _pair(eedge, j, off, lane_lo[0:1]) + lax.dot_general(
            bm[g], xt, (((0,), (0,)), ((), ())), preferred_element_type=F32)
    return y_off


def _ssd_fwd_kernel(cur_ref, prev_ref, next_ref, dt_ref, cw_ref, cb_ref, dtb_ref, alog_ref,
                    yp_ref, h_scr, *, nchunks):
    q = SSD_Q
    n = pl.program_id(1)

    @pl.when(n == 0)
    def _():
        h_scr[...] = jnp.zeros_like(h_scr)

    xs, dt, acc_ = _ssd_front(cur_ref, prev_ref, next_ref, dt_ref, cw_ref, cb_ref, dtb_ref,
                              alog_ref, n, nchunks - 1)
    nx = SSD_HEADS // 2
    xh = xs[0:nx]
    bm = [t.astype(BF16) for t in xs[nx:nx + SSD_GROUPS]]
    cm = [t.astype(BF16) for t in xs[nx + SSD_GROUPS:nx + 2 * SSD_GROUPS]]
    lane_lo = lax.broadcasted_iota(jnp.int32, (q, LANE), 1) < SSD_P
    ex = jnp.exp(acc_)

    acc_t = acc_.T
    dt_t = dt.T
    ti = lax.broadcasted_iota(jnp.int32, (q, q), 0)
    si = lax.broadcasted_iota(jnp.int32, (q, q), 1)
    lower = si <= ti
    upper = si >= ti
    nt = (((1,), (1,)), ((), ()))
    heads_per_grp = SSD_HEADS // SSD_GROUPS
    y = []
    for j in range(nx):
        g = (2 * j) // heads_per_grp
        cb = lax.dot_general(cm[g], bm[g], nt, preferred_element_type=F32)
        parts = []
        for h in (2 * j, 2 * j + 1):
            hb = SSD_HEADS + h
            lf = jnp.where(lower, jnp.exp(jnp.minimum(acc_[:, h:h + 1] - acc_t[h:h + 1, :], 0.0)), 0.0)
            lb = jnp.where(upper, jnp.exp(jnp.minimum(acc_[:, hb:hb + 1] - acc_t[hb:hb + 1, :], 0.0)), 0.0)
            mh = (cb * (lf * dt_t[h:h + 1, :] + lb * dt_t[hb:hb + 1, :])).astype(BF16)
            parts.append(jnp.dot(mh, xh[j].astype(BF16), preferred_element_type=F32))
        y.append(jnp.where(lane_lo, parts[0], parts[1]))

    y_off = _ssd_state_step(xh, bm, cm, dt, acc_, ex, h_scr, False, lane_lo)
    for j in range(nx):
        yp_ref[j] = y[j] + y_off[j]


def _ssd_bwd_kernel(cur_ref, prev_ref, next_ref, dt_ref, cw_ref, cb_ref, dtb_ref, alog_ref,
                    yp_ref, z_ref, skip_ref, norm_ref, y_ref, h_scr, *, nchunks):
    q = SSD_Q
    n = pl.program_id(1)

    @pl.when(n == 0)
    def _():
        h_scr[...] = jnp.zeros_like(h_scr)

    xs, dt, acc_ = _ssd_front(cur_ref, prev_ref, next_ref, dt_ref, cw_ref, cb_ref, dtb_ref,
                              alog_ref, nchunks - 1 - n, nchunks - 1)
    nx = SSD_HEADS // 2
    xh = xs[0:nx]
    bm = [t.astype(BF16) for t in xs[nx:nx + SSD_GROUPS]]
    cm = [t.astype(BF16) for t in xs[nx + SSD_GROUPS:nx + 2 * SSD_GROUPS]]
    lane_lo = lax.broadcasted_iota(jnp.int32, (q, LANE), 1) < SSD_P
    ex = jnp.exp(acc_)
    y_off = _ssd_state_step(xh, bm, cm, dt, acc_, ex, h_scr, True, lane_lo)

    ys = []
    ssq = jnp.zeros((q, 1), F32)
    for j in range(nx):
        z = z_ref[j]
        yj = (yp_ref[j] + y_off[j] + skip_ref[j] * xh[j]) * (z * _sigmoid(z))
        ssq = ssq + jnp.sum(yj * yj, axis=-1, keepdims=True)
        ys.append(yj)
    r = lax.rsqrt(ssq * (1.0 / (nx * LANE)) + EPS)
    for j in range(nx):
        y_ref[j] = (ys[j] * r * norm_ref[j]).astype(BF16)


def _ssd(u3, udt, conv_w_l, conv_b_l, dt_bias_l, a_log_l, skip_l, norm_l, bsz, seq):
    q = SSD_Q
    nchunks = seq // q
    nxbc = BLK_MERGE - BLK_SSD_X
    nx = SSD_HEADS // 2
    hb = q // HALO
    tb = (bsz * seq) // HALO

    cw = conv_w_l.astype(F32).reshape(4, nxbc, 1, LANE)
    cb = conv_b_l.astype(F32).reshape(nxbc, 1, LANE)
    dtb = jnp.zeros((1, LANE), F32).at[0, :N_DT].set(dt_bias_l.astype(F32).reshape(-1))
    alog = jnp.zeros((1, LANE), F32).at[0, :N_DT].set(a_log_l.astype(F32).reshape(-1))
    skip = jnp.repeat(skip_l.astype(F32), SSD_P).reshape(nx, 1, LANE)
    norm = norm_l.astype(F32).reshape(nx, 1, LANE)

    def specs(chunk_of):
        def cur(b, n):
            return (BLK_SSD_X // nxbc, b * nchunks + chunk_of(n), 0)

        def prev(b, n):
            return (BLK_SSD_X // nxbc, jnp.maximum((b * nchunks + chunk_of(n)) * hb - 1, 0), 0)

        def nxt(b, n):
            return (BLK_SSD_X // nxbc, jnp.minimum((b * nchunks + chunk_of(n) + 1) * hb, tb - 1), 0)

        return [pl.BlockSpec((nxbc, q, LANE), cur),
                pl.BlockSpec((nxbc, HALO, LANE), prev),
                pl.BlockSpec((nxbc, HALO, LANE), nxt),
                pl.BlockSpec((q, LANE), lambda b, n: (b * nchunks + chunk_of(n), 0)),
                pl.BlockSpec((4, nxbc, 1, LANE), lambda b, n: (0, 0, 0, 0)),
                pl.BlockSpec((nxbc, 1, LANE), lambda b, n: (0, 0, 0)),
                pl.BlockSpec((1, LANE), lambda b, n: (0, 0)),
                pl.BlockSpec((1, LANE), lambda b, n: (0, 0))]

    state = pltpu.VMEM((nx, SSD_N, LANE), F32)
    ident = lambda n: n
    ypart = pl.pallas_call(
        functools.partial(_ssd_fwd_kernel, nchunks=nchunks),
        grid=(bsz, nchunks),
        in_specs=specs(ident),
        out_specs=pl.BlockSpec((nx, q, LANE), lambda b, n: (0, b * nchunks + n, 0)),
        out_shape=jax.ShapeDtypeStruct((nx, bsz * seq, LANE), F32),
        scratch_shapes=[state],
        compiler_params=_cparams(("parallel", "arbitrary")),
    )(u3, u3, u3, udt, cw, cb, dtb, alog)

    flip = lambda n: nchunks - 1 - n
    blk = lambda b, n: (0, b * nchunks + flip(n), 0)
    return pl.pallas_call(
        functools.partial(_ssd_bwd_kernel, nchunks=nchunks),
        grid=(bsz, nchunks),
        in_specs=specs(flip) + [
            pl.BlockSpec((nx, q, LANE), blk),
            pl.BlockSpec((nx, q, LANE), lambda b, n: (BLK_SSD_Z // nx, b * nchunks + flip(n), 0)),
            pl.BlockSpec((nx, 1, LANE), lambda b, n: (0, 0, 0)),
            pl.BlockSpec((nx, 1, LANE), lambda b, n: (0, 0, 0))],
        out_specs=pl.BlockSpec((nx, q, LANE), blk),
        out_shape=jax.ShapeDtypeStruct((nx, bsz * seq, LANE), BF16),
        scratch_shapes=[state],
        compiler_params=_cparams(("parallel", "arbitrary")),
    )(u3, u3, u3, udt, cw, cb, dtb, alog, ypart, u3, skip, norm)


def _cat_blocks(ref, lo, n):
    return jnp.concatenate([ref[lo + k] for k in range(n)], axis=-1)


def _merge_kernel(yhg_ref, yna_ref, ylru_ref, yssd_ref, gt_ref, x_ref, mod_ref,
                  whg_ref, wna_ref, wlru_ref, wssd_ref, wout_ref, o_ref):
    nd = D_MODEL // LANE
    merged = None
    branches = ((yhg_ref, whg_ref), (yna_ref, wna_ref), (ylru_ref, wlru_ref), (yssd_ref, wssd_ref))
    for i, (y_ref, w_ref) in enumerate(branches):
        p = jnp.dot(_cat_blocks(y_ref, 0, y_ref.shape[0]), w_ref[...], preferred_element_type=F32)
        term = _cat_blocks(gt_ref, nd * i, nd) * p
        merged = term if merged is None else merged + term
    out = jnp.dot(merged.astype(BF16), wout_ref[...], preferred_element_type=F32)
    gate = mod_ref[0][:, 2 * D_MODEL:3 * D_MODEL]
    o_ref[...] = x_ref[...] + gate * out


def _merge(y_hg, y_na, y_lru, y_ssd, u3, x2, mod_l, w_hg, w_na, w_lru, w_ssd, w_out, seq, tm=256):
    t = x2.shape[0]
    ngate = N_BLK - BLK_MERGE

    def yspec(arr):
        return pl.BlockSpec((arr.shape[0], tm, LANE), lambda i: (0, i, 0))

    def wspec(arr):
        return pl.BlockSpec(arr.shape, lambda i: (0, 0))

    return pl.pallas_call(
        _merge_kernel,
        grid=(t // tm,),
        in_specs=[yspec(y_hg), yspec(y_na), yspec(y_lru), yspec(y_ssd),
                  pl.BlockSpec((ngate, tm, LANE), lambda i: (BLK_MERGE // ngate, i, 0)),
                  pl.BlockSpec((tm, D_MODEL), lambda i: (i, 0)),
                  pl.BlockSpec((1, 1, 6 * D_MODEL), lambda i: ((i * tm) // seq, 0, 0)),
                  wspec(w_hg), wspec(w_na), wspec(w_lru), wspec(w_ssd), wspec(w_out)],
        out_specs=pl.BlockSpec((tm, D_MODEL), lambda i: (i, 0)),
        out_shape=jax.ShapeDtypeStruct((t, D_MODEL), F32),
        compiler_params=_cparams(("parallel",)),
    )(y_hg, y_na, y_lru, y_ssd, u3, x2, mod_l, w_hg, w_na, w_lru, w_ssd, w_out)


def _ffn_kernel(x_ref, mod_ref, g_ref, w1_ref, w2_ref, gfin_ref, o_ref, h_scr, acc_scr, *, final_norm):
    j = pl.program_id(1)
    nj = pl.num_programs(1)

    @pl.when(j == 0)
    def _():
        x = x_ref[...]
        ms = jnp.mean(x * x, axis=-1, keepdims=True)
        y = x * lax.rsqrt(ms + EPS) * g_ref[...]
        mod = mod_ref[0]
        shift = mod[:, 3 * D_MODEL:4 * D_MODEL]
        scale = mod[:, 4 * D_MODEL:5 * D_MODEL]
        h_scr[...] = (y * (1.0 + scale) + shift).astype(BF16)

    hid = jnp.maximum(jnp.dot(h_scr[...], w1_ref[...], preferred_element_type=F32), 0.0)
    part = jnp.dot((hid * hid).astype(BF16), w2_ref[...], preferred_element_type=F32)

    @pl.when(j == 0)
    def _():
        acc_scr[...] = part

    @pl.when(j > 0)
    def _():
        acc_scr[...] += part

    @pl.when(j == nj - 1)
    def _():
        gate = mod_ref[0][:, 5 * D_MODEL:6 * D_MODEL]
        xo = x_ref[...] + gate * acc_scr[...]
        if final_norm:
            ms = jnp.mean(xo * xo, axis=-1, keepdims=True)
            xo = xo * lax.rsqrt(ms + EPS) * gfin_ref[...]
        o_ref[...] = xo


def _ffn(x2, mod_l, g, w1, w2, g_final, seq, final_norm, tm=1024, tf=1024):
    t = x2.shape[0]
    tm = min(tm, seq)
    return pl.pallas_call(
        functools.partial(_ffn_kernel, final_norm=final_norm),
        grid=(t // tm, D_FF // tf),
        in_specs=[pl.BlockSpec((tm, D_MODEL), lambda i, j: (i, 0)),
                  pl.BlockSpec((1, 1, 6 * D_MODEL), lambda i, j: ((i * tm) // seq, 0, 0)),
                  pl.BlockSpec((1, D_MODEL), lambda i, j: (0, 0)),
                  pl.BlockSpec((D_MODEL, tf), lambda i, j: (0, j)),
                  pl.BlockSpec((tf, D_MODEL), lambda i, j: (j, 0)),
                  pl.BlockSpec((1, D_MODEL), lambda i, j: (0, 0))],
        out_specs=pl.BlockSpec((tm, D_MODEL), lambda i, j: (i, 0)),
        out_shape=jax.ShapeDtypeStruct((t, D_MODEL), F32),
        scratch_shapes=[pltpu.VMEM((tm, D_MODEL), BF16), pltpu.VMEM((tm, D_MODEL), F32)],
        compiler_params=_cparams(("parallel", "arbitrary")),
    )(x2, mod_l, g, w1, w2, g_final)


def _split_w_in(w_in_l):
    w = w_in_l
    main = jnp.concatenate([w[:, :DT_COL0], w[:, DT_COL0 + N_DT:]], axis=1).astype(BF16)
    dt = jnp.zeros((D_MODEL, LANE), BF16).at[:, :N_DT].set(w[:, DT_COL0:DT_COL0 + N_DT].astype(BF16))
    return main, dt


def kernel(x, c, w_ada, b_ada, g_mix, g_ffn, w_in, hg_lb, hg_norm, na_rpb, lru_conv_w, lru_conv_b, lru_gate_w, lru_gate_b, lru_lambda, ssd_conv_w, ssd_conv_b, ssd_dt_bias, ssd_a_log, ssd_skip, ssd_norm, w_br_hg, w_br_na, w_br_lru, w_br_ssd, w_out, w_ff1, w_ff2, g_final):
    bsz, seq, d = x.shape
    depth = w_in.shape[0]
    rows = seq // GRID_W
    assert d == D_MODEL and seq % LRU_TC == 0 and rows % NA_G == 0 and rows >= 2 * NA_G

    mod = _ada_mod(c, w_ada, b_ada).reshape(depth, bsz, 1, 6 * D_MODEL)
    lbp = _hg_lower_bounds(hg_lb)
    hg_sums_np, hg_masks_np = _hgrn_consts()
    hg_sums = jnp.asarray(hg_sums_np, BF16)
    hg_masks = jnp.asarray(hg_masks_np)
    gfin = g_final.reshape(1, D_MODEL)

    x2 = x.reshape(bsz * seq, D_MODEL)
    for l in range(depth):
        w_main, w_dt = _split_w_in(w_in[l])
        u3, udt = _inproj(x2, mod[l], g_mix[l].reshape(1, D_MODEL), w_main, w_dt, seq)
        y_hg = _hgrn(u3, lbp[:, l], hg_norm[l].reshape(1, -1), hg_sums, hg_masks, bsz, seq)
        y_na = _na(u3, _na_bias_table(na_rpb[l]), bsz, seq)
        gwd, gbd, lamd = _lru_weights(lru_gate_w[l], lru_gate_b[l], lru_lambda[l])
        y_lru = _lru(u3, lru_conv_w[l], lru_conv_b[l], gwd, gbd, lamd, bsz, seq)
        y_ssd = _ssd(u3, udt, ssd_conv_w[l], ssd_conv_b[l], ssd_dt_bias[l], ssd_a_log[l],
                     ssd_skip[l], ssd_norm[l], bsz, seq)
        x2 = _merge(y_hg, y_na, y_lru, y_ssd, u3, x2, mod[l],
                    w_br_hg[l].astype(BF16), w_br_na[l].astype(BF16), w_br_lru[l].astype(BF16),
                    w_br_ssd[l].astype(BF16), w_out[l].astype(BF16), seq)
        x2 = _ffn(x2, mod[l], g_ffn[l].reshape(1, D_MODEL), w_ff1[l].astype(BF16), w_ff2[l].astype(BF16),
                  gfin, seq, final_norm=(l == depth - 1))
    return x2.reshape(bsz, seq, D_MODEL)
```

```python
import functools

import numpy as np
import jax
import jax.numpy as jnp
from jax import lax
from jax.experimental import pallas as pl
from jax.experimental.pallas import tpu as pltpu

F32 = jnp.float32
BF16 = jnp.bfloat16

LANE = 128
SUB = 8
D_MODEL = 1024
D_FF = 4 * D_MODEL
GRID_W = 64
EPS = 1e-6

HG_HEADS = 4
HG_HPS = 2
HG_CHUNK = 128
HG_LEVELS = (64, 32, 16, 8, 4, 2, 1)

NA_ROWS = 8
NA_COLS = 16
NA_HEADS = 8
NA_DH = 64
NA_G = 8

LRU_WIDTH = 512
LRU_BLOCKS = 8
LRU_BW = 64
LRU_C = 8.0
LRU_CW = 256
LRU_TC = 512
LRU_TINY = 1e-37

SSD_HEADS = 16
SSD_P = 64
SSD_GROUPS = 4
SSD_N = 128
SSD_Q = 128
HALO = 16
CONV_ROWS = 128
CONV_TAPS = 4

BLK_MERGE = 0
BLK_HG_Q, BLK_HG_I, BLK_HG_G = 32, 36, 40
BLK_NA_Q, BLK_NA_K, BLK_NA_V = 44, 48, 52
BLK_LRU_X, BLK_LRU_G = 56, 60
BLK_SSD_X, BLK_SSD_B, BLK_SSD_C, BLK_SSD_Z = 64, 72, 76, 80
N_BLK = 88
N_ZBLK = 8
N_GATE = 32
N_MAIN = (N_ZBLK + N_BLK) * LANE
N_DT = 2 * SSD_HEADS

VMEM_LIMIT = 48 * 1024 * 1024


def _cparams(sem):
    return pltpu.CompilerParams(dimension_semantics=sem, vmem_limit_bytes=VMEM_LIMIT)


def _sigmoid(x):
    return 0.5 * jnp.tanh(0.5 * x) + 0.5


def _softplus(x):
    return jnp.maximum(x, 0.0) + jnp.log1p(jnp.exp(-jnp.abs(x)))


def _split_bf16(x):
    hi = x.astype(BF16)
    lo = (x - hi.astype(F32)).astype(BF16)
    return jnp.concatenate([hi, lo], axis=1)


def _join_f32(r):
    half = r.shape[1] // 2
    return r[:, :half] + r[:, half:]


def _conv_shift_matrix():
    r = CONV_ROWS
    offs = [j - CONV_TAPS // 2 for j in range(CONV_TAPS) if j != CONV_TAPS // 2]
    m = np.zeros((len(offs), r, r + 2 * HALO), np.float32)
    for i, o in enumerate(offs):
        m[i, np.arange(r), HALO + np.arange(r) + o] = 1.0
    return m.reshape(len(offs) * r, r + 2 * HALO)


def _conv_taps(win, shift, wts, bias):
    r = CONV_ROWS
    sh = jnp.dot(shift, win, preferred_element_type=F32)
    acc = bias + wts(CONV_TAPS // 2) * win[HALO:HALO + r].astype(F32)
    i = 0
    for j in range(CONV_TAPS):
        if j != CONV_TAPS // 2:
            acc = acc + wts(j) * sh[i * r:(i + 1) * r]
            i += 1
    return acc


def _ada_kernel(c_ref, w_ref, b_ref, o_ref):
    c = c_ref[...]
    ca = c * _sigmoid(c)
    o_ref[0] = jnp.dot(ca, w_ref[0], preferred_element_type=F32,
                       precision=lax.Precision.HIGHEST) + b_ref[0]


def _ada_mod(c, w_ada, b_ada):
    depth, d, d6 = w_ada.shape
    bsz = c.shape[0]
    cp = jnp.zeros((SUB, d), F32).at[:bsz].set(c)
    out = pl.pallas_call(
        _ada_kernel,
        grid=(depth, d6 // d),
        in_specs=[pl.BlockSpec((SUB, d), lambda l, j: (0, 0)),
                  pl.BlockSpec((1, d, d), lambda l, j: (l, 0, j)),
                  pl.BlockSpec((1, 1, d), lambda l, j: (l, 0, j))],
        out_specs=pl.BlockSpec((1, SUB, d), lambda l, j: (l, 0, j)),
        out_shape=jax.ShapeDtypeStruct((depth, SUB, d6), F32),
        compiler_params=_cparams(("parallel", "parallel")),
    )(cp, w_ada, b_ada.reshape(depth, 1, d6))
    return out[:, :bsz]


def _lb_kernel(hg_ref, o_ref):
    depth = hg_ref.shape[0]
    xs = [hg_ref[l] for l in range(depth)]
    m = xs[0]
    for x in xs[1:]:
        m = jnp.maximum(m, x)
    es = [jnp.exp(x - m) for x in xs]
    tot = es[0]
    for e in es[1:]:
        tot = tot + e
    cs = None
    first = None
    for l in range(depth):
        sm = es[l] / tot
        cs = sm if cs is None else cs + sm
        if l == 0:
            first = cs
        lb = cs - first
        o_ref[0, l] = lb
        o_ref[1, l] = jnp.log(lb)
        o_ref[2, l] = jnp.log1p(-lb)


def _hg_lower_bounds(hg_lb):
    two, depth, k = hg_lb.shape
    return pl.pallas_call(
        _lb_kernel,
        out_shape=jax.ShapeDtypeStruct((3, depth, two, k), F32),
    )(hg_lb.transpose(1, 0, 2))


def _inproj_kernel(x_ref, mod_ref, g_ref, w_ref, wdt_ref, uz_ref, u_ref, udt_ref, h_scr, *, tn, gate_tiles):
    j = pl.program_id(1)
    nb = tn // LANE

    @pl.when(j == 0)
    def _():
        x = x_ref[...]
        ms = jnp.mean(x * x, axis=-1, keepdims=True)
        y = x * lax.rsqrt(ms + EPS) * g_ref[...]
        mod = mod_ref[0]
        shift = mod[:, 0:D_MODEL]
        scale = mod[:, D_MODEL:2 * D_MODEL]
        hb = (y * (1.0 + scale) + shift).astype(BF16)
        h_scr[...] = hb
        udt_ref[...] = jnp.dot(hb, wdt_ref[...], preferred_element_type=F32)

    acc = jnp.dot(h_scr[...], w_ref[...], preferred_element_type=F32)

    @pl.when(j == 0)
    def _():
        for k in range(nb):
            uz_ref[k] = acc[:, k * LANE:(k + 1) * LANE]

    @pl.when(jnp.logical_and(j >= 1, j <= gate_tiles))
    def _():
        gt = _sigmoid(acc).astype(BF16)
        for k in range(nb):
            u_ref[k] = gt[:, k * LANE:(k + 1) * LANE]

    @pl.when(j > gate_tiles)
    def _():
        ab = acc.astype(BF16)
        for k in range(nb):
            u_ref[k] = ab[:, k * LANE:(k + 1) * LANE]


def _inproj(x2, mod_l, g, w_main, w_dt, seq, tm=1024, tn=1024):
    t = x2.shape[0]
    tm = min(tm, seq)
    nb = tn // LANE
    assert nb == N_ZBLK and N_GATE % nb == 0
    kern = functools.partial(_inproj_kernel, tn=tn, gate_tiles=N_GATE // nb)
    return pl.pallas_call(
        kern,
        grid=(t // tm, N_MAIN // tn),
        in_specs=[pl.BlockSpec((tm, D_MODEL), lambda i, j: (i, 0)),
                  pl.BlockSpec((1, 1, 6 * D_MODEL), lambda i, j: ((i * tm) // seq, 0, 0)),
                  pl.BlockSpec((1, D_MODEL), lambda i, j: (0, 0)),
                  pl.BlockSpec((D_MODEL, tn), lambda i, j: (0, j)),
                  pl.BlockSpec((D_MODEL, LANE), lambda i, j: (0, 0))],
        out_specs=[pl.BlockSpec((nb, tm, LANE), lambda i, j: (0, i, 0)),
                   pl.BlockSpec((nb, tm, LANE), lambda i, j: (jnp.maximum(j - 1, 0), i, 0)),
                   pl.BlockSpec((tm, LANE), lambda i, j: (i, 0))],
        out_shape=[jax.ShapeDtypeStruct((N_ZBLK, t, LANE), F32),
                   jax.ShapeDtypeStruct((N_BLK, t, LANE), BF16),
                   jax.ShapeDtypeStruct((t, LANE), F32)],
        scratch_shapes=[pltpu.VMEM((tm, D_MODEL), BF16)],
        compiler_params=_cparams(("parallel", "arbitrary")),
    )(x2, mod_l, g, w_main, w_dt)


def _hgrn_consts():
    c = HG_CHUNK
    nlev = len(HG_LEVELS)
    t = np.arange(c)[:, None]
    r = np.arange(c)[None, :]
    sums = np.zeros((2, 1 + nlev, c, c), np.float32)
    masks = np.zeros((2, nlev + 1, c, c), np.float32)
    sums[0, 0] = r <= t
    sums[1, 0] = r >= t
    for li, m in enumerate(HG_LEVELS):
        grp = t // (2 * m)
        ref = grp * 2 * m + m - 1
        sums[0, 1 + li] = ((t > ref) & (r > ref) & (r <= t)) | ((t <= ref) & (r > t) & (r <= ref))
        ref = grp * 2 * m + m
        sums[1, 1 + li] = ((t < ref) & (r >= t) & (r < ref)) | ((t >= ref) & (r >= ref) & (r < t))
        own = ((t // (2 * m)) == (r // (2 * m))) & ((t % (2 * m)) >= m) & ((r % (2 * m)) < m)
        masks[0, li] = own
        masks[1, li] = own.T
    masks[:, nlev] = np.eye(c)
    return sums.reshape(2, (1 + nlev) * c, c), masks


def _hgrn_chunks(streams, sums_ref, msk_ref):
    c = HG_CHUNK
    nlev = len(HG_LEVELS)
    nt = (((1,), (1,)), ((), ()))

    keys, pieces = [], []
    for (q, z, v, lb, llb, l1m, st, rev) in streams:
        logsig = jnp.minimum(z, 0.0) - jnp.log(1.0 + jnp.exp(-jnp.abs(z)))
        keys.append((1.0 - lb) * jnp.exp(logsig - z))
        y2 = l1m + logsig
        lf = jnp.maximum(llb, y2) + jnp.log(1.0 + jnp.exp(-jnp.abs(llb - y2)))
        pieces.append(_split_bf16(lf))

    ex_alls = []
    for (q, z, v, lb, llb, l1m, st, rev), p in zip(streams, pieces):
        ex_alls.append(_join_f32(jnp.dot(sums_ref[1 if rev else 0], p, preferred_element_type=F32)))

    qfs = [s[0].astype(F32) for s in streams]
    accs = []
    for (q, z, v, lb, llb, l1m, st, rev), key in zip(streams, keys):
        d = 1 if rev else 0
        accs.append(lax.dot_general(q, key.astype(BF16), nt, preferred_element_type=F32) * msk_ref[d, nlev])
    for li in range(nlev):
        for si, ((q, z, v, lb, llb, l1m, st, rev), key) in enumerate(zip(streams, keys)):
            d = 1 if rev else 0
            ex = jnp.exp(ex_alls[si][(1 + li) * c:(2 + li) * c])
            sl = lax.dot_general((qfs[si] * ex).astype(BF16), (key * ex).astype(BF16), nt,
                                 preferred_element_type=F32)
            accs[si] = accs[si] + sl * msk_ref[d, li]

    outs = []
    for si, ((q, z, v, lb, llb, l1m, st, rev), key) in enumerate(zip(streams, keys)):
        b = ex_alls[si][0:c]
        o = jnp.dot(accs[si].astype(BF16), v, preferred_element_type=F32)
        qs = (qfs[si] * jnp.exp(b)).astype(BF16)
        o = o + lax.dot_general(qs, st.astype(BF16), nt, preferred_element_type=F32)
        bedge = b[0:1, :] if rev else b[c - 1:c, :]
        ks_ = (key * jnp.exp(bedge - b)).astype(BF16)
        vt = v.astype(F32).T.astype(BF16)
        st_new = st * jnp.exp(bedge) + jnp.dot(vt, ks_, preferred_element_type=F32)
        outs.append((o, st_new))
    return outs


def _hgrn_kernel(q_ref, zf_ref, zb_ref, v_ref, g_ref, lbp_ref, norm_ref, sums_ref, msk_ref, y_ref,
                 o_scr, st_scr, *, seq):
    c = HG_CHUNK
    n = seq // c
    half = n // 2

    def lanes(h):
        return slice(h * LANE, (h + 1) * LANE)

    def step(i, finalize):
        rf = pl.multiple_of(i * c, c)
        rb = pl.multiple_of((n - 1 - i) * c, c)
        streams = []
        for h in range(HG_HPS):
            for rev, r0, z_ref in ((False, rf, zf_ref), (True, rb, zb_ref)):
                d = 1 if rev else 0
                streams.append((q_ref[h, pl.ds(r0, c), :], z_ref[h, pl.ds(r0, c), :], v_ref[h, pl.ds(r0, c), :],
                                lbp_ref[0, d:d + 1, lanes(h)], lbp_ref[1, d:d + 1, lanes(h)],
                                lbp_ref[2, d:d + 1, lanes(h)], st_scr[2 * h + d], rev))
        outs = _hgrn_chunks(streams, sums_ref, msk_ref)
        for h in range(HG_HPS):
            for d, r0 in ((0, rf), (1, rb)):
                o, st = outs[2 * h + d]
                st_scr[2 * h + d] = st
                if not finalize:
                    o_scr[h, pl.ds(r0, c), :] = o
                else:
                    o = o + o_scr[h, pl.ds(r0, c), :]
                    o = o * lax.rsqrt(jnp.mean(o * o, axis=-1, keepdims=True) + EPS) * norm_ref[:, lanes(h)]
                    g = g_ref[h, pl.ds(r0, c), :].astype(F32)
                    y_ref[h, pl.ds(r0, c), :] = (o * (g * _sigmoid(g))).astype(BF16)

    def first_half(i, carry):
        step(i, False)
        return carry

    def second_half(i, carry):
        step(i, True)
        return carry

    st_scr[...] = jnp.zeros_like(st_scr)
    lax.fori_loop(0, half, first_half, 0)
    lax.fori_loop(half, n, second_half, 0)


def _hgrn(u3, uz, lbp_l, hg_norm_l, sums, masks, bsz, seq):
    hps = HG_HPS
    nlev = len(HG_LEVELS)
    assert (seq // HG_CHUNK) % 2 == 0

    def ublk(base):
        return pl.BlockSpec((hps, seq, LANE), lambda b, hp, base=base: (base // hps + hp, b, 0))

    def zblk(base):
        return pl.BlockSpec((hps, seq, LANE), lambda b, hp, base=base: (base // hps + hp, b, 0))

    kern = functools.partial(_hgrn_kernel, seq=seq)
    return pl.pallas_call(
        kern,
        grid=(bsz, HG_HEADS // hps),
        in_specs=[ublk(BLK_HG_Q), zblk(0), zblk(HG_HEADS), ublk(BLK_HG_I), ublk(BLK_HG_G),
                  pl.BlockSpec((3, 2, hps * LANE), lambda b, hp: (0, 0, hp)),
                  pl.BlockSpec((1, hps * LANE), lambda b, hp: (0, hp)),
                  pl.BlockSpec((2, (1 + nlev) * HG_CHUNK, HG_CHUNK), lambda b, hp: (0, 0, 0)),
                  pl.BlockSpec((2, nlev + 1, HG_CHUNK, HG_CHUNK), lambda b, hp: (0, 0, 0, 0))],
        out_specs=pl.BlockSpec((hps, seq, LANE), lambda b, hp: (hp, b, 0)),
        out_shape=jax.ShapeDtypeStruct((HG_HEADS, bsz * seq, LANE), BF16),
        scratch_shapes=[pltpu.VMEM((hps, seq, LANE), F32), pltpu.VMEM((2 * hps, LANE, LANE), F32)],
        compiler_params=_cparams(("parallel", "parallel")),
    )(u3, uz, uz, u3, u3, lbp_l, hg_norm_l, sums, masks)


NA_NEG = -1e30


def _na_bias_table(rpb_l):
    qc = np.arange(GRID_W)[:, None]
    kc = np.arange(GRID_W)[None, :]
    cstart = np.clip(qc - NA_COLS // 2, 0, GRID_W - NA_COLS)
    col_ok = (kc >= cstart) & (kc < cstart + NA_COLS)
    dc = np.clip(kc - qc, 1 - NA_COLS, NA_COLS - 1) + NA_COLS - 1
    onehot = (dc[None] == np.arange(2 * NA_COLS - 1)[:, None, None]).astype(np.float32)
    cols = jnp.einsum('hrc,cqk->hrqk', rpb_l.astype(F32), jnp.asarray(onehot),
                      precision=lax.Precision.HIGHEST)
    cols = jnp.where(col_ok[None, None], cols, NA_NEG)
    tab = jnp.stack([cols[:, NA_ROWS - 1 - s:2 * NA_ROWS - 1 - s] for s in range(NA_ROWS)], axis=1)
    return tab.transpose(0, 1, 3, 2, 4).reshape(NA_HEADS, NA_ROWS, GRID_W, NA_ROWS * GRID_W)


def _na_slot(variant, i):
    half = NA_ROWS // 2
    if variant == 0:
        return max(i - half, 0), min(i, half)
    if variant == 1:
        return i, half
    return (half + i, half) if i < half else (NA_G, i)


def _na_kernel(q_ref, k_ref, v_ref, tab_ref, o_ref, bias_scr, *, seq):
    rows = seq // GRID_W
    ngrp = rows // NA_G
    gq = NA_G * GRID_W
    kwin = 2 * NA_G * GRID_W
    kw = NA_ROWS * GRID_W
    nt = (((1,), (1,)), ((), ()))

    @pl.when(pl.program_id(1) == 0)
    def _():
        neg = jnp.full((GRID_W, LANE), NA_NEG, F32)
        for variant in range(3):
            for i in range(NA_G):
                a0, sft = _na_slot(variant, i)
                base = (a0 // 2) * LANE
                wid = min(kw + LANE, kwin - base)
                for h in range(2):
                    slab = jnp.concatenate([tab_ref[h, sft], neg], axis=1)
                    if a0 % 2:
                        slab = pltpu.roll(slab, GRID_W, 1)
                    pieces = [neg] * (base // LANE) + [slab[:, :wid]] + [neg] * ((kwin - base - wid) // LANE)
                    bias_scr[variant, h, i * GRID_W:(i + 1) * GRID_W, :] = jnp.concatenate(pieces, axis=1)

    head0 = lax.broadcasted_iota(jnp.int32, (gq, LANE), 1) < NA_DH
    zero = jnp.zeros((gq, LANE), BF16)

    def body(j, carry):
        variant = jnp.where(j == 0, 0, jnp.where(j == ngrp - 1, 2, 1))
        k0r = jnp.clip(j * NA_G - NA_ROWS // 2, 0, rows - 2 * NA_G)
        q0 = pl.multiple_of(j * gq, gq)
        k0 = pl.multiple_of(k0r * GRID_W, GRID_W)
        q = q_ref[0, pl.ds(q0, gq), :] * (NA_DH ** -0.5)
        kb = k_ref[0, pl.ds(k0, kwin), :]
        vb = v_ref[0, pl.ds(k0, kwin), :]
        outs = []
        for h in range(2):
            qh = jnp.where(head0, q, zero) if h == 0 else jnp.where(head0, zero, q)
            s = lax.dot_general(qh, kb, nt, preferred_element_type=F32) + bias_scr[variant, h]
            m = jnp.max(s, axis=-1, keepdims=True)
            p = jnp.exp(s - m)
            l = jnp.sum(p, axis=-1, keepdims=True)
            outs.append(jnp.dot(p.astype(BF16), vb, preferred_element_type=F32) / l)
        o_ref[0, pl.ds(q0, gq), :] = jnp.where(head0, outs[0], outs[1]).astype(BF16)
        return carry

    lax.fori_loop(0, ngrp, body, 0)


def _na(u3, tab, bsz, seq):
    def ublk(base):
        return pl.BlockSpec((1, seq, LANE), lambda hp, b, base=base: (base + hp, b, 0))

    kern = functools.partial(_na_kernel, seq=seq)
    kw = NA_ROWS * GRID_W
    return pl.pallas_call(
        kern,
        grid=(NA_HEADS // 2, bsz),
        in_specs=[ublk(BLK_NA_Q), ublk(BLK_NA_K), ublk(BLK_NA_V),
                  pl.BlockSpec((2, NA_ROWS, GRID_W, kw), lambda hp, b: (hp, 0, 0, 0))],
        out_specs=pl.BlockSpec((1, seq, LANE), lambda hp, b: (hp, b, 0)),
        out_shape=jax.ShapeDtypeStruct((NA_HEADS // 2, bsz * seq, LANE), BF16),
        scratch_shapes=[pltpu.VMEM((3, 2, NA_G * GRID_W, 2 * NA_G * GRID_W), F32)],
        compiler_params=_cparams(("parallel", "arbitrary")),
    )(u3, u3, u3, tab)


def _gelu_tanh(x):
    return 0.5 * x * (1.0 + jnp.tanh(0.7978845608028654 * (x + 0.044715 * (x * x * x))))


def _lru_kernel(x_ref, gate_ref, cw_ref, cb_ref, gw_ref, gb_ref, lam_ref, shift_ref, y_ref,
                xpad, xf_scr, a_scr, u_scr, hf_scr, *, seq):
    w = LRU_CW
    tc = LRU_TC
    nch = seq // tc
    ngrp = seq // SUB
    nb = w // LANE
    r = CONV_ROWS

    zpad = jnp.zeros((HALO, w), BF16)
    xpad[0:HALO, :] = zpad
    xpad[seq + HALO:seq + 2 * HALO, :] = zpad
    for k in range(nb):
        xpad[HALO:seq + HALO, k * LANE:(k + 1) * LANE] = x_ref[k]

    sub = lax.broadcasted_iota(jnp.int32, (tc // SUB, SUB, w), 1)

    for d in range(2):
        rev = d == 1
        sp = _softplus(-lam_ref[0, d])

        def pass_a(i, carry, d=d, rev=rev, sp=sp):
            r0 = pl.multiple_of(i * tc, tc)
            if not rev:
                win = xpad[pl.ds(r0, tc + 2 * HALO), :]
                parts = [_conv_taps(win[s * r:(s + 1) * r + 2 * HALO], shift_ref[...],
                                    lambda j: cw_ref[j:j + 1, :], cb_ref[...]) for s in range(tc // r)]
                xf = jnp.concatenate(parts, axis=0)
                xf_scr[pl.ds(r0, tc), :] = xf
            else:
                xf = xf_scr[pl.ds(r0, tc), :]
            gts = jnp.dot(xf.astype(BF16), gw_ref[0, d], preferred_element_type=F32) + gb_ref[0, d]
            rg = _sigmoid(gts[:, 0:w])
            ig = _sigmoid(gts[:, w:2 * w])
            log_a = (-LRU_C) * rg * sp
            a = jnp.exp(log_a)
            y = -jnp.tanh(log_a) * (a * a + 1.0)
            u = y * lax.rsqrt(jnp.maximum(y, LRU_TINY)) * (ig * xf)
            a = a.reshape(tc // SUB, SUB, w)
            u = u.reshape(tc // SUB, SUB, w)
            for dd in (1, 2, 4):
                sh = (SUB - dd) if rev else dd
                ok = (sub < SUB - dd) if rev else (sub >= dd)
                a_s = pltpu.roll(a, sh, 1)
                u_s = pltpu.roll(u, sh, 1)
                u = jnp.where(ok, a * u_s + u, u)
                a = jnp.where(ok, a * a_s, a)
            a_scr[pl.ds(r0, tc), :] = a.reshape(tc, w)
            u_scr[pl.ds(r0, tc), :] = u.reshape(tc, w)
            return carry

        lax.fori_loop(0, nch, pass_a, 0)

        def pass_b(g, h, rev=rev):
            gi = (ngrp - 1 - g) if rev else g
            r0 = pl.multiple_of(gi * SUB, SUB)
            hp = h[0:1, :] if rev else h[SUB - 1:SUB, :]
            hn = u_scr[pl.ds(r0, SUB), :] + a_scr[pl.ds(r0, SUB), :] * hp
            if rev:
                u_scr[pl.ds(r0, SUB), :] = hn
            else:
                hf_scr[pl.ds(r0, SUB), :] = hn
            return hn

        lax.fori_loop(0, ngrp, pass_b, jnp.zeros((SUB, w), F32), unroll=8)

    def pass_c(i, carry):
        r0 = pl.multiple_of(i * tc, tc)
        hsum = hf_scr[pl.ds(r0, tc), :] + u_scr[pl.ds(r0, tc), :]
        for k in range(nb):
            g = gate_ref[k, pl.ds(r0, tc), :].astype(F32)
            y_ref[k, pl.ds(r0, tc), :] = (hsum[:, k * LANE:(k + 1) * LANE] * _gelu_tanh(g)).astype(BF16)
        return carry

    lax.fori_loop(0, nch, pass_c, 0)


def _lru_weights(gate_w_l, gate_b_l, lam_l):
    ncb = LRU_WIDTH // LRU_CW
    per = LRU_CW // LRU_BW
    gw = gate_w_l.astype(F32)
    eye = jnp.eye(per, dtype=F32)
    blocks = gw.reshape(2, 2, ncb, per, LRU_BW, LRU_BW)
    dense = jnp.einsum('dgcpio,pq->dgcpiqo', blocks, eye).reshape(2, 2, ncb, LRU_CW, LRU_CW)
    dense = dense.transpose(2, 0, 3, 1, 4).reshape(ncb, 2, LRU_CW, 2 * LRU_CW).astype(BF16)
    gb = gate_b_l.astype(F32).reshape(2, 2, ncb, LRU_CW).transpose(2, 0, 1, 3).reshape(ncb, 2, 1, 2 * LRU_CW)
    lam = lam_l.astype(F32).reshape(2, ncb, LRU_CW).transpose(1, 0, 2).reshape(ncb, 2, 1, LRU_CW)
    return dense, gb, lam


def _lru(u3, conv_w_l, conv_b_l, gwd, gbd, lamd, shift, bsz, seq):
    ncb = LRU_WIDTH // LRU_CW
    nb = LRU_CW // LANE
    kern = functools.partial(_lru_kernel, seq=seq)

    def ublk(base):
        return pl.BlockSpec((nb, seq, LANE), lambda b, cb, base=base: (base // nb + cb, b, 0))

    return pl.pallas_call(
        kern,
        grid=(bsz, ncb),
        in_specs=[ublk(BLK_LRU_X), ublk(BLK_LRU_G),
                  pl.BlockSpec((CONV_TAPS, LRU_CW), lambda b, cb: (0, cb)),
                  pl.BlockSpec((1, LRU_CW), lambda b, cb: (0, cb)),
                  pl.BlockSpec((1, 2, LRU_CW, 2 * LRU_CW), lambda b, cb: (cb, 0, 0, 0)),
                  pl.BlockSpec((1, 2, 1, 2 * LRU_CW), lambda b, cb: (cb, 0, 0, 0)),
                  pl.BlockSpec((1, 2, 1, LRU_CW), lambda b, cb: (cb, 0, 0, 0)),
                  pl.BlockSpec(shift.shape, lambda b, cb: (0, 0))],
        out_specs=pl.BlockSpec((nb, seq, LANE), lambda b, cb: (cb, b, 0)),
        out_shape=jax.ShapeDtypeStruct((LRU_WIDTH // LANE, bsz * seq, LANE), BF16),
        scratch_shapes=[pltpu.VMEM((seq + 2 * HALO, LRU_CW), BF16),
                        pltpu.VMEM((seq, LRU_CW), F32),
                        pltpu.VMEM((seq, LRU_CW), F32),
                        pltpu.VMEM((seq, LRU_CW), F32),
                        pltpu.VMEM((seq, LRU_CW), F32)],
        compiler_params=_cparams(("parallel", "parallel")),
    )(u3, u3, conv_w_l, conv_b_l.reshape(1, LRU_WIDTH), gwd, gbd, lamd, shift)


def _ssd_front(cur_ref, prev_ref, next_ref, dt_ref, cw_ref, cb_ref, dtb_ref, alog_ref, shift_ref,
               tri_ref, n, nlast):
    q = SSD_Q
    nblk = cur_ref.shape[0]
    zero = jnp.zeros((HALO, LANE), BF16)
    wins = []
    for k in range(nblk):
        pv = jnp.where(n > 0, prev_ref[k], zero)
        nx = jnp.where(n < nlast, next_ref[k], zero)
        wins.append(jnp.concatenate([pv, cur_ref[k], nx], axis=0))
    win = jnp.concatenate(wins, axis=1)
    r = CONV_ROWS
    sh = jnp.dot(shift_ref[...], win, preferred_element_type=F32)
    xs = []
    for k in range(nblk):
        cols = slice(k * LANE, (k + 1) * LANE)
        acc = cb_ref[k] + cw_ref[CONV_TAPS // 2, k] * cur_ref[k].astype(F32)
        i = 0
        for j in range(CONV_TAPS):
            if j != CONV_TAPS // 2:
                acc = acc + cw_ref[j, k] * sh[i * r:(i + 1) * r, cols]
                i += 1
        xs.append(acc * _sigmoid(acc))
    dt = _softplus(dt_ref[...] + dtb_ref[...])
    a = dt * (-jnp.exp(alog_ref[...]))
    lane = lax.broadcasted_iota(jnp.int32, (q, LANE), 1)
    ap = _split_bf16(a)
    pre = _join_f32(jnp.dot(tri_ref[0], ap, preferred_element_type=F32))
    suf = _join_f32(jnp.dot(tri_ref[1], ap, preferred_element_type=F32))
    acc_ = jnp.where(lane < SSD_HEADS, pre, suf)
    return xs, dt, acc_


def _pair(col, j, off, lane_lo):
    c0 = off + 2 * j
    return jnp.where(lane_lo, col[:, c0:c0 + 1], col[:, c0 + 1:c0 + 2])


def _ssd_state_step(xh, bm, cm, dt, acc_, ex, h_scr, rev, lane_lo):
    q = SSD_Q
    off = SSD_HEADS if rev else 0
    edge = acc_[0:1, :] if rev else acc_[q - 1:q, :]
    wt = jnp.exp(edge - acc_) * dt
    eedge = jnp.exp(edge)
    y_off = []
    per_grp = SSD_HEADS // SSD_GROUPS // 2
    for j in range(SSD_HEADS // 2):
        g = j // per_grp
        hj = h_scr[j]
        y_off.append(_pair(ex, j, off, lane_lo)
                     * jnp.dot(cm[g], hj.astype(BF16), preferred_element_type=F32))
        xt = (xh[j] * _pair(wt, j, off, lane_lo)).astype(BF16)
        h_scr[j] = hj * _pair(eedge, j, off, lane_lo[0:1]) + lax.dot_general(
            bm[g], xt, (((0,), (0,)), ((), ())), preferred_element_type=F32)
    return y_off


def _ssd_fwd_kernel(cur_ref, prev_ref, next_ref, dt_ref, cw_ref, cb_ref, dtb_ref, alog_ref,
                    shift_ref, tri_ref, yp_ref, h_scr, *, nchunks):
    q = SSD_Q
    n = pl.program_id(1)

    @pl.when(n == 0)
    def _():
        h_scr[...] = jnp.zeros_like(h_scr)

    xs, dt, acc_ = _ssd_front(cur_ref, prev_ref, next_ref, dt_ref, cw_ref, cb_ref, dtb_ref,
                              alog_ref, shift_ref, tri_ref, n, nchunks - 1)
    nx = SSD_HEADS // 2
    xh = xs[0:nx]
    bm = [t.astype(BF16) for t in xs[nx:nx + SSD_GROUPS]]
    cm = [t.astype(BF16) for t in xs[nx + SSD_GROUPS:nx + 2 * SSD_GROUPS]]
    lane_lo = lax.broadcasted_iota(jnp.int32, (q, LANE), 1) < SSD_P
    ex = jnp.exp(acc_)

    acc_t = acc_.T
    dt_t = dt.T
    ti = lax.broadcasted_iota(jnp.int32, (q, q), 0)
    si = lax.broadcasted_iota(jnp.int32, (q, q), 1)
    lower = si <= ti
    upper = si >= ti
    nt = (((1,), (1,)), ((), ()))
    heads_per_grp = SSD_HEADS // SSD_GROUPS
    y = []
    for j in range(nx):
        g = (2 * j) // heads_per_grp
        cb = lax.dot_general(cm[g], bm[g], nt, preferred_element_type=F32)
        parts = []
        for h in (2 * j, 2 * j + 1):
            hb = SSD_HEADS + h
            lf = jnp.where(lower, jnp.exp(acc_[:, h:h + 1] - acc_t[h:h + 1, :]), 0.0)
            lb = jnp.where(upper, jnp.exp(acc_[:, hb:hb + 1] - acc_t[hb:hb + 1, :]), 0.0)
            mh = (cb * (lf * dt_t[h:h + 1, :] + lb * dt_t[hb:hb + 1, :])).astype(BF16)
            parts.append(jnp.dot(mh, xh[j].astype(BF16), preferred_element_type=F32))
        y.append(jnp.where(lane_lo, parts[0], parts[1]))

    y_off = _ssd_state_step(xh, bm, cm, dt, acc_, ex, h_scr, False, lane_lo)
    for j in range(nx):
        yp_ref[j] = y[j] + y_off[j]


def _ssd_bwd_kernel(cur_ref, prev_ref, next_ref, dt_ref, cw_ref, cb_ref, dtb_ref, alog_ref,
                    shift_ref, tri_ref, yp_ref, z_ref, skip_ref, norm_ref, y_ref, h_scr, *, nchunks):
    q = SSD_Q
    n = pl.program_id(1)

    @pl.when(n == 0)
    def _():
        h_scr[...] = jnp.zeros_like(h_scr)

    xs, dt, acc_ = _ssd_front(cur_ref, prev_ref, next_ref, dt_ref, cw_ref, cb_ref, dtb_ref,
                              alog_ref, shift_ref, tri_ref, nchunks - 1 - n, nchunks - 1)
    nx = SSD_HEADS // 2
    xh = xs[0:nx]
    bm = [t.astype(BF16) for t in xs[nx:nx + SSD_GROUPS]]
    cm = [t.astype(BF16) for t in xs[nx + SSD_GROUPS:nx + 2 * SSD_GROUPS]]
    lane_lo = lax.broadcasted_iota(jnp.int32, (q, LANE), 1) < SSD_P
    ex = jnp.exp(acc_)
    y_off = _ssd_state_step(xh, bm, cm, dt, acc_, ex, h_scr, True, lane_lo)

    ys = []
    ssq = jnp.zeros((q, 1), F32)
    for j in range(nx):
        z = z_ref[j].astype(F32)
        yj = (yp_ref[j] + y_off[j] + skip_ref[j] * xh[j]) * (z * _sigmoid(z))
        ssq = ssq + jnp.sum(yj * yj, axis=-1, keepdims=True)
        ys.append(yj)
    r = lax.rsqrt(ssq * (1.0 / (nx * LANE)) + EPS)
    for j in range(nx):
        y_ref[j] = (ys[j] * r * norm_ref[j]).astype(BF16)


def _ssd(u3, udt, conv_w_l, conv_b_l, dt_bias_l, a_log_l, skip_l, norm_l, shift, tri, bsz, seq):
    q = SSD_Q
    nchunks = seq // q
    nxbc = BLK_SSD_Z - BLK_SSD_X
    nx = SSD_HEADS // 2
    hb = q // HALO
    tb = (bsz * seq) // HALO

    cw = conv_w_l.astype(F32).reshape(CONV_TAPS, nxbc, 1, LANE)
    cb = conv_b_l.astype(F32).reshape(nxbc, 1, LANE)
    dtb = jnp.zeros((1, LANE), F32).at[0, :N_DT].set(dt_bias_l.astype(F32).reshape(-1))
    alog = jnp.zeros((1, LANE), F32).at[0, :N_DT].set(a_log_l.astype(F32).reshape(-1))
    skip = jnp.repeat(skip_l.astype(F32), SSD_P).reshape(nx, 1, LANE)
    norm = norm_l.astype(F32).reshape(nx, 1, LANE)

    def specs(chunk_of):
        def cur(b, n):
            return (BLK_SSD_X // nxbc, b * nchunks + chunk_of(n), 0)

        def prev(b, n):
            return (BLK_SSD_X // nxbc, jnp.maximum((b * nchunks + chunk_of(n)) * hb - 1, 0), 0)

        def nxt(b, n):
            return (BLK_SSD_X // nxbc, jnp.minimum((b * nchunks + chunk_of(n) + 1) * hb, tb - 1), 0)

        return [pl.BlockSpec((nxbc, q, LANE), cur),
                pl.BlockSpec((nxbc, HALO, LANE), prev),
                pl.BlockSpec((nxbc, HALO, LANE), nxt),
                pl.BlockSpec((q, LANE), lambda b, n: (b * nchunks + chunk_of(n), 0)),
                pl.BlockSpec((CONV_TAPS, nxbc, 1, LANE), lambda b, n: (0, 0, 0, 0)),
                pl.BlockSpec((nxbc, 1, LANE), lambda b, n: (0, 0, 0)),
                pl.BlockSpec((1, LANE), lambda b, n: (0, 0)),
                pl.BlockSpec((1, LANE), lambda b, n: (0, 0)),
                pl.BlockSpec(shift.shape, lambda b, n: (0, 0)),
                pl.BlockSpec(tri.shape, lambda b, n: (0, 0, 0))]

    state = pltpu.VMEM((nx, SSD_N, LANE), F32)
    ident = lambda n: n
    ypart = pl.pallas_call(
        functools.partial(_ssd_fwd_kernel, nchunks=nchunks),
        grid=(bsz, nchunks),
        in_specs=specs(ident),
        out_specs=pl.BlockSpec((nx, q, LANE), lambda b, n: (0, b * nchunks + n, 0)),
        out_shape=jax.ShapeDtypeStruct((nx, bsz * seq, LANE), F32),
        scratch_shapes=[state],
        compiler_params=_cparams(("parallel", "arbitrary")),
    )(u3, u3, u3, udt, cw, cb, dtb, alog, shift, tri)

    flip = lambda n: nchunks - 1 - n
    blk = lambda b, n: (0, b * nchunks + flip(n), 0)
    return pl.pallas_call(
        functools.partial(_ssd_bwd_kernel, nchunks=nchunks),
        grid=(bsz, nchunks),
        in_specs=specs(flip) + [
            pl.BlockSpec((nx, q, LANE), blk),
            pl.BlockSpec((nx, q, LANE), lambda b, n: (BLK_SSD_Z // nx, b * nchunks + flip(n), 0)),
            pl.BlockSpec((nx, 1, LANE), lambda b, n: (0, 0, 0)),
            pl.BlockSpec((nx, 1, LANE), lambda b, n: (0, 0, 0))],
        out_specs=pl.BlockSpec((nx, q, LANE), blk),
        out_shape=jax.ShapeDtypeStruct((nx, bsz * seq, LANE), BF16),
        scratch_shapes=[state],
        compiler_params=_cparams(("parallel", "arbitrary")),
    )(u3, u3, u3, udt, cw, cb, dtb, alog, shift, tri, ypart, u3, skip, norm)


def _cat_blocks(ref, lo, n):
    return jnp.concatenate([ref[lo + k] for k in range(n)], axis=-1)


def _merge_kernel(yhg_ref, yna_ref, ylru_ref, yssd_ref, gt_ref, x_ref, mod_ref,
                  whg_ref, wna_ref, wlru_ref, wssd_ref, wout_ref, o_ref):
    nd = D_MODEL // LANE
    merged = None
    branches = ((yhg_ref, whg_ref), (yna_ref, wna_ref), (ylru_ref, wlru_ref), (yssd_ref, wssd_ref))
    for i, (y_ref, w_ref) in enumerate(branches):
        p = jnp.dot(_cat_blocks(y_ref, 0, y_ref.shape[0]), w_ref[...], preferred_element_type=F32)
        term = _cat_blocks(gt_ref, nd * i, nd).astype(F32) * p
        merged = term if merged is None else merged + term
    out = jnp.dot(merged.astype(BF16), wout_ref[...], preferred_element_type=F32)
    gate = mod_ref[0][:, 2 * D_MODEL:3 * D_MODEL]
    o_ref[...] = x_ref[...] + gate * out


def _merge(y_hg, y_na, y_lru, y_ssd, u3, x2, mod_l, w_hg, w_na, w_lru, w_ssd, w_out, seq, tm=512):
    t = x2.shape[0]
    tm = min(tm, seq)

    def yspec(arr):
        return pl.BlockSpec((arr.shape[0], tm, LANE), lambda i: (0, i, 0))

    def wspec(arr):
        return pl.BlockSpec(arr.shape, lambda i: (0, 0))

    return pl.pallas_call(
        _merge_kernel,
        grid=(t // tm,),
        in_specs=[yspec(y_hg), yspec(y_na), yspec(y_lru), yspec(y_ssd),
                  pl.BlockSpec((N_GATE, tm, LANE), lambda i: (BLK_MERGE // N_GATE, i, 0)),
                  pl.BlockSpec((tm, D_MODEL), lambda i: (i, 0)),
                  pl.BlockSpec((1, 1, 6 * D_MODEL), lambda i: ((i * tm) // seq, 0, 0)),
                  wspec(w_hg), wspec(w_na), wspec(w_lru), wspec(w_ssd), wspec(w_out)],
        out_specs=pl.BlockSpec((tm, D_MODEL), lambda i: (i, 0)),
        out_shape=jax.ShapeDtypeStruct((t, D_MODEL), F32),
        compiler_params=_cparams(("parallel",)),
    )(y_hg, y_na, y_lru, y_ssd, u3, x2, mod_l, w_hg, w_na, w_lru, w_ssd, w_out)


def _ffn_kernel(x_ref, mod_ref, g_ref, w1_ref, w2_ref, gfin_ref, o_ref, h_scr, acc_scr, *, final_norm):
    j = pl.program_id(1)
    nj = pl.num_programs(1)

    @pl.when(j == 0)
    def _():
        x = x_ref[...]
        ms = jnp.mean(x * x, axis=-1, keepdims=True)
        y = x * lax.rsqrt(ms + EPS) * g_ref[...]
        mod = mod_ref[0]
        shift = mod[:, 3 * D_MODEL:4 * D_MODEL]
        scale = mod[:, 4 * D_MODEL:5 * D_MODEL]
        h_scr[...] = (y * (1.0 + scale) + shift).astype(BF16)

    hid = jnp.maximum(jnp.dot(h_scr[...], w1_ref[...], preferred_element_type=F32), 0.0)
    part = jnp.dot((hid * hid).astype(BF16), w2_ref[...], preferred_element_type=F32)

    @pl.when(j == 0)
    def _():
        acc_scr[...] = part

    @pl.when(j > 0)
    def _():
        acc_scr[...] += part

    @pl.when(j == nj - 1)
    def _():
        gate = mod_ref[0][:, 5 * D_MODEL:6 * D_MODEL]
        xo = x_ref[...] + gate * acc_scr[...]
        if final_norm:
            ms = jnp.mean(xo * xo, axis=-1, keepdims=True)
            xo = xo * lax.rsqrt(ms + EPS) * gfin_ref[...]
        o_ref[...] = xo


def _ffn(x2, mod_l, g, w1, w2, g_final, seq, final_norm, tm=1024, tf=1024):
    t = x2.shape[0]
    tm = min(tm, seq)
    return pl.pallas_call(
        functools.partial(_ffn_kernel, final_norm=final_norm),
        grid=(t // tm, D_FF // tf),
        in_specs=[pl.BlockSpec((tm, D_MODEL), lambda i, j: (i, 0)),
                  pl.BlockSpec((1, 1, 6 * D_MODEL), lambda i, j: ((i * tm) // seq, 0, 0)),
                  pl.BlockSpec((1, D_MODEL), lambda i, j: (0, 0)),
                  pl.BlockSpec((D_MODEL, tf), lambda i, j: (0, j)),
                  pl.BlockSpec((tf, D_MODEL), lambda i, j: (j, 0)),
                  pl.BlockSpec((1, D_MODEL), lambda i, j: (0, 0))],
        out_specs=pl.BlockSpec((tm, D_MODEL), lambda i, j: (i, 0)),
        out_shape=jax.ShapeDtypeStruct((t, D_MODEL), F32),
        scratch_shapes=[pltpu.VMEM((tm, D_MODEL), BF16), pltpu.VMEM((tm, D_MODEL), F32)],
        compiler_params=_cparams(("parallel", "arbitrary")),
    )(x2, mod_l, g, w1, w2, g_final)


W_IN_ORDER = ((512, 1536),
              (8224, 12320),
              (0, 512),
              (1536, 2560),
              (2560, 5120),
              (6144, 8192),
              (5120, 6144))
W_IN_DT = (8192, 8224)


def _split_w_in(w_in_l):
    main = jnp.concatenate([w_in_l[:, a:b] for a, b in W_IN_ORDER], axis=1).astype(BF16)
    dt = jnp.zeros((D_MODEL, LANE), BF16).at[:, :N_DT].set(w_in_l[:, W_IN_DT[0]:W_IN_DT[1]].astype(BF16))
    return main, dt


def kernel(x, c, w_ada, b_ada, g_mix, g_ffn, w_in, hg_lb, hg_norm, na_rpb, lru_conv_w, lru_conv_b, lru_gate_w, lru_gate_b, lru_lambda, ssd_conv_w, ssd_conv_b, ssd_dt_bias, ssd_a_log, ssd_skip, ssd_norm, w_br_hg, w_br_na, w_br_lru, w_br_ssd, w_out, w_ff1, w_ff2, g_final):
    bsz, seq, d = x.shape
    depth = w_in.shape[0]
    rows = seq // GRID_W
    assert d == D_MODEL and seq % LRU_TC == 0 and rows % NA_G == 0 and rows >= 2 * NA_G

    mod = _ada_mod(c, w_ada, b_ada).reshape(depth, bsz, 1, 6 * D_MODEL)
    lbp = _hg_lower_bounds(hg_lb)
    hg_sums_np, hg_masks_np = _hgrn_consts()
    hg_sums = jnp.asarray(hg_sums_np, BF16)
    hg_masks = jnp.asarray(hg_masks_np)
    shift = jnp.asarray(_conv_shift_matrix(), BF16)
    tri = hg_sums[:, :HG_CHUNK, :]
    gfin = g_final.reshape(1, D_MODEL)

    x2 = x.reshape(bsz * seq, D_MODEL)
    for l in range(depth):
        w_main, w_dt = _split_w_in(w_in[l])
        uz, u3, udt = _inproj(x2, mod[l], g_mix[l].reshape(1, D_MODEL), w_main, w_dt, seq)
        y_hg = _hgrn(u3, uz, lbp[:, l], hg_norm[l].reshape(1, -1), hg_sums, hg_masks, bsz, seq)
        y_na = _na(u3, _na_bias_table(na_rpb[l]), bsz, seq)
        gwd, gbd, lamd = _lru_weights(lru_gate_w[l], lru_gate_b[l], lru_lambda[l])
        y_lru = _lru(u3, lru_conv_w[l], lru_conv_b[l], gwd, gbd, lamd, shift, bsz, seq)
        y_ssd = _ssd(u3, udt, ssd_conv_w[l], ssd_conv_b[l], ssd_dt_bias[l], ssd_a_log[l],
                     ssd_skip[l], ssd_norm[l], shift, tri, bsz, seq)
        x2 = _merge(y_hg, y_na, y_lru, y_ssd, u3, x2, mod[l],
                    w_br_hg[l].astype(BF16), w_br_na[l].astype(BF16), w_br_lru[l].astype(BF16),
                    w_br_ssd[l].astype(BF16), w_out[l].astype(BF16), seq)
        x2 = _ffn(x2, mod[l], g_ffn[l].reshape(1, D_MODEL), w_ff1[l].astype(BF16), w_ff2[l].astype(BF16),
                  gfin, seq, final_norm=(l == depth - 1))
    return x2.reshape(bsz, seq, D_MODEL)
```

```python
import functools

import numpy as np
import jax
import jax.numpy as jnp
from jax import lax
from jax.experimental import pallas as pl
from jax.experimental.pallas import tpu as pltpu

F32 = jnp.float32
BF16 = jnp.bfloat16

LANE = 128
SUB = 8
MXU_N = 256
D_MODEL = 1024
D_FF = 4 * D_MODEL
GRID_W = 64
EPS = 1e-6

HG_HEADS = 4
HG_HPS = 2
HG_CHUNK = 128
HG_LEVELS = (64, 32, 16, 8, 4, 2, 1)

NA_ROWS = 8
NA_COLS = 16
NA_HEADS = 8
NA_DH = 64
NA_G = 8

LRU_WIDTH = 512
LRU_BLOCKS = 8
LRU_BW = 64
LRU_C = 8.0
LRU_CW = 256
LRU_TC = 512
LRU_TINY = 1e-37

SSD_HEADS = 16
SSD_P = 64
SSD_GROUPS = 4
SSD_N = 128
SSD_Q = 128
HALO = 16
CONV_ROWS = 128
CONV_TAPS = 4

BLK_MERGE = 0
BLK_HG_Q, BLK_HG_I, BLK_HG_G = 32, 36, 40
BLK_NA_Q, BLK_NA_K, BLK_NA_V = 44, 48, 52
BLK_LRU_X, BLK_LRU_G = 56, 60
BLK_SSD_X, BLK_SSD_B, BLK_SSD_C, BLK_SSD_Z = 64, 72, 76, 80
N_BLK = 88
N_ZBLK = 8
N_GATE = 32
N_MAIN = (N_ZBLK + N_BLK) * LANE
N_DT = 2 * SSD_HEADS

VMEM_LIMIT = 48 * 1024 * 1024


def _cparams(sem):
    return pltpu.CompilerParams(dimension_semantics=sem, vmem_limit_bytes=VMEM_LIMIT)


def _sigmoid(x):
    return 0.5 * jnp.tanh(0.5 * x) + 0.5


def _softplus(x):
    return jnp.maximum(x, 0.0) + jnp.log1p(jnp.exp(-jnp.abs(x)))


def _split_bf16(x):
    hi = x.astype(BF16)
    lo = (x - hi.astype(F32)).astype(BF16)
    return jnp.concatenate([hi, lo], axis=1)


def _join_f32(r):
    half = r.shape[1] // 2
    return r[:, :half] + r[:, half:]


def _conv_shift_matrix():
    r = CONV_ROWS
    offs = [j - CONV_TAPS // 2 for j in range(CONV_TAPS) if j != CONV_TAPS // 2]
    m = np.zeros((len(offs), r, r + 2 * HALO), np.float32)
    for i, o in enumerate(offs):
        m[i, np.arange(r), HALO + np.arange(r) + o] = 1.0
    return m.reshape(len(offs) * r, r + 2 * HALO)


def _conv_taps(win, shift, wts, bias):
    r = CONV_ROWS
    sh = jnp.dot(shift, win, preferred_element_type=F32)
    acc = bias + wts(CONV_TAPS // 2) * win[HALO:HALO + r].astype(F32)
    i = 0
    for j in range(CONV_TAPS):
        if j != CONV_TAPS // 2:
            acc = acc + wts(j) * sh[i * r:(i + 1) * r]
            i += 1
    return acc


def _ada_kernel(c_ref, w_ref, b_ref, o_ref):
    c = c_ref[...]
    ca = c * _sigmoid(c)
    o_ref[0] = jnp.dot(ca, w_ref[0], preferred_element_type=F32,
                       precision=lax.Precision.HIGHEST) + b_ref[0]


def _ada_mod(c, w_ada, b_ada):
    depth, d, d6 = w_ada.shape
    bsz = c.shape[0]
    cp = jnp.zeros((SUB, d), F32).at[:bsz].set(c)
    out = pl.pallas_call(
        _ada_kernel,
        grid=(depth, d6 // d),
        in_specs=[pl.BlockSpec((SUB, d), lambda l, j: (0, 0)),
                  pl.BlockSpec((1, d, d), lambda l, j: (l, 0, j)),
                  pl.BlockSpec((1, 1, d), lambda l, j: (l, 0, j))],
        out_specs=pl.BlockSpec((1, SUB, d), lambda l, j: (l, 0, j)),
        out_shape=jax.ShapeDtypeStruct((depth, SUB, d6), F32),
        compiler_params=_cparams(("parallel", "parallel")),
    )(cp, w_ada, b_ada.reshape(depth, 1, d6))
    return out[:, :bsz]


def _lb_kernel(hg_ref, o_ref):
    depth = hg_ref.shape[0]
    xs = [hg_ref[l] for l in range(depth)]
    m = xs[0]
    for x in xs[1:]:
        m = jnp.maximum(m, x)
    es = [jnp.exp(x - m) for x in xs]
    tot = es[0]
    for e in es[1:]:
        tot = tot + e
    cs = None
    first = None
    for l in range(depth):
        sm = es[l] / tot
        cs = sm if cs is None else cs + sm
        if l == 0:
            first = cs
        lb = cs - first
        o_ref[0, l] = lb
        o_ref[1, l] = jnp.log(lb)
        o_ref[2, l] = jnp.log1p(-lb)


def _hg_lower_bounds(hg_lb):
    two, depth, k = hg_lb.shape
    return pl.pallas_call(
        _lb_kernel,
        out_shape=jax.ShapeDtypeStruct((3, depth, two, k), F32),
    )(hg_lb.transpose(1, 0, 2))


def _inproj_kernel(x_ref, mod_ref, g_ref, w_ref, wdt_ref, uz_ref, u_ref, udt_ref, h_scr, *, tn, gate_tiles):
    j = pl.program_id(1)
    nb = tn // LANE

    @pl.when(j == 0)
    def _():
        x = x_ref[...]
        ms = jnp.mean(x * x, axis=-1, keepdims=True)
        y = x * lax.rsqrt(ms + EPS) * g_ref[...]
        mod = mod_ref[0]
        shift = mod[:, 0:D_MODEL]
        scale = mod[:, D_MODEL:2 * D_MODEL]
        hb = (y * (1.0 + scale) + shift).astype(BF16)
        h_scr[...] = hb
        udt_ref[...] = jnp.dot(hb, wdt_ref[...], preferred_element_type=F32)

    is_gate = jnp.logical_and(j >= 1, j <= gate_tiles)
    hb = h_scr[...]
    per = MXU_N // LANE
    accs = []

    def epilogue(c):
        val = jnp.where(is_gate, _sigmoid(accs[c]), accs[c]).astype(BF16)
        for k in range(per):
            u_ref[c * per + k] = val[:, k * LANE:(k + 1) * LANE]

    for c in range(tn // MXU_N):
        accs.append(jnp.dot(hb, w_ref[:, c * MXU_N:(c + 1) * MXU_N], preferred_element_type=F32))
        if c > 0:
            epilogue(c - 1)
    epilogue(tn // MXU_N - 1)

    @pl.when(j == 0)
    def _():
        for c in range(tn // MXU_N):
            for k in range(per):
                uz_ref[c * per + k] = accs[c][:, k * LANE:(k + 1) * LANE]


def _inproj(x2, mod_l, g, w_main, w_dt, seq, tm=1024, tn=1024):
    t = x2.shape[0]
    tm = min(tm, seq)
    nb = tn // LANE
    assert nb == N_ZBLK and N_GATE % nb == 0
    kern = functools.partial(_inproj_kernel, tn=tn, gate_tiles=N_GATE // nb)
    return pl.pallas_call(
        kern,
        grid=(t // tm, N_MAIN // tn),
        in_specs=[pl.BlockSpec((tm, D_MODEL), lambda i, j: (i, 0)),
                  pl.BlockSpec((1, 1, 6 * D_MODEL), lambda i, j: ((i * tm) // seq, 0, 0)),
                  pl.BlockSpec((1, D_MODEL), lambda i, j: (0, 0)),
                  pl.BlockSpec((D_MODEL, tn), lambda i, j: (0, j)),
                  pl.BlockSpec((D_MODEL, LANE), lambda i, j: (0, 0))],
        out_specs=[pl.BlockSpec((nb, tm, LANE), lambda i, j: (0, i, 0)),
                   pl.BlockSpec((nb, tm, LANE), lambda i, j: (jnp.maximum(j - 1, 0), i, 0)),
                   pl.BlockSpec((tm, LANE), lambda i, j: (i, 0))],
        out_shape=[jax.ShapeDtypeStruct((N_ZBLK, t, LANE), F32),
                   jax.ShapeDtypeStruct((N_BLK, t, LANE), BF16),
                   jax.ShapeDtypeStruct((t, LANE), F32)],
        scratch_shapes=[pltpu.VMEM((tm, D_MODEL), BF16)],
        compiler_params=_cparams(("parallel", "arbitrary")),
    )(x2, mod_l, g, w_main, w_dt)


def _hgrn_consts():
    c = HG_CHUNK
    nlev = len(HG_LEVELS)
    t = np.arange(c)[:, None]
    r = np.arange(c)[None, :]
    sums = np.zeros((2, 1 + nlev, c, c), np.float32)
    masks = np.zeros((2, nlev + 1, c, c), np.float32)
    sums[0, 0] = r <= t
    sums[1, 0] = r >= t
    for li, m in enumerate(HG_LEVELS):
        grp = t // (2 * m)
        ref = grp * 2 * m + m - 1
        sums[0, 1 + li] = ((t > ref) & (r > ref) & (r <= t)) | ((t <= ref) & (r > t) & (r <= ref))
        ref = grp * 2 * m + m
        sums[1, 1 + li] = ((t < ref) & (r >= t) & (r < ref)) | ((t >= ref) & (r >= ref) & (r < t))
        own = ((t // (2 * m)) == (r // (2 * m))) & ((t % (2 * m)) >= m) & ((r % (2 * m)) < m)
        masks[0, li] = own
        masks[1, li] = own.T
    masks[:, nlev] = np.eye(c)
    return sums.reshape(2, (1 + nlev) * c, c), masks


def _hgrn_chunks(streams, sums_ref, msk_ref):
    c = HG_CHUNK
    nlev = len(HG_LEVELS)
    nt = (((1,), (1,)), ((), ()))

    keys, pieces = [], []
    for (q, z, v, lb, llb, l1m, st, rev) in streams:
        logsig = jnp.minimum(z, 0.0) - jnp.log(1.0 + jnp.exp(-jnp.abs(z)))
        keys.append((1.0 - lb) * jnp.exp(logsig - z))
        y2 = l1m + logsig
        lf = jnp.maximum(llb, y2) + jnp.log(1.0 + jnp.exp(-jnp.abs(llb - y2)))
        pieces.append(_split_bf16(lf))

    ex_alls = []
    for (q, z, v, lb, llb, l1m, st, rev), p in zip(streams, pieces):
        ex_alls.append(_join_f32(jnp.dot(sums_ref[1 if rev else 0], p, preferred_element_type=F32)))

    qfs = [s[0].astype(F32) for s in streams]
    accs = []
    for (q, z, v, lb, llb, l1m, st, rev), key in zip(streams, keys):
        d = 1 if rev else 0
        accs.append(lax.dot_general(q, key.astype(BF16), nt, preferred_element_type=F32) * msk_ref[d, nlev])
    for li in range(nlev):
        for si, ((q, z, v, lb, llb, l1m, st, rev), key) in enumerate(zip(streams, keys)):
            d = 1 if rev else 0
            ex = jnp.exp(ex_alls[si][(1 + li) * c:(2 + li) * c])
            sl = lax.dot_general((qfs[si] * ex).astype(BF16), (key * ex).astype(BF16), nt,
                                 preferred_element_type=F32)
            accs[si] = accs[si] + sl * msk_ref[d, li]

    outs = []
    for si, ((q, z, v, lb, llb, l1m, st, rev), key) in enumerate(zip(streams, keys)):
        b = ex_alls[si][0:c]
        o = jnp.dot(accs[si].astype(BF16), v, preferred_element_type=F32)
        qs = (qfs[si] * jnp.exp(b)).astype(BF16)
        o = o + lax.dot_general(qs, st.astype(BF16), nt, preferred_element_type=F32)
        bedge = b[0:1, :] if rev else b[c - 1:c, :]
        ks_ = (key * jnp.exp(bedge - b)).astype(BF16)
        vt = v.astype(F32).T.astype(BF16)
        st_new = st * jnp.exp(bedge) + jnp.dot(vt, ks_, preferred_element_type=F32)
        outs.append((o, st_new))
    return outs


def _hgrn_kernel(q_ref, zf_ref, zb_ref, v_ref, g_ref, lbp_ref, norm_ref, sums_ref, msk_ref, y_ref,
                 o_scr, st_scr, *, seq):
    c = HG_CHUNK
    n = seq // c
    half = n // 2

    def lanes(h):
        return slice(h * LANE, (h + 1) * LANE)

    def step(i, finalize):
        rf = pl.multiple_of(i * c, c)
        rb = pl.multiple_of((n - 1 - i) * c, c)
        streams = []
        for h in range(HG_HPS):
            for rev, r0, z_ref in ((False, rf, zf_ref), (True, rb, zb_ref)):
                d = 1 if rev else 0
                streams.append((q_ref[h, pl.ds(r0, c), :], z_ref[h, pl.ds(r0, c), :], v_ref[h, pl.ds(r0, c), :],
                                lbp_ref[0, d:d + 1, lanes(h)], lbp_ref[1, d:d + 1, lanes(h)],
                                lbp_ref[2, d:d + 1, lanes(h)], st_scr[2 * h + d], rev))
        outs = _hgrn_chunks(streams, sums_ref, msk_ref)
        for h in range(HG_HPS):
            for d, r0 in ((0, rf), (1, rb)):
                o, st = outs[2 * h + d]
                st_scr[2 * h + d] = st
                if not finalize:
                    o_scr[h, pl.ds(r0, c), :] = o
                else:
                    o = o + o_scr[h, pl.ds(r0, c), :]
                    o = o * lax.rsqrt(jnp.mean(o * o, axis=-1, keepdims=True) + EPS) * norm_ref[:, lanes(h)]
                    g = g_ref[h, pl.ds(r0, c), :].astype(F32)
                    y_ref[h, pl.ds(r0, c), :] = (o * (g * _sigmoid(g))).astype(BF16)

    def first_half(i, carry):
        step(i, False)
        return carry

    def second_half(i, carry):
        step(i, True)
        return carry

    st_scr[...] = jnp.zeros_like(st_scr)
    lax.fori_loop(0, half, first_half, 0)
    lax.fori_loop(half, n, second_half, 0)


def _hgrn(u3, uz, lbp_l, hg_norm_l, sums, masks, bsz, seq):
    hps = HG_HPS
    nlev = len(HG_LEVELS)
    assert (seq // HG_CHUNK) % 2 == 0

    def ublk(base):
        return pl.BlockSpec((hps, seq, LANE), lambda b, hp, base=base: (base // hps + hp, b, 0))

    def zblk(base):
        return pl.BlockSpec((hps, seq, LANE), lambda b, hp, base=base: (base // hps + hp, b, 0))

    kern = functools.partial(_hgrn_kernel, seq=seq)
    return pl.pallas_call(
        kern,
        grid=(bsz, HG_HEADS // hps),
        in_specs=[ublk(BLK_HG_Q), zblk(0), zblk(HG_HEADS), ublk(BLK_HG_I), ublk(BLK_HG_G),
                  pl.BlockSpec((3, 2, hps * LANE), lambda b, hp: (0, 0, hp)),
                  pl.BlockSpec((1, hps * LANE), lambda b, hp: (0, hp)),
                  pl.BlockSpec((2, (1 + nlev) * HG_CHUNK, HG_CHUNK), lambda b, hp: (0, 0, 0)),
                  pl.BlockSpec((2, nlev + 1, HG_CHUNK, HG_CHUNK), lambda b, hp: (0, 0, 0, 0))],
        out_specs=pl.BlockSpec((hps, seq, LANE), lambda b, hp: (hp, b, 0)),
        out_shape=jax.ShapeDtypeStruct((HG_HEADS, bsz * seq, LANE), BF16),
        scratch_shapes=[pltpu.VMEM((hps, seq, LANE), F32), pltpu.VMEM((2 * hps, LANE, LANE), F32)],
        compiler_params=_cparams(("parallel", "parallel")),
    )(u3, uz, uz, u3, u3, lbp_l, hg_norm_l, sums, masks)


NA_NEG = -1e30


def _na_bias_table(rpb_l):
    qc = np.arange(GRID_W)[:, None]
    kc = np.arange(GRID_W)[None, :]
    cstart = np.clip(qc - NA_COLS // 2, 0, GRID_W - NA_COLS)
    col_ok = (kc >= cstart) & (kc < cstart + NA_COLS)
    dc = np.clip(kc - qc, 1 - NA_COLS, NA_COLS - 1) + NA_COLS - 1
    onehot = (dc[None] == np.arange(2 * NA_COLS - 1)[:, None, None]).astype(np.float32)
    cols = jnp.einsum('hrc,cqk->hrqk', rpb_l.astype(F32), jnp.asarray(onehot),
                      precision=lax.Precision.HIGHEST)
    cols = jnp.where(col_ok[None, None], cols, NA_NEG)
    tab = jnp.stack([cols[:, NA_ROWS - 1 - s:2 * NA_ROWS - 1 - s] for s in range(NA_ROWS)], axis=1)
    return tab.transpose(0, 1, 3, 2, 4).reshape(NA_HEADS, NA_ROWS, GRID_W, NA_ROWS * GRID_W)


def _na_slot(variant, i):
    half = NA_ROWS // 2
    if variant == 0:
        return max(i - half, 0), min(i, half)
    if variant == 1:
        return i, half
    return (half + i, half) if i < half else (NA_G, i)


def _na_kernel(q_ref, k_ref, v_ref, tab_ref, o_ref, bias_scr, *, seq):
    rows = seq // GRID_W
    ngrp = rows // NA_G
    gq = NA_G * GRID_W
    kwin = 2 * NA_G * GRID_W
    kw = NA_ROWS * GRID_W
    nt = (((1,), (1,)), ((), ()))

    @pl.when(pl.program_id(1) == 0)
    def _():
        neg = jnp.full((GRID_W, LANE), NA_NEG, F32)
        for variant in range(3):
            for i in range(NA_G):
                a0, sft = _na_slot(variant, i)
                base = (a0 // 2) * LANE
                wid = min(kw + LANE, kwin - base)
                for h in range(2):
                    slab = jnp.concatenate([tab_ref[h, sft], neg], axis=1)
                    if a0 % 2:
                        slab = pltpu.roll(slab, GRID_W, 1)
                    pieces = [neg] * (base // LANE) + [slab[:, :wid]] + [neg] * ((kwin - base - wid) // LANE)
                    bias_scr[variant, h, i * GRID_W:(i + 1) * GRID_W, :] = jnp.concatenate(pieces, axis=1)

    head0 = lax.broadcasted_iota(jnp.int32, (gq, LANE), 1) < NA_DH
    zero = jnp.zeros((gq, LANE), BF16)

    def body(j, carry):
        variant = jnp.where(j == 0, 0, jnp.where(j == ngrp - 1, 2, 1))
        k0r = jnp.clip(j * NA_G - NA_ROWS // 2, 0, rows - 2 * NA_G)
        q0 = pl.multiple_of(j * gq, gq)
        k0 = pl.multiple_of(k0r * GRID_W, GRID_W)
        q = q_ref[0, pl.ds(q0, gq), :] * (NA_DH ** -0.5)
        kb = k_ref[0, pl.ds(k0, kwin), :]
        vb = v_ref[0, pl.ds(k0, kwin), :]
        qhs = [jnp.where(head0, q, zero), jnp.where(head0, zero, q)]
        ss = [lax.dot_general(qh, kb, nt, preferred_element_type=F32) for qh in qhs]
        ps, ls = [], []
        for h in range(2):
            s = ss[h] + bias_scr[variant, h]
            p = jnp.exp(s - jnp.max(s, axis=-1, keepdims=True))
            ls.append(jnp.sum(p, axis=-1, keepdims=True))
            ps.append(p.astype(BF16))
        outs = [jnp.dot(p, vb, preferred_element_type=F32) for p in ps]
        o_ref[0, pl.ds(q0, gq), :] = jnp.where(head0, outs[0] / ls[0], outs[1] / ls[1]).astype(BF16)
        return carry

    lax.fori_loop(0, ngrp, body, 0)


def _na(u3, tab, bsz, seq):
    def ublk(base):
        return pl.BlockSpec((1, seq, LANE), lambda hp, b, base=base: (base + hp, b, 0))

    kern = functools.partial(_na_kernel, seq=seq)
    kw = NA_ROWS * GRID_W
    return pl.pallas_call(
        kern,
        grid=(NA_HEADS // 2, bsz),
        in_specs=[ublk(BLK_NA_Q), ublk(BLK_NA_K), ublk(BLK_NA_V),
                  pl.BlockSpec((2, NA_ROWS, GRID_W, kw), lambda hp, b: (hp, 0, 0, 0))],
        out_specs=pl.BlockSpec((1, seq, LANE), lambda hp, b: (hp, b, 0)),
        out_shape=jax.ShapeDtypeStruct((NA_HEADS // 2, bsz * seq, LANE), BF16),
        scratch_shapes=[pltpu.VMEM((3, 2, NA_G * GRID_W, 2 * NA_G * GRID_W), F32)],
        compiler_params=_cparams(("parallel", "arbitrary")),
    )(u3, u3, u3, tab)


def _gelu_tanh(x):
    return 0.5 * x * (1.0 + jnp.tanh(0.7978845608028654 * (x + 0.044715 * (x * x * x))))


def _lru_kernel(x_ref, gate_ref, cw_ref, cb_ref, gw_ref, gb_ref, lam_ref, shift_ref, y_ref,
                xpad, xf_scr, a_scr, u_scr, hf_scr, *, seq):
    w = LRU_CW
    tc = LRU_TC
    nch = seq // tc
    ngrp = seq // SUB
    nb = w // LANE
    r = CONV_ROWS

    zpad = jnp.zeros((HALO, w), BF16)
    xpad[0:HALO, :] = zpad
    xpad[seq + HALO:seq + 2 * HALO, :] = zpad
    for k in range(nb):
        xpad[HALO:seq + HALO, k * LANE:(k + 1) * LANE] = x_ref[k]

    sub = lax.broadcasted_iota(jnp.int32, (tc // SUB, SUB, w), 1)

    for d in range(2):
        rev = d == 1
        sp = _softplus(-lam_ref[0, d])

        def pass_a(i, carry, d=d, rev=rev, sp=sp):
            r0 = pl.multiple_of(i * tc, tc)
            if not rev:
                win = xpad[pl.ds(r0, tc + 2 * HALO), :]
                parts = [_conv_taps(win[s * r:(s + 1) * r + 2 * HALO], shift_ref[...],
                                    lambda j: cw_ref[j:j + 1, :], cb_ref[...]) for s in range(tc // r)]
                xf = jnp.concatenate(parts, axis=0)
                xf_scr[pl.ds(r0, tc), :] = xf
            else:
                xf = xf_scr[pl.ds(r0, tc), :]
            gts = jnp.dot(xf.astype(BF16), gw_ref[0, d], preferred_element_type=F32) + gb_ref[0, d]
            rg = _sigmoid(gts[:, 0:w])
            ig = _sigmoid(gts[:, w:2 * w])
            log_a = (-LRU_C) * rg * sp
            a = jnp.exp(log_a)
            y = -jnp.tanh(log_a) * (a * a + 1.0)
            u = y * lax.rsqrt(jnp.maximum(y, LRU_TINY)) * (ig * xf)
            a = a.reshape(tc // SUB, SUB, w)
            u = u.reshape(tc // SUB, SUB, w)
            for dd in (1, 2, 4):
                sh = (SUB - dd) if rev else dd
                ok = (sub < SUB - dd) if rev else (sub >= dd)
                a_s = pltpu.roll(a, sh, 1)
                u_s = pltpu.roll(u, sh, 1)
                u = jnp.where(ok, a * u_s + u, u)
                a = jnp.where(ok, a * a_s, a)
            a_scr[pl.ds(r0, tc), :] = a.reshape(tc, w)
            u_scr[pl.ds(r0, tc), :] = u.reshape(tc, w)
            return carry

        lax.fori_loop(0, nch, pass_a, 0)

        def pass_b(g, h, rev=rev):
            gi = (ngrp - 1 - g) if rev else g
            r0 = pl.multiple_of(gi * SUB, SUB)
            hp = h[0:1, :] if rev else h[SUB - 1:SUB, :]
            hn = u_scr[pl.ds(r0, SUB), :] + a_scr[pl.ds(r0, SUB), :] * hp
            if rev:
                u_scr[pl.ds(r0, SUB), :] = hn
            else:
                hf_scr[pl.ds(r0, SUB), :] = hn
            return hn

        lax.fori_loop(0, ngrp, pass_b, jnp.zeros((SUB, w), F32), unroll=8)

    def pass_c(i, carry):
        r0 = pl.multiple_of(i * tc, tc)
        hsum = hf_scr[pl.ds(r0, tc), :] + u_scr[pl.ds(r0, tc), :]
        for k in range(nb):
            g = gate_ref[k, pl.ds(r0, tc), :].astype(F32)
            y_ref[k, pl.ds(r0, tc), :] = (hsum[:, k * LANE:(k + 1) * LANE] * _gelu_tanh(g)).astype(BF16)
        return carry

    lax.fori_loop(0, nch, pass_c, 0)


def _lru_weights(gate_w_l, gate_b_l, lam_l):
    ncb = LRU_WIDTH // LRU_CW
    per = LRU_CW // LRU_BW
    gw = gate_w_l.astype(F32)
    eye = jnp.eye(per, dtype=F32)
    blocks = gw.reshape(2, 2, ncb, per, LRU_BW, LRU_BW)
    dense = jnp.einsum('dgcpio,pq->dgcpiqo', blocks, eye).reshape(2, 2, ncb, LRU_CW, LRU_CW)
    dense = dense.transpose(2, 0, 3, 1, 4).reshape(ncb, 2, LRU_CW, 2 * LRU_CW).astype(BF16)
    gb = gate_b_l.astype(F32).reshape(2, 2, ncb, LRU_CW).transpose(2, 0, 1, 3).reshape(ncb, 2, 1, 2 * LRU_CW)
    lam = lam_l.astype(F32).reshape(2, ncb, LRU_CW).transpose(1, 0, 2).reshape(ncb, 2, 1, LRU_CW)
    return dense, gb, lam


def _lru(u3, conv_w_l, conv_b_l, gwd, gbd, lamd, shift, bsz, seq):
    ncb = LRU_WIDTH // LRU_CW
    nb = LRU_CW // LANE
    kern = functools.partial(_lru_kernel, seq=seq)

    def ublk(base):
        return pl.BlockSpec((nb, seq, LANE), lambda b, cb, base=base: (base // nb + cb, b, 0))

    return pl.pallas_call(
        kern,
        grid=(bsz, ncb),
        in_specs=[ublk(BLK_LRU_X), ublk(BLK_LRU_G),
                  pl.BlockSpec((CONV_TAPS, LRU_CW), lambda b, cb: (0, cb)),
                  pl.BlockSpec((1, LRU_CW), lambda b, cb: (0, cb)),
                  pl.BlockSpec((1, 2, LRU_CW, 2 * LRU_CW), lambda b, cb: (cb, 0, 0, 0)),
                  pl.BlockSpec((1, 2, 1, 2 * LRU_CW), lambda b, cb: (cb, 0, 0, 0)),
                  pl.BlockSpec((1, 2, 1, LRU_CW), lambda b, cb: (cb, 0, 0, 0)),
                  pl.BlockSpec(shift.shape, lambda b, cb: (0, 0))],
        out_specs=pl.BlockSpec((nb, seq, LANE), lambda b, cb: (cb, b, 0)),
        out_shape=jax.ShapeDtypeStruct((LRU_WIDTH // LANE, bsz * seq, LANE), BF16),
        scratch_shapes=[pltpu.VMEM((seq + 2 * HALO, LRU_CW), BF16),
                        pltpu.VMEM((seq, LRU_CW), F32),
                        pltpu.VMEM((seq, LRU_CW), F32),
                        pltpu.VMEM((seq, LRU_CW), F32),
                        pltpu.VMEM((seq, LRU_CW), F32)],
        compiler_params=_cparams(("parallel", "parallel")),
    )(u3, u3, conv_w_l, conv_b_l.reshape(1, LRU_WIDTH), gwd, gbd, lamd, shift)


def _ssd_decay(dt_ref, dtb_ref, alog_ref, tri_ref):
    dt = _softplus(dt_ref[...] + dtb_ref[...])
    a = dt * (-jnp.exp(alog_ref[...]))
    lane = lax.broadcasted_iota(jnp.int32, (SSD_Q, LANE), 1)
    ap = _split_bf16(a)
    pre = _join_f32(jnp.dot(tri_ref[0], ap, preferred_element_type=F32))
    suf = _join_f32(jnp.dot(tri_ref[1], ap, preferred_element_type=F32))
    return dt, jnp.where(lane < SSD_HEADS, pre, suf)


def _ssd_conv(cur_ref, prev_ref, next_ref, cw_ref, cb_ref, shift_ref, n, nlast):
    q = SSD_Q
    nblk = cur_ref.shape[0]
    zero = jnp.zeros((HALO, LANE), BF16)
    wins = []
    for k in range(nblk):
        pv = jnp.where(n > 0, prev_ref[k], zero)
        nx = jnp.where(n < nlast, next_ref[k], zero)
        wins.append(jnp.concatenate([pv, cur_ref[k], nx], axis=0))
    win = jnp.concatenate(wins, axis=1)
    r = CONV_ROWS
    sh = jnp.dot(shift_ref[...], win, preferred_element_type=F32)
    xs = []
    for k in range(nblk):
        cols = slice(k * LANE, (k + 1) * LANE)
        acc = cb_ref[k] + cw_ref[CONV_TAPS // 2, k] * cur_ref[k].astype(F32)
        i = 0
        for j in range(CONV_TAPS):
            if j != CONV_TAPS // 2:
                acc = acc + cw_ref[j, k] * sh[i * r:(i + 1) * r, cols]
                i += 1
        xs.append(acc * _sigmoid(acc))
    return xs


def _pair(col, j, off, lane_lo):
    c0 = off + 2 * j
    return jnp.where(lane_lo, col[:, c0:c0 + 1], col[:, c0 + 1:c0 + 2])


def _ssd_state_step(xh, bm, cm, dt, acc_, ex, h_scr, rev, lane_lo):
    q = SSD_Q
    off = SSD_HEADS if rev else 0
    edge = acc_[0:1, :] if rev else acc_[q - 1:q, :]
    wt = jnp.exp(edge - acc_) * dt
    eedge = jnp.exp(edge)
    y_off = []
    per_grp = SSD_HEADS // SSD_GROUPS // 2
    for j in range(SSD_HEADS // 2):
        g = j // per_grp
        hj = h_scr[j]
        y_off.append(_pair(ex, j, off, lane_lo)
                     * jnp.dot(cm[g], hj.astype(BF16), preferred_element_type=F32))
        xt = (xh[j] * _pair(wt, j, off, lane_lo)).astype(BF16)
        h_scr[j] = hj * _pair(eedge, j, off, lane_lo[0:1]) + lax.dot_general(
            bm[g], xt, (((0,), (0,)), ((), ())), preferred_element_type=F32)
    return y_off


def _ssd_fwd_kernel(cur_ref, prev_ref, next_ref, dt_ref, cw_ref, cb_ref, dtb_ref, alog_ref,
                    shift_ref, tri_ref, yp_ref, h_scr, *, nchunks):
    q = SSD_Q
    n = pl.program_id(1)

    @pl.when(n == 0)
    def _():
        h_scr[...] = jnp.zeros_like(h_scr)

    dt, acc_ = _ssd_decay(dt_ref, dtb_ref, alog_ref, tri_ref)
    ex = jnp.exp(acc_)
    acc_t = acc_.T
    dt_t = dt.T
    xs = _ssd_conv(cur_ref, prev_ref, next_ref, cw_ref, cb_ref, shift_ref, n, nchunks - 1)
    nx = SSD_HEADS // 2
    xh = xs[0:nx]
    bm = [t.astype(BF16) for t in xs[nx:nx + SSD_GROUPS]]
    cm = [t.astype(BF16) for t in xs[nx + SSD_GROUPS:nx + 2 * SSD_GROUPS]]
    lane_lo = lax.broadcasted_iota(jnp.int32, (q, LANE), 1) < SSD_P

    ti = lax.broadcasted_iota(jnp.int32, (q, q), 0)
    si = lax.broadcasted_iota(jnp.int32, (q, q), 1)
    lower = si <= ti
    upper = si >= ti
    nt = (((1,), (1,)), ((), ()))
    heads_per_grp = SSD_HEADS // SSD_GROUPS
    y = []
    for j in range(nx):
        g = (2 * j) // heads_per_grp
        cb = lax.dot_general(cm[g], bm[g], nt, preferred_element_type=F32)
        parts = []
        for h in (2 * j, 2 * j + 1):
            hb = SSD_HEADS + h
            lf = jnp.where(lower, jnp.exp(acc_[:, h:h + 1] - acc_t[h:h + 1, :]), 0.0)
            lb = jnp.where(upper, jnp.exp(acc_[:, hb:hb + 1] - acc_t[hb:hb + 1, :]), 0.0)
            mh = (cb * (lf * dt_t[h:h + 1, :] + lb * dt_t[hb:hb + 1, :])).astype(BF16)
            parts.append(jnp.dot(mh, xh[j].astype(BF16), preferred_element_type=F32))
        y.append(jnp.where(lane_lo, parts[0], parts[1]))

    y_off = _ssd_state_step(xh, bm, cm, dt, acc_, ex, h_scr, False, lane_lo)
    for j in range(nx):
        yp_ref[j] = y[j] + y_off[j]


def _ssd_bwd_kernel(cur_ref, prev_ref, next_ref, dt_ref, cw_ref, cb_ref, dtb_ref, alog_ref,
                    shift_ref, tri_ref, yp_ref, z_ref, skip_ref, norm_ref, y_ref, h_scr, *, nchunks):
    q = SSD_Q
    n = pl.program_id(1)

    @pl.when(n == 0)
    def _():
        h_scr[...] = jnp.zeros_like(h_scr)

    dt, acc_ = _ssd_decay(dt_ref, dtb_ref, alog_ref, tri_ref)
    ex = jnp.exp(acc_)
    xs = _ssd_conv(cur_ref, prev_ref, next_ref, cw_ref, cb_ref, shift_ref, nchunks - 1 - n, nchunks - 1)
    nx = SSD_HEADS // 2
    xh = xs[0:nx]
    bm = [t.astype(BF16) for t in xs[nx:nx + SSD_GROUPS]]
    cm = [t.astype(BF16) for t in xs[nx + SSD_GROUPS:nx + 2 * SSD_GROUPS]]
    lane_lo = lax.broadcasted_iota(jnp.int32, (q, LANE), 1) < SSD_P
    y_off = _ssd_state_step(xh, bm, cm, dt, acc_, ex, h_scr, True, lane_lo)

    ys = []
    ssq = jnp.zeros((q, 1), F32)
    for j in range(nx):
        z = z_ref[j].astype(F32)
        yj = (yp_ref[j] + y_off[j] + skip_ref[j] * xh[j]) * (z * _sigmoid(z))
        ssq = ssq + jnp.sum(yj * yj, axis=-1, keepdims=True)
        ys.append(yj)
    r = lax.rsqrt(ssq * (1.0 / (nx * LANE)) + EPS)
    for j in range(nx):
        y_ref[j] = (ys[j] * r * norm_ref[j]).astype(BF16)


def _ssd(u3, udt, conv_w_l, conv_b_l, dt_bias_l, a_log_l, skip_l, norm_l, shift, tri, bsz, seq):
    q = SSD_Q
    nchunks = seq // q
    nxbc = BLK_SSD_Z - BLK_SSD_X
    nx = SSD_HEADS // 2
    hb = q // HALO
    tb = (bsz * seq) // HALO

    cw = conv_w_l.astype(F32).reshape(CONV_TAPS, nxbc, 1, LANE)
    cb = conv_b_l.astype(F32).reshape(nxbc, 1, LANE)
    dtb = jnp.zeros((1, LANE), F32).at[0, :N_DT].set(dt_bias_l.astype(F32).reshape(-1))
    alog = jnp.zeros((1, LANE), F32).at[0, :N_DT].set(a_log_l.astype(F32).reshape(-1))
    skip = jnp.repeat(skip_l.astype(F32), SSD_P).reshape(nx, 1, LANE)
    norm = norm_l.astype(F32).reshape(nx, 1, LANE)

    def specs(chunk_of):
        def cur(b, n):
            return (BLK_SSD_X // nxbc, b * nchunks + chunk_of(n), 0)

        def prev(b, n):
            return (BLK_SSD_X // nxbc, jnp.maximum((b * nchunks + chunk_of(n)) * hb - 1, 0), 0)

        def nxt(b, n):
            return (BLK_SSD_X // nxbc, jnp.minimum((b * nchunks + chunk_of(n) + 1) * hb, tb - 1), 0)

        return [pl.BlockSpec((nxbc, q, LANE), cur),
                pl.BlockSpec((nxbc, HALO, LANE), prev),
                pl.BlockSpec((nxbc, HALO, LANE), nxt),
                pl.BlockSpec((q, LANE), lambda b, n: (b * nchunks + chunk_of(n), 0)),
                pl.BlockSpec((CONV_TAPS, nxbc, 1, LANE), lambda b, n: (0, 0, 0, 0)),
                pl.BlockSpec((nxbc, 1, LANE), lambda b, n: (0, 0, 0)),
                pl.BlockSpec((1, LANE), lambda b, n: (0, 0)),
                pl.BlockSpec((1, LANE), lambda b, n: (0, 0)),
                pl.BlockSpec(shift.shape, lambda b, n: (0, 0)),
                pl.BlockSpec(tri.shape, lambda b, n: (0, 0, 0))]

    state = pltpu.VMEM((nx, SSD_N, LANE), F32)
    ident = lambda n: n
    ypart = pl.pallas_call(
        functools.partial(_ssd_fwd_kernel, nchunks=nchunks),
        grid=(bsz, nchunks),
        in_specs=specs(ident),
        out_specs=pl.BlockSpec((nx, q, LANE), lambda b, n: (0, b * nchunks + n, 0)),
        out_shape=jax.ShapeDtypeStruct((nx, bsz * seq, LANE), F32),
        scratch_shapes=[state],
        compiler_params=_cparams(("parallel", "arbitrary")),
    )(u3, u3, u3, udt, cw, cb, dtb, alog, shift, tri)

    flip = lambda n: nchunks - 1 - n
    blk = lambda b, n: (0, b * nchunks + flip(n), 0)
    return pl.pallas_call(
        functools.partial(_ssd_bwd_kernel, nchunks=nchunks),
        grid=(bsz, nchunks),
        in_specs=specs(flip) + [
            pl.BlockSpec((nx, q, LANE), blk),
            pl.BlockSpec((nx, q, LANE), lambda b, n: (BLK_SSD_Z // nx, b * nchunks + flip(n), 0)),
            pl.BlockSpec((nx, 1, LANE), lambda b, n: (0, 0, 0)),
            pl.BlockSpec((nx, 1, LANE), lambda b, n: (0, 0, 0))],
        out_specs=pl.BlockSpec((nx, q, LANE), blk),
        out_shape=jax.ShapeDtypeStruct((nx, bsz * seq, LANE), BF16),
        scratch_shapes=[state],
        compiler_params=_cparams(("parallel", "arbitrary")),
    )(u3, u3, u3, udt, cw, cb, dtb, alog, shift, tri, ypart, u3, skip, norm)


def _cat_blocks(ref, lo, n):
    return jnp.concatenate([ref[lo + k] for k in range(n)], axis=-1)


def _merge_kernel(yhg_ref, yna_ref, ylru_ref, yssd_ref, gt_ref, x_ref, mod_ref,
                  whg_ref, wna_ref, wlru_ref, wssd_ref, wout_ref, o_ref, m_scr):
    nd = D_MODEL // LANE
    per = MXU_N // LANE
    nchunk = D_MODEL // MXU_N
    branches = ((yhg_ref, whg_ref), (yna_ref, wna_ref), (ylru_ref, wlru_ref), (yssd_ref, wssd_ref))
    ys = [_cat_blocks(y_ref, 0, y_ref.shape[0]) for y_ref, _ in branches]

    prods = []

    def gate_sum(c):
        tot = None
        for i in range(len(branches)):
            term = _cat_blocks(gt_ref, nd * i + per * c, per).astype(F32) * prods[c][i]
            tot = term if tot is None else tot + term
        m_scr[:, c * MXU_N:(c + 1) * MXU_N] = tot.astype(BF16)

    for c in range(nchunk):
        cols = slice(c * MXU_N, (c + 1) * MXU_N)
        prods.append([jnp.dot(y, w_ref[:, cols], preferred_element_type=F32)
                      for y, (_, w_ref) in zip(ys, branches)])
        if c > 0:
            gate_sum(c - 1)
    gate_sum(nchunk - 1)

    mb = m_scr[...]
    gate = mod_ref[0][:, 2 * D_MODEL:3 * D_MODEL]
    outs = []

    def residual(c):
        cols = slice(c * MXU_N, (c + 1) * MXU_N)
        o_ref[:, cols] = x_ref[:, cols] + gate[:, cols] * outs[c]

    for c in range(nchunk):
        outs.append(jnp.dot(mb, wout_ref[:, c * MXU_N:(c + 1) * MXU_N], preferred_element_type=F32))
        if c > 0:
            residual(c - 1)
    residual(nchunk - 1)


def _merge(y_hg, y_na, y_lru, y_ssd, u3, x2, mod_l, w_hg, w_na, w_lru, w_ssd, w_out, seq, tm=512):
    t = x2.shape[0]
    tm = min(tm, seq)

    def yspec(arr):
        return pl.BlockSpec((arr.shape[0], tm, LANE), lambda i: (0, i, 0))

    def wspec(arr):
        return pl.BlockSpec(arr.shape, lambda i: (0, 0))

    return pl.pallas_call(
        _merge_kernel,
        grid=(t // tm,),
        in_specs=[yspec(y_hg), yspec(y_na), yspec(y_lru), yspec(y_ssd),
                  pl.BlockSpec((N_GATE, tm, LANE), lambda i: (BLK_MERGE // N_GATE, i, 0)),
                  pl.BlockSpec((tm, D_MODEL), lambda i: (i, 0)),
                  pl.BlockSpec((1, 1, 6 * D_MODEL), lambda i: ((i * tm) // seq, 0, 0)),
                  wspec(w_hg), wspec(w_na), wspec(w_lru), wspec(w_ssd), wspec(w_out)],
        out_specs=pl.BlockSpec((tm, D_MODEL), lambda i: (i, 0)),
        out_shape=jax.ShapeDtypeStruct((t, D_MODEL), F32),
        scratch_shapes=[pltpu.VMEM((tm, D_MODEL), BF16)],
        compiler_params=_cparams(("parallel",)),
    )(y_hg, y_na, y_lru, y_ssd, u3, x2, mod_l, w_hg, w_na, w_lru, w_ssd, w_out)


def _ffn_kernel(x_ref, mod_ref, g_ref, w1_ref, w2_ref, gfin_ref, o_ref, h_scr, e_scr, acc_scr, *, final_norm):
    j = pl.program_id(1)
    nj = pl.num_programs(1)

    @pl.when(j == 0)
    def _():
        acc_scr[...] = jnp.zeros_like(acc_scr)
        x = x_ref[...]
        ms = jnp.mean(x * x, axis=-1, keepdims=True)
        y = x * lax.rsqrt(ms + EPS) * g_ref[...]
        mod = mod_ref[0]
        shift = mod[:, 3 * D_MODEL:4 * D_MODEL]
        scale = mod[:, 4 * D_MODEL:5 * D_MODEL]
        h_scr[...] = (y * (1.0 + scale) + shift).astype(BF16)

    hb = h_scr[...]
    tf = w1_ref.shape[1]
    hids = []

    def square(c):
        hid = jnp.maximum(hids[c], 0.0)
        e_scr[:, c * MXU_N:(c + 1) * MXU_N] = (hid * hid).astype(BF16)

    for c in range(tf // MXU_N):
        hids.append(jnp.dot(hb, w1_ref[:, c * MXU_N:(c + 1) * MXU_N], preferred_element_type=F32))
        if c > 0:
            square(c - 1)
    square(tf // MXU_N - 1)

    eb = e_scr[...]
    parts = []

    def accumulate(c):
        cols = slice(c * MXU_N, (c + 1) * MXU_N)
        acc_scr[:, cols] = acc_scr[:, cols] + parts[c]

    for c in range(D_MODEL // MXU_N):
        parts.append(jnp.dot(eb, w2_ref[:, c * MXU_N:(c + 1) * MXU_N], preferred_element_type=F32))
        if c > 0:
            accumulate(c - 1)
    accumulate(D_MODEL // MXU_N - 1)

    @pl.when(j == nj - 1)
    def _():
        gate = mod_ref[0][:, 5 * D_MODEL:6 * D_MODEL]
        xo = x_ref[...] + gate * acc_scr[...]
        if final_norm:
            ms = jnp.mean(xo * xo, axis=-1, keepdims=True)
            xo = xo * lax.rsqrt(ms + EPS) * gfin_ref[...]
        o_ref[...] = xo


def _ffn(x2, mod_l, g, w1, w2, g_final, seq, final_norm, tm=1024, tf=1024):
    t = x2.shape[0]
    tm = min(tm, seq)
    return pl.pallas_call(
        functools.partial(_ffn_kernel, final_norm=final_norm),
        grid=(t // tm, D_FF // tf),
        in_specs=[pl.BlockSpec((tm, D_MODEL), lambda i, j: (i, 0)),
                  pl.BlockSpec((1, 1, 6 * D_MODEL), lambda i, j: ((i * tm) // seq, 0, 0)),
                  pl.BlockSpec((1, D_MODEL), lambda i, j: (0, 0)),
                  pl.BlockSpec((D_MODEL, tf), lambda i, j: (0, j)),
                  pl.BlockSpec((tf, D_MODEL), lambda i, j: (j, 0)),
                  pl.BlockSpec((1, D_MODEL), lambda i, j: (0, 0))],
        out_specs=pl.BlockSpec((tm, D_MODEL), lambda i, j: (i, 0)),
        out_shape=jax.ShapeDtypeStruct((t, D_MODEL), F32),
        scratch_shapes=[pltpu.VMEM((tm, D_MODEL), BF16), pltpu.VMEM((tm, tf), BF16),
                        pltpu.VMEM((tm, D_MODEL), F32)],
        compiler_params=_cparams(("parallel", "arbitrary")),
    )(x2, mod_l, g, w1, w2, g_final)


W_IN_ORDER = ((512, 1536),
              (8224, 12320),
              (0, 512),
              (1536, 2560),
              (2560, 5120),
              (6144, 8192),
              (5120, 6144))
W_IN_DT = (8192, 8224)


def _split_w_in(w_in_l):
    main = jnp.concatenate([w_in_l[:, a:b] for a, b in W_IN_ORDER], axis=1).astype(BF16)
    dt = jnp.zeros((D_MODEL, LANE), BF16).at[:, :N_DT].set(w_in_l[:, W_IN_DT[0]:W_IN_DT[1]].astype(BF16))
    return main, dt


def kernel(x, c, w_ada, b_ada, g_mix, g_ffn, w_in, hg_lb, hg_norm, na_rpb, lru_conv_w, lru_conv_b, lru_gate_w, lru_gate_b, lru_lambda, ssd_conv_w, ssd_conv_b, ssd_dt_bias, ssd_a_log, ssd_skip, ssd_norm, w_br_hg, w_br_na, w_br_lru, w_br_ssd, w_out, w_ff1, w_ff2, g_final):
    bsz, seq, d = x.shape
    depth = w_in.shape[0]
    rows = seq // GRID_W
    assert d == D_MODEL and seq % LRU_TC == 0 and rows % NA_G == 0 and rows >= 2 * NA_G

    mod = _ada_mod(c, w_ada, b_ada).reshape(depth, bsz, 1, 6 * D_MODEL)
    lbp = _hg_lower_bounds(hg_lb)
    hg_sums_np, hg_masks_np = _hgrn_consts()
    hg_sums = jnp.asarray(hg_sums_np, BF16)
    hg_masks = jnp.asarray(hg_masks_np)
    shift = jnp.asarray(_conv_shift_matrix(), BF16)
    tri = hg_sums[:, :HG_CHUNK, :]
    gfin = g_final.reshape(1, D_MODEL)

    x2 = x.reshape(bsz * seq, D_MODEL)
    for l in range(depth):
        w_main, w_dt = _split_w_in(w_in[l])
        uz, u3, udt = _inproj(x2, mod[l], g_mix[l].reshape(1, D_MODEL), w_main, w_dt, seq)
        y_hg = _hgrn(u3, uz, lbp[:, l], hg_norm[l].reshape(1, -1), hg_sums, hg_masks, bsz, seq)
        y_na = _na(u3, _na_bias_table(na_rpb[l]), bsz, seq)
        gwd, gbd, lamd = _lru_weights(lru_gate_w[l], lru_gate_b[l], lru_lambda[l])
        y_lru = _lru(u3, lru_conv_w[l], lru_conv_b[l], gwd, gbd, lamd, shift, bsz, seq)
        y_ssd = _ssd(u3, udt, ssd_conv_w[l], ssd_conv_b[l], ssd_dt_bias[l], ssd_a_log[l],
                     ssd_skip[l], ssd_norm[l], shift, tri, bsz, seq)
        x2 = _merge(y_hg, y_na, y_lru, y_ssd, u3, x2, mod[l],
                    w_br_hg[l].astype(BF16), w_br_na[l].astype(BF16), w_br_lru[l].astype(BF16),
                    w_br_ssd[l].astype(BF16), w_out[l].astype(BF16), seq)
        x2 = _ffn(x2, mod[l], g_ffn[l].reshape(1, D_MODEL), w_ff1[l].astype(BF16), w_ff2[l].astype(BF16),
                  gfin, seq, final_norm=(l == depth - 1))
    return x2.reshape(bsz, seq, D_MODEL)
```

```python
import functools

import numpy as np
import jax
import jax.numpy as jnp
from jax import lax
from jax.experimental import pallas as pl
from jax.experimental.pallas import tpu as pltpu

F32 = jnp.float32
BF16 = jnp.bfloat16

LANE = 128
SUB = 8
MXU_N = 256
D_MODEL = 1024
D_FF = 4 * D_MODEL
GRID_W = 64
EPS = 1e-6

HG_HEADS = 4
HG_HPS = 2
HG_CHUNK = 128
HG_LEVELS = (64, 32, 16, 8, 4, 2, 1)

NA_ROWS = 8
NA_COLS = 16
NA_HEADS = 8
NA_DH = 64
NA_G = 8

LRU_WIDTH = 512
LRU_BLOCKS = 8
LRU_BW = 64
LRU_C = 8.0
LRU_CW = 256
LRU_TC = 512
LRU_TINY = 1e-37

SSD_HEADS = 16
SSD_P = 64
SSD_GROUPS = 4
SSD_N = 128
SSD_Q = 128
HALO = 16
CONV_ROWS = 128
CONV_TAPS = 4

BLK_MERGE = 0
BLK_HG_Q, BLK_HG_I, BLK_HG_G = 32, 36, 40
BLK_NA_Q, BLK_NA_K, BLK_NA_V = 44, 48, 52
BLK_LRU_X, BLK_LRU_G = 56, 60
BLK_SSD_X, BLK_SSD_B, BLK_SSD_C, BLK_SSD_Z = 64, 72, 76, 80
N_BLK = 88
N_ZBLK = 8
N_GATE = 32
N_MAIN = (N_ZBLK + N_BLK) * LANE
N_DT = 2 * SSD_HEADS

VMEM_LIMIT = 48 * 1024 * 1024


def _cparams(sem):
    return pltpu.CompilerParams(dimension_semantics=sem, vmem_limit_bytes=VMEM_LIMIT)


def _sigmoid(x):
    return 0.5 * jnp.tanh(0.5 * x) + 0.5


def _softplus(x):
    return jnp.maximum(x, 0.0) + jnp.log1p(jnp.exp(-jnp.abs(x)))


def _split_bf16(x):
    hi = x.astype(BF16)
    lo = (x - hi.astype(F32)).astype(BF16)
    return jnp.concatenate([hi, lo], axis=1)


def _join_f32(r):
    half = r.shape[1] // 2
    return r[:, :half] + r[:, half:]


def _conv_shift_matrix():
    r = CONV_ROWS
    offs = [j - CONV_TAPS // 2 for j in range(CONV_TAPS) if j != CONV_TAPS // 2]
    m = np.zeros((len(offs), r, r + 2 * HALO), np.float32)
    for i, o in enumerate(offs):
        m[i, np.arange(r), HALO + np.arange(r) + o] = 1.0
    return m.reshape(len(offs) * r, r + 2 * HALO)


def _conv_taps(win, shift, wts, bias):
    r = CONV_ROWS
    sh = jnp.dot(shift, win, preferred_element_type=F32)
    acc = bias + wts(CONV_TAPS // 2) * win[HALO:HALO + r].astype(F32)
    i = 0
    for j in range(CONV_TAPS):
        if j != CONV_TAPS // 2:
            acc = acc + wts(j) * sh[i * r:(i + 1) * r]
            i += 1
    return acc


def _ada_kernel(c_ref, w_ref, b_ref, o_ref):
    c = c_ref[...]
    ca = c * _sigmoid(c)
    o_ref[0] = jnp.dot(ca, w_ref[0], preferred_element_type=F32,
                       precision=lax.Precision.HIGHEST) + b_ref[0]


def _ada_mod(c, w_ada, b_ada):
    depth, d, d6 = w_ada.shape
    bsz = c.shape[0]
    cp = jnp.zeros((SUB, d), F32).at[:bsz].set(c)
    out = pl.pallas_call(
        _ada_kernel,
        grid=(depth, d6 // d),
        in_specs=[pl.BlockSpec((SUB, d), lambda l, j: (0, 0)),
                  pl.BlockSpec((1, d, d), lambda l, j: (l, 0, j)),
                  pl.BlockSpec((1, 1, d), lambda l, j: (l, 0, j))],
        out_specs=pl.BlockSpec((1, SUB, d), lambda l, j: (l, 0, j)),
        out_shape=jax.ShapeDtypeStruct((depth, SUB, d6), F32),
        compiler_params=_cparams(("parallel", "parallel")),
    )(cp, w_ada, b_ada.reshape(depth, 1, d6))
    return out[:, :bsz]


def _lb_kernel(hg_ref, o_ref):
    depth = hg_ref.shape[0]
    xs = [hg_ref[l] for l in range(depth)]
    m = xs[0]
    for x in xs[1:]:
        m = jnp.maximum(m, x)
    es = [jnp.exp(x - m) for x in xs]
    tot = es[0]
    for e in es[1:]:
        tot = tot + e
    cs = None
    first = None
    for l in range(depth):
        sm = es[l] / tot
        cs = sm if cs is None else cs + sm
        if l == 0:
            first = cs
        lb = cs - first
        o_ref[0, l] = lb
        o_ref[1, l] = jnp.log(lb)
        o_ref[2, l] = jnp.log1p(-lb)


def _hg_lower_bounds(hg_lb):
    two, depth, k = hg_lb.shape
    return pl.pallas_call(
        _lb_kernel,
        out_shape=jax.ShapeDtypeStruct((3, depth, two, k), F32),
    )(hg_lb.transpose(1, 0, 2))


def _inproj_kernel(x_ref, mod_ref, g_ref, w_ref, wdt_ref, uz_ref, u_ref, udt_ref, h_scr, *, tn, z_tiles, gate_tiles):
    j = pl.program_id(1)
    nb = tn // LANE

    @pl.when(j == 0)
    def _():
        x = x_ref[...]
        ms = jnp.mean(x * x, axis=-1, keepdims=True)
        y = x * lax.rsqrt(ms + EPS) * g_ref[...]
        mod = mod_ref[0]
        shift = mod[:, 0:D_MODEL]
        scale = mod[:, D_MODEL:2 * D_MODEL]
        hb = (y * (1.0 + scale) + shift).astype(BF16)
        h_scr[...] = hb
        udt_ref[...] = jnp.dot(hb, wdt_ref[...], preferred_element_type=F32)

    is_gate = jnp.logical_and(j >= z_tiles, j < z_tiles + gate_tiles)
    hb = h_scr[...]
    per = MXU_N // LANE
    accs = []

    def epilogue(c):
        val = jnp.where(is_gate, _sigmoid(accs[c]), accs[c]).astype(BF16)
        for k in range(per):
            u_ref[c * per + k] = val[:, k * LANE:(k + 1) * LANE]

    for c in range(tn // MXU_N):
        accs.append(jnp.dot(hb, w_ref[:, c * MXU_N:(c + 1) * MXU_N], preferred_element_type=F32))
        if c > 0:
            epilogue(c - 1)
    epilogue(tn // MXU_N - 1)

    @pl.when(j < z_tiles)
    def _():
        for c in range(tn // MXU_N):
            for k in range(per):
                uz_ref[c * per + k] = accs[c][:, k * LANE:(k + 1) * LANE]


def _inproj(x2, mod_l, g, w_main, w_dt, seq, tm=1024, tn=1024):
    t = x2.shape[0]
    tm = min(tm, seq)
    nb = tn // LANE
    assert N_ZBLK % nb == 0 and N_GATE % nb == 0
    z_tiles = N_ZBLK // nb
    kern = functools.partial(_inproj_kernel, tn=tn, z_tiles=z_tiles, gate_tiles=N_GATE // nb)
    return pl.pallas_call(
        kern,
        grid=(t // tm, N_MAIN // tn),
        in_specs=[pl.BlockSpec((tm, D_MODEL), lambda i, j: (i, 0)),
                  pl.BlockSpec((1, 1, 6 * D_MODEL), lambda i, j: ((i * tm) // seq, 0, 0)),
                  pl.BlockSpec((1, D_MODEL), lambda i, j: (0, 0)),
                  pl.BlockSpec((D_MODEL, tn), lambda i, j: (0, j)),
                  pl.BlockSpec((D_MODEL, LANE), lambda i, j: (0, 0))],
        out_specs=[pl.BlockSpec((nb, tm, LANE), lambda i, j: (jnp.minimum(j, z_tiles - 1), i, 0)),
                   pl.BlockSpec((nb, tm, LANE), lambda i, j: (jnp.maximum(j - z_tiles, 0), i, 0)),
                   pl.BlockSpec((tm, LANE), lambda i, j: (i, 0))],
        out_shape=[jax.ShapeDtypeStruct((N_ZBLK, t, LANE), F32),
                   jax.ShapeDtypeStruct((N_BLK, t, LANE), BF16),
                   jax.ShapeDtypeStruct((t, LANE), F32)],
        scratch_shapes=[pltpu.VMEM((tm, D_MODEL), BF16)],
        compiler_params=_cparams(("parallel", "arbitrary")),
    )(x2, mod_l, g, w_main, w_dt)


def _hgrn_consts():
    c = HG_CHUNK
    nlev = len(HG_LEVELS)
    t = np.arange(c)[:, None]
    r = np.arange(c)[None, :]
    sums = np.zeros((2, 1 + nlev, c, c), np.float32)
    masks = np.zeros((2, nlev + 1, c, c), np.float32)
    sums[0, 0] = r <= t
    sums[1, 0] = r >= t
    for li, m in enumerate(HG_LEVELS):
        grp = t // (2 * m)
        ref = grp * 2 * m + m - 1
        sums[0, 1 + li] = ((t > ref) & (r > ref) & (r <= t)) | ((t <= ref) & (r > t) & (r <= ref))
        ref = grp * 2 * m + m
        sums[1, 1 + li] = ((t < ref) & (r >= t) & (r < ref)) | ((t >= ref) & (r >= ref) & (r < t))
        own = ((t // (2 * m)) == (r // (2 * m))) & ((t % (2 * m)) >= m) & ((r % (2 * m)) < m)
        masks[0, li] = own
        masks[1, li] = own.T
    masks[:, nlev] = np.eye(c)
    return sums.reshape(2, (1 + nlev) * c, c), masks


def _hgrn_chunks(streams, sums_ref, msk_ref):
    c = HG_CHUNK
    nlev = len(HG_LEVELS)
    nt = (((1,), (1,)), ((), ()))

    keys, pieces = [], []
    for (q, z, v, lb, llb, l1m, st, rev) in streams:
        logsig = jnp.minimum(z, 0.0) - jnp.log(1.0 + jnp.exp(-jnp.abs(z)))
        keys.append((1.0 - lb) * jnp.exp(logsig - z))
        y2 = l1m + logsig
        lf = jnp.maximum(llb, y2) + jnp.log(1.0 + jnp.exp(-jnp.abs(llb - y2)))
        pieces.append(_split_bf16(lf))

    ex_alls = []
    for (q, z, v, lb, llb, l1m, st, rev), p in zip(streams, pieces):
        ex_alls.append(_join_f32(jnp.dot(sums_ref[1 if rev else 0], p, preferred_element_type=F32)))

    qfs = [s[0].astype(F32) for s in streams]
    accs = []
    for (q, z, v, lb, llb, l1m, st, rev), key in zip(streams, keys):
        d = 1 if rev else 0
        accs.append(lax.dot_general(q, key.astype(BF16), nt, preferred_element_type=F32) * msk_ref[d, nlev])
    for li in range(nlev):
        for si, ((q, z, v, lb, llb, l1m, st, rev), key) in enumerate(zip(streams, keys)):
            d = 1 if rev else 0
            ex = jnp.exp(ex_alls[si][(1 + li) * c:(2 + li) * c])
            sl = lax.dot_general((qfs[si] * ex).astype(BF16), (key * ex).astype(BF16), nt,
                                 preferred_element_type=F32)
            accs[si] = accs[si] + sl * msk_ref[d, li]

    outs = []
    for si, ((q, z, v, lb, llb, l1m, st, rev), key) in enumerate(zip(streams, keys)):
        b = ex_alls[si][0:c]
        o = jnp.dot(accs[si].astype(BF16), v, preferred_element_type=F32)
        qs = (qfs[si] * jnp.exp(b)).astype(BF16)
        o = o + lax.dot_general(qs, st.astype(BF16), nt, preferred_element_type=F32)
        bedge = b[0:1, :] if rev else b[c - 1:c, :]
        ks_ = (key * jnp.exp(bedge - b)).astype(BF16)
        vt = v.astype(F32).T.astype(BF16)
        st_new = st * jnp.exp(bedge) + jnp.dot(vt, ks_, preferred_element_type=F32)
        outs.append((o, st_new))
    return outs


def _hgrn_kernel(q_ref, zf_ref, zb_ref, v_ref, g_ref, lbp_ref, norm_ref, sums_ref, msk_ref, y_ref,
                 o_scr, st_scr, *, seq):
    c = HG_CHUNK
    n = seq // c
    half = n // 2

    def lanes(h):
        return slice(h * LANE, (h + 1) * LANE)

    def step(i, finalize):
        rf = pl.multiple_of(i * c, c)
        rb = pl.multiple_of((n - 1 - i) * c, c)
        streams = []
        for h in range(HG_HPS):
            for rev, r0, z_ref in ((False, rf, zf_ref), (True, rb, zb_ref)):
                d = 1 if rev else 0
                streams.append((q_ref[h, pl.ds(r0, c), :], z_ref[h, pl.ds(r0, c), :], v_ref[h, pl.ds(r0, c), :],
                                lbp_ref[0, d:d + 1, lanes(h)], lbp_ref[1, d:d + 1, lanes(h)],
                                lbp_ref[2, d:d + 1, lanes(h)], st_scr[2 * h + d], rev))
        outs = _hgrn_chunks(streams, sums_ref, msk_ref)
        for h in range(HG_HPS):
            for d, r0 in ((0, rf), (1, rb)):
                o, st = outs[2 * h + d]
                st_scr[2 * h + d] = st
                if not finalize:
                    o_scr[h, pl.ds(r0, c), :] = o
                else:
                    o = o + o_scr[h, pl.ds(r0, c), :]
                    o = o * lax.rsqrt(jnp.mean(o * o, axis=-1, keepdims=True) + EPS) * norm_ref[:, lanes(h)]
                    g = g_ref[h, pl.ds(r0, c), :].astype(F32)
                    y_ref[h, pl.ds(r0, c), :] = (o * (g * _sigmoid(g))).astype(BF16)

    def first_half(i, carry):
        step(i, False)
        return carry

    def second_half(i, carry):
        step(i, True)
        return carry

    st_scr[...] = jnp.zeros_like(st_scr)
    lax.fori_loop(0, half, first_half, 0)
    lax.fori_loop(half, n, second_half, 0)


def _hgrn(u3, uz, lbp_l, hg_norm_l, sums, masks, bsz, seq):
    hps = HG_HPS
    nlev = len(HG_LEVELS)
    assert (seq // HG_CHUNK) % 2 == 0

    def ublk(base):
        return pl.BlockSpec((hps, seq, LANE), lambda b, hp, base=base: (base // hps + hp, b, 0))

    def zblk(base):
        return pl.BlockSpec((hps, seq, LANE), lambda b, hp, base=base: (base // hps + hp, b, 0))

    kern = functools.partial(_hgrn_kernel, seq=seq)
    return pl.pallas_call(
        kern,
        grid=(bsz, HG_HEADS // hps),
        in_specs=[ublk(BLK_HG_Q), zblk(0), zblk(HG_HEADS), ublk(BLK_HG_I), ublk(BLK_HG_G),
                  pl.BlockSpec((3, 2, hps * LANE), lambda b, hp: (0, 0, hp)),
                  pl.BlockSpec((1, hps * LANE), lambda b, hp: (0, hp)),
                  pl.BlockSpec((2, (1 + nlev) * HG_CHUNK, HG_CHUNK), lambda b, hp: (0, 0, 0)),
                  pl.BlockSpec((2, nlev + 1, HG_CHUNK, HG_CHUNK), lambda b, hp: (0, 0, 0, 0))],
        out_specs=pl.BlockSpec((hps, seq, LANE), lambda b, hp: (hp, b, 0)),
        out_shape=jax.ShapeDtypeStruct((HG_HEADS, bsz * seq, LANE), BF16),
        scratch_shapes=[pltpu.VMEM((hps, seq, LANE), F32), pltpu.VMEM((2 * hps, LANE, LANE), F32)],
        compiler_params=_cparams(("parallel", "parallel")),
    )(u3, uz, uz, u3, u3, lbp_l, hg_norm_l, sums, masks)


NA_NEG = -1e30


def _na_bias_table(rpb_l):
    qc = np.arange(GRID_W)[:, None]
    kc = np.arange(GRID_W)[None, :]
    cstart = np.clip(qc - NA_COLS // 2, 0, GRID_W - NA_COLS)
    col_ok = (kc >= cstart) & (kc < cstart + NA_COLS)
    dc = np.clip(kc - qc, 1 - NA_COLS, NA_COLS - 1) + NA_COLS - 1
    onehot = (dc[None] == np.arange(2 * NA_COLS - 1)[:, None, None]).astype(np.float32)
    cols = jnp.einsum('hrc,cqk->hrqk', rpb_l.astype(F32), jnp.asarray(onehot),
                      precision=lax.Precision.HIGHEST)
    cols = jnp.where(col_ok[None, None], cols, NA_NEG)
    tab = jnp.stack([cols[:, NA_ROWS - 1 - s:2 * NA_ROWS - 1 - s] for s in range(NA_ROWS)], axis=1)
    return tab.transpose(0, 1, 3, 2, 4).reshape(NA_HEADS, NA_ROWS, GRID_W, NA_ROWS * GRID_W)


def _na_slot(variant, i):
    half = NA_ROWS // 2
    if variant == 0:
        return max(i - half, 0), min(i, half)
    if variant == 1:
        return i, half
    return (half + i, half) if i < half else (NA_G, i)


def _na_kernel(q_ref, k_ref, v_ref, tab_ref, o_ref, bias_scr, *, seq):
    rows = seq // GRID_W
    ngrp = rows // NA_G
    gq = NA_G * GRID_W
    kwin = 2 * NA_G * GRID_W
    kw = NA_ROWS * GRID_W
    nt = (((1,), (1,)), ((), ()))

    @pl.when(pl.program_id(1) == 0)
    def _():
        neg = jnp.full((GRID_W, LANE), NA_NEG, F32)
        for variant in range(3):
            for i in range(NA_G):
                a0, sft = _na_slot(variant, i)
                base = (a0 // 2) * LANE
                wid = min(kw + LANE, kwin - base)
                for h in range(2):
                    slab = jnp.concatenate([tab_ref[h, sft], neg], axis=1)
                    if a0 % 2:
                        slab = pltpu.roll(slab, GRID_W, 1)
                    pieces = [neg] * (base // LANE) + [slab[:, :wid]] + [neg] * ((kwin - base - wid) // LANE)
                    bias_scr[variant, h, i * GRID_W:(i + 1) * GRID_W, :] = jnp.concatenate(pieces, axis=1)

    head0 = lax.broadcasted_iota(jnp.int32, (gq, LANE), 1) < NA_DH
    zero = jnp.zeros((gq, LANE), BF16)

    def body(j, carry):
        variant = jnp.where(j == 0, 0, jnp.where(j == ngrp - 1, 2, 1))
        k0r = jnp.clip(j * NA_G - NA_ROWS // 2, 0, rows - 2 * NA_G)
        q0 = pl.multiple_of(j * gq, gq)
        k0 = pl.multiple_of(k0r * GRID_W, GRID_W)
        q = q_ref[0, pl.ds(q0, gq), :] * (NA_DH ** -0.5)
        kb = k_ref[0, pl.ds(k0, kwin), :]
        vb = v_ref[0, pl.ds(k0, kwin), :]
        qhs = [jnp.where(head0, q, zero), jnp.where(head0, zero, q)]
        ss = [lax.dot_general(qh, kb, nt, preferred_element_type=F32) for qh in qhs]
        ps, ls = [], []
        for h in range(2):
            s = ss[h] + bias_scr[variant, h]
            p = jnp.exp(s - jnp.max(s, axis=-1, keepdims=True))
            ls.append(jnp.sum(p, axis=-1, keepdims=True))
            ps.append(p.astype(BF16))
        outs = [jnp.dot(p, vb, preferred_element_type=F32) for p in ps]
        o_ref[0, pl.ds(q0, gq), :] = jnp.where(head0, outs[0] / ls[0], outs[1] / ls[1]).astype(BF16)
        return carry

    lax.fori_loop(0, ngrp, body, 0)


def _na(u3, tab, bsz, seq):
    def ublk(base):
        return pl.BlockSpec((1, seq, LANE), lambda hp, b, base=base: (base + hp, b, 0))

    kern = functools.partial(_na_kernel, seq=seq)
    kw = NA_ROWS * GRID_W
    return pl.pallas_call(
        kern,
        grid=(NA_HEADS // 2, bsz),
        in_specs=[ublk(BLK_NA_Q), ublk(BLK_NA_K), ublk(BLK_NA_V),
                  pl.BlockSpec((2, NA_ROWS, GRID_W, kw), lambda hp, b: (hp, 0, 0, 0))],
        out_specs=pl.BlockSpec((1, seq, LANE), lambda hp, b: (hp, b, 0)),
        out_shape=jax.ShapeDtypeStruct((NA_HEADS // 2, bsz * seq, LANE), BF16),
        scratch_shapes=[pltpu.VMEM((3, 2, NA_G * GRID_W, 2 * NA_G * GRID_W), F32)],
        compiler_params=_cparams(("parallel", "arbitrary")),
    )(u3, u3, u3, tab)


def _gelu_tanh(x):
    return 0.5 * x * (1.0 + jnp.tanh(0.7978845608028654 * (x + 0.044715 * (x * x * x))))


def _lru_kernel(x_ref, gate_ref, cw_ref, cb_ref, gw_ref, gb_ref, lam_ref, shift_ref, y_ref,
                xpad, xf_scr, a_scr, u_scr, hf_scr, *, seq):
    w = LRU_CW
    tc = LRU_TC
    nch = seq // tc
    ngrp = seq // SUB
    nb = w // LANE
    r = CONV_ROWS

    zpad = jnp.zeros((HALO, w), BF16)
    xpad[0:HALO, :] = zpad
    xpad[seq + HALO:seq + 2 * HALO, :] = zpad
    for k in range(nb):
        xpad[HALO:seq + HALO, k * LANE:(k + 1) * LANE] = x_ref[k]

    sub = lax.broadcasted_iota(jnp.int32, (tc // SUB, SUB, w), 1)

    for d in range(2):
        rev = d == 1
        sp = _softplus(-lam_ref[0, d])

        def pass_a(i, carry, d=d, rev=rev, sp=sp):
            r0 = pl.multiple_of(i * tc, tc)
            if not rev:
                win = xpad[pl.ds(r0, tc + 2 * HALO), :]
                parts = [_conv_taps(win[s * r:(s + 1) * r + 2 * HALO], shift_ref[...],
                                    lambda j: cw_ref[j:j + 1, :], cb_ref[...]) for s in range(tc // r)]
                xf = jnp.concatenate(parts, axis=0)
                xf_scr[pl.ds(r0, tc), :] = xf
            else:
                xf = xf_scr[pl.ds(r0, tc), :]
            gts = jnp.dot(xf.astype(BF16), gw_ref[0, d], preferred_element_type=F32) + gb_ref[0, d]
            rg = _sigmoid(gts[:, 0:w])
            ig = _sigmoid(gts[:, w:2 * w])
            log_a = (-LRU_C) * rg * sp
            a = jnp.exp(log_a)
            y = -jnp.tanh(log_a) * (a * a + 1.0)
            u = y * lax.rsqrt(jnp.maximum(y, LRU_TINY)) * (ig * xf)
            a = a.reshape(tc // SUB, SUB, w)
            u = u.reshape(tc // SUB, SUB, w)
            for dd in (1, 2, 4):
                sh = (SUB - dd) if rev else dd
                ok = (sub < SUB - dd) if rev else (sub >= dd)
                a_s = pltpu.roll(a, sh, 1)
                u_s = pltpu.roll(u, sh, 1)
                u = jnp.where(ok, a * u_s + u, u)
                a = jnp.where(ok, a * a_s, a)
            a_scr[pl.ds(r0, tc), :] = a.reshape(tc, w)
            u_scr[pl.ds(r0, tc), :] = u.reshape(tc, w)
            return carry

        lax.fori_loop(0, nch, pass_a, 0)

        def pass_b(g, h, rev=rev):
            gi = (ngrp - 1 - g) if rev else g
            r0 = pl.multiple_of(gi * SUB, SUB)
            hp = h[0:1, :] if rev else h[SUB - 1:SUB, :]
            hn = u_scr[pl.ds(r0, SUB), :] + a_scr[pl.ds(r0, SUB), :] * hp
            if rev:
                u_scr[pl.ds(r0, SUB), :] = hn
            else:
                hf_scr[pl.ds(r0, SUB), :] = hn
            return hn

        lax.fori_loop(0, ngrp, pass_b, jnp.zeros((SUB, w), F32), unroll=8)

    def pass_c(i, carry):
        r0 = pl.multiple_of(i * tc, tc)
        hsum = hf_scr[pl.ds(r0, tc), :] + u_scr[pl.ds(r0, tc), :]
        for k in range(nb):
            g = gate_ref[k, pl.ds(r0, tc), :].astype(F32)
            y_ref[k, pl.ds(r0, tc), :] = (hsum[:, k * LANE:(k + 1) * LANE] * _gelu_tanh(g)).astype(BF16)
        return carry

    lax.fori_loop(0, nch, pass_c, 0)


def _lru_weights(gate_w_l, gate_b_l, lam_l):
    ncb = LRU_WIDTH // LRU_CW
    per = LRU_CW // LRU_BW
    gw = gate_w_l.astype(F32)
    eye = jnp.eye(per, dtype=F32)
    blocks = gw.reshape(2, 2, ncb, per, LRU_BW, LRU_BW)
    dense = jnp.einsum('dgcpio,pq->dgcpiqo', blocks, eye).reshape(2, 2, ncb, LRU_CW, LRU_CW)
    dense = dense.transpose(2, 0, 3, 1, 4).reshape(ncb, 2, LRU_CW, 2 * LRU_CW).astype(BF16)
    gb = gate_b_l.astype(F32).reshape(2, 2, ncb, LRU_CW).transpose(2, 0, 1, 3).reshape(ncb, 2, 1, 2 * LRU_CW)
    lam = lam_l.astype(F32).reshape(2, ncb, LRU_CW).transpose(1, 0, 2).reshape(ncb, 2, 1, LRU_CW)
    return dense, gb, lam


def _lru(u3, conv_w_l, conv_b_l, gwd, gbd, lamd, shift, bsz, seq):
    ncb = LRU_WIDTH // LRU_CW
    nb = LRU_CW // LANE
    kern = functools.partial(_lru_kernel, seq=seq)

    def ublk(base):
        return pl.BlockSpec((nb, seq, LANE), lambda b, cb, base=base: (base // nb + cb, b, 0))

    return pl.pallas_call(
        kern,
        grid=(bsz, ncb),
        in_specs=[ublk(BLK_LRU_X), ublk(BLK_LRU_G),
                  pl.BlockSpec((CONV_TAPS, LRU_CW), lambda b, cb: (0, cb)),
                  pl.BlockSpec((1, LRU_CW), lambda b, cb: (0, cb)),
                  pl.BlockSpec((1, 2, LRU_CW, 2 * LRU_CW), lambda b, cb: (cb, 0, 0, 0)),
                  pl.BlockSpec((1, 2, 1, 2 * LRU_CW), lambda b, cb: (cb, 0, 0, 0)),
                  pl.BlockSpec((1, 2, 1, LRU_CW), lambda b, cb: (cb, 0, 0, 0)),
                  pl.BlockSpec(shift.shape, lambda b, cb: (0, 0))],
        out_specs=pl.BlockSpec((nb, seq, LANE), lambda b, cb: (cb, b, 0)),
        out_shape=jax.ShapeDtypeStruct((LRU_WIDTH // LANE, bsz * seq, LANE), BF16),
        scratch_shapes=[pltpu.VMEM((seq + 2 * HALO, LRU_CW), BF16),
                        pltpu.VMEM((seq, LRU_CW), F32),
                        pltpu.VMEM((seq, LRU_CW), F32),
                        pltpu.VMEM((seq, LRU_CW), F32),
                        pltpu.VMEM((seq, LRU_CW), F32)],
        compiler_params=_cparams(("parallel", "parallel")),
    )(u3, u3, conv_w_l, conv_b_l.reshape(1, LRU_WIDTH), gwd, gbd, lamd, shift)


def _ssd_decay(dt_ref, dtb_ref, alog_ref, tri_ref):
    dt = _softplus(dt_ref[...] + dtb_ref[...])
    a = dt * (-jnp.exp(alog_ref[...]))
    lane = lax.broadcasted_iota(jnp.int32, (SSD_Q, LANE), 1)
    ap = _split_bf16(a)
    pre = _join_f32(jnp.dot(tri_ref[0], ap, preferred_element_type=F32))
    suf = _join_f32(jnp.dot(tri_ref[1], ap, preferred_element_type=F32))
    return dt, jnp.where(lane < SSD_HEADS, pre, suf)


def _ssd_conv(cur_ref, prev_ref, next_ref, cw_ref, cb_ref, shift_ref, n, nlast):
    q = SSD_Q
    nblk = cur_ref.shape[0]
    zero = jnp.zeros((HALO, LANE), BF16)
    wins = []
    for k in range(nblk):
        pv = jnp.where(n > 0, prev_ref[k], zero)
        nx = jnp.where(n < nlast, next_ref[k], zero)
        wins.append(jnp.concatenate([pv, cur_ref[k], nx], axis=0))
    win = jnp.concatenate(wins, axis=1)
    r = CONV_ROWS
    sh = jnp.dot(shift_ref[...], win, preferred_element_type=F32)
    xs = []
    for k in range(nblk):
        cols = slice(k * LANE, (k + 1) * LANE)
        acc = cb_ref[k] + cw_ref[CONV_TAPS // 2, k] * cur_ref[k].astype(F32)
        i = 0
        for j in range(CONV_TAPS):
            if j != CONV_TAPS // 2:
                acc = acc + cw_ref[j, k] * sh[i * r:(i + 1) * r, cols]
                i += 1
        xs.append(acc * _sigmoid(acc))
    return xs


def _ssd_expand_matrix():
    nx = SSD_HEADS // 2
    m = np.zeros((2, LANE, nx * LANE), np.float32)
    for d in range(2):
        for j in range(nx):
            for half in range(2):
                m[d, d * SSD_HEADS + 2 * j + half, j * LANE + half * SSD_P:j * LANE + (half + 1) * SSD_P] = 1.0
    return np.concatenate([m, m], axis=1)


def _ssd_state_step(xh, bm, cm, dt, acc_, ex, h_scr, rev, expand_ref):
    q = SSD_Q
    d = 1 if rev else 0
    off = d * SSD_HEADS
    er = 0 if rev else q - 1
    edge = acc_[er:er + 1, :]
    lane = lax.broadcasted_iota(jnp.int32, (q, LANE), 1)
    if expand_ref is not None:
        mine = jnp.logical_and(lane >= off, lane < off + SSD_HEADS)
        wt = jnp.where(mine, jnp.exp(edge - acc_) * dt, 0.0)
        pex = jnp.dot(_split_bf16(ex), expand_ref[d], preferred_element_type=F32)
        pwt = jnp.dot(_split_bf16(wt), expand_ref[d], preferred_element_type=F32)
        pattern = lambda p, j: p[:, j * LANE:(j + 1) * LANE]
    else:
        wt = jnp.exp(edge - acc_) * dt
        pex, pwt = ex, wt
        lane_lo = lane < SSD_P

        def pattern(p, j):
            c0 = off + 2 * j
            return jnp.where(lane_lo, p[:, c0:c0 + 1], p[:, c0 + 1:c0 + 2])

    y_off = []
    per_grp = SSD_HEADS // SSD_GROUPS // 2
    for j in range(SSD_HEADS // 2):
        g = j // per_grp
        hj = h_scr[j]
        pe = pattern(pex, j)
        y_off.append(pe * jnp.dot(cm[g], hj.astype(BF16), preferred_element_type=F32))
        xt = (xh[j] * pattern(pwt, j)).astype(BF16)
        h_scr[j] = hj * pe[er:er + 1, :] + lax.dot_general(
            bm[g], xt, (((0,), (0,)), ((), ())), preferred_element_type=F32)
    return y_off


def _ssd_fwd_kernel(cur_ref, prev_ref, next_ref, dt_ref, cw_ref, cb_ref, dtb_ref, alog_ref,
                    shift_ref, tri_ref, yp_ref, xc_ref, h_scr, *, nchunks):
    q = SSD_Q
    n = pl.program_id(1)

    @pl.when(n == 0)
    def _():
        h_scr[...] = jnp.zeros_like(h_scr)

    dt, acc_ = _ssd_decay(dt_ref, dtb_ref, alog_ref, tri_ref)
    ex = jnp.exp(acc_)
    acc_t = acc_.T
    dt_t = dt.T
    xs = _ssd_conv(cur_ref, prev_ref, next_ref, cw_ref, cb_ref, shift_ref, n, nchunks - 1)
    for k in range(len(xs)):
        xc_ref[k] = xs[k].astype(BF16)
    nx = SSD_HEADS // 2
    xh = xs[0:nx]
    bm = [t.astype(BF16) for t in xs[nx:nx + SSD_GROUPS]]
    cm = [t.astype(BF16) for t in xs[nx + SSD_GROUPS:nx + 2 * SSD_GROUPS]]
    lane_lo = lax.broadcasted_iota(jnp.int32, (q, LANE), 1) < SSD_P

    ti = lax.broadcasted_iota(jnp.int32, (q, q), 0)
    si = lax.broadcasted_iota(jnp.int32, (q, q), 1)
    lower = si <= ti
    upper = si >= ti
    nt = (((1,), (1,)), ((), ()))
    heads_per_grp = SSD_HEADS // SSD_GROUPS
    y = []
    for j in range(nx):
        g = (2 * j) // heads_per_grp
        cb = lax.dot_general(cm[g], bm[g], nt, preferred_element_type=F32)
        parts = []
        for h in (2 * j, 2 * j + 1):
            hb = SSD_HEADS + h
            lf = jnp.where(lower, jnp.exp(acc_[:, h:h + 1] - acc_t[h:h + 1, :]), 0.0)
            lb = jnp.where(upper, jnp.exp(acc_[:, hb:hb + 1] - acc_t[hb:hb + 1, :]), 0.0)
            mh = (cb * (lf * dt_t[h:h + 1, :] + lb * dt_t[hb:hb + 1, :])).astype(BF16)
            parts.append(jnp.dot(mh, xh[j].astype(BF16), preferred_element_type=F32))
        y.append(jnp.where(lane_lo, parts[0], parts[1]))

    y_off = _ssd_state_step(xh, bm, cm, dt, acc_, ex, h_scr, False, None)
    for j in range(nx):
        yp_ref[j] = y[j] + y_off[j]


def _ssd_bwd_kernel(xc_ref, dt_ref, dtb_ref, alog_ref, tri_ref, expand_ref, yp_ref, z_ref, skip_ref,
                    norm_ref, y_ref, h_scr):
    q = SSD_Q
    n = pl.program_id(1)

    @pl.when(n == 0)
    def _():
        h_scr[...] = jnp.zeros_like(h_scr)

    dt, acc_ = _ssd_decay(dt_ref, dtb_ref, alog_ref, tri_ref)
    ex = jnp.exp(acc_)
    nx = SSD_HEADS // 2
    xh = [xc_ref[k].astype(F32) for k in range(nx)]
    bm = [xc_ref[nx + g] for g in range(SSD_GROUPS)]
    cm = [xc_ref[nx + SSD_GROUPS + g] for g in range(SSD_GROUPS)]
    y_off = _ssd_state_step(xh, bm, cm, dt, acc_, ex, h_scr, True, expand_ref)

    ys = []
    ssq = jnp.zeros((q, 1), F32)
    for j in range(nx):
        z = z_ref[j].astype(F32)
        yj = (yp_ref[j] + y_off[j] + skip_ref[j] * xh[j]) * (z * _sigmoid(z))
        ssq = ssq + jnp.sum(yj * yj, axis=-1, keepdims=True)
        ys.append(yj)
    r = lax.rsqrt(ssq * (1.0 / (nx * LANE)) + EPS)
    for j in range(nx):
        y_ref[j] = (ys[j] * r * norm_ref[j]).astype(BF16)


def _ssd(u3, udt, conv_w_l, conv_b_l, dt_bias_l, a_log_l, skip_l, norm_l, shift, tri, expand, bsz, seq):
    q = SSD_Q
    nchunks = seq // q
    nxbc = BLK_SSD_Z - BLK_SSD_X
    nx = SSD_HEADS // 2
    hb = q // HALO
    tb = (bsz * seq) // HALO

    cw = conv_w_l.astype(F32).reshape(CONV_TAPS, nxbc, 1, LANE)
    cb = conv_b_l.astype(F32).reshape(nxbc, 1, LANE)
    dtb = jnp.zeros((1, LANE), F32).at[0, :N_DT].set(dt_bias_l.astype(F32).reshape(-1))
    alog = jnp.zeros((1, LANE), F32).at[0, :N_DT].set(a_log_l.astype(F32).reshape(-1))
    skip = jnp.repeat(skip_l.astype(F32), SSD_P).reshape(nx, 1, LANE)
    norm = norm_l.astype(F32).reshape(nx, 1, LANE)

    def cur(b, n):
        return (BLK_SSD_X // nxbc, b * nchunks + n, 0)

    def prev(b, n):
        return (BLK_SSD_X // nxbc, jnp.maximum((b * nchunks + n) * hb - 1, 0), 0)

    def nxt(b, n):
        return (BLK_SSD_X // nxbc, jnp.minimum((b * nchunks + n + 1) * hb, tb - 1), 0)

    def const(arr):
        return pl.BlockSpec(arr.shape, lambda b, n: (0,) * arr.ndim)

    state = pltpu.VMEM((nx, SSD_N, LANE), F32)
    fwd_blk = lambda b, n: (0, b * nchunks + n, 0)
    ypart, xconv = pl.pallas_call(
        functools.partial(_ssd_fwd_kernel, nchunks=nchunks),
        grid=(bsz, nchunks),
        in_specs=[pl.BlockSpec((nxbc, q, LANE), cur),
                  pl.BlockSpec((nxbc, HALO, LANE), prev),
                  pl.BlockSpec((nxbc, HALO, LANE), nxt),
                  pl.BlockSpec((q, LANE), lambda b, n: (b * nchunks + n, 0)),
                  const(cw), const(cb), const(dtb), const(alog), const(shift), const(tri)],
        out_specs=[pl.BlockSpec((nx, q, LANE), fwd_blk), pl.BlockSpec((nxbc, q, LANE), fwd_blk)],
        out_shape=[jax.ShapeDtypeStruct((nx, bsz * seq, LANE), F32),
                   jax.ShapeDtypeStruct((nxbc, bsz * seq, LANE), BF16)],
        scratch_shapes=[state],
        compiler_params=_cparams(("parallel", "arbitrary")),
    )(u3, u3, u3, udt, cw, cb, dtb, alog, shift, tri)

    bwd_blk = lambda b, n: (0, b * nchunks + nchunks - 1 - n, 0)
    return pl.pallas_call(
        _ssd_bwd_kernel,
        grid=(bsz, nchunks),
        in_specs=[pl.BlockSpec((nxbc, q, LANE), bwd_blk),
                  pl.BlockSpec((q, LANE), lambda b, n: (b * nchunks + nchunks - 1 - n, 0)),
                  const(dtb), const(alog), const(tri), const(expand),
                  pl.BlockSpec((nx, q, LANE), bwd_blk),
                  pl.BlockSpec((nx, q, LANE), lambda b, n: (BLK_SSD_Z // nx, b * nchunks + nchunks - 1 - n, 0)),
                  const(skip), const(norm)],
        out_specs=pl.BlockSpec((nx, q, LANE), bwd_blk),
        out_shape=jax.ShapeDtypeStruct((nx, bsz * seq, LANE), BF16),
        scratch_shapes=[state],
        compiler_params=_cparams(("parallel", "arbitrary")),
    )(xconv, udt, dtb, alog, tri, expand, ypart, u3, skip, norm)


def _cat_blocks(ref, lo, n):
    return jnp.concatenate([ref[lo + k] for k in range(n)], axis=-1)


def _merge_kernel(yhg_ref, yna_ref, ylru_ref, yssd_ref, gt_ref, x_ref, mod_ref,
                  whg_ref, wna_ref, wlru_ref, wssd_ref, wout_ref, o_ref, m_scr):
    nd = D_MODEL // LANE
    per = MXU_N // LANE
    nchunk = D_MODEL // MXU_N
    branches = ((yhg_ref, whg_ref), (yna_ref, wna_ref), (ylru_ref, wlru_ref), (yssd_ref, wssd_ref))
    ys = [_cat_blocks(y_ref, 0, y_ref.shape[0]) for y_ref, _ in branches]

    prods = []

    def gate_sum(c):
        tot = None
        for i in range(len(branches)):
            term = _cat_blocks(gt_ref, nd * i + per * c, per).astype(F32) * prods[c][i]
            tot = term if tot is None else tot + term
        m_scr[:, c * MXU_N:(c + 1) * MXU_N] = tot.astype(BF16)

    for c in range(nchunk):
        cols = slice(c * MXU_N, (c + 1) * MXU_N)
        prods.append([jnp.dot(y, w_ref[:, cols], preferred_element_type=F32)
                      for y, (_, w_ref) in zip(ys, branches)])
        if c > 0:
            gate_sum(c - 1)
    gate_sum(nchunk - 1)

    mb = m_scr[...]
    gate = mod_ref[0][:, 2 * D_MODEL:3 * D_MODEL]
    outs = []

    def residual(c):
        cols = slice(c * MXU_N, (c + 1) * MXU_N)
        o_ref[:, cols] = x_ref[:, cols] + gate[:, cols] * outs[c]

    for c in range(nchunk):
        outs.append(jnp.dot(mb, wout_ref[:, c * MXU_N:(c + 1) * MXU_N], preferred_element_type=F32))
        if c > 0:
            residual(c - 1)
    residual(nchunk - 1)


def _merge(y_hg, y_na, y_lru, y_ssd, u3, x2, mod_l, w_hg, w_na, w_lru, w_ssd, w_out, seq, tm=512):
    t = x2.shape[0]
    tm = min(tm, seq)

    def yspec(arr):
        return pl.BlockSpec((arr.shape[0], tm, LANE), lambda i: (0, i, 0))

    def wspec(arr):
        return pl.BlockSpec(arr.shape, lambda i: (0, 0))

    return pl.pallas_call(
        _merge_kernel,
        grid=(t // tm,),
        in_specs=[yspec(y_hg), yspec(y_na), yspec(y_lru), yspec(y_ssd),
                  pl.BlockSpec((N_GATE, tm, LANE), lambda i: (BLK_MERGE // N_GATE, i, 0)),
                  pl.BlockSpec((tm, D_MODEL), lambda i: (i, 0)),
                  pl.BlockSpec((1, 1, 6 * D_MODEL), lambda i: ((i * tm) // seq, 0, 0)),
                  wspec(w_hg), wspec(w_na), wspec(w_lru), wspec(w_ssd), wspec(w_out)],
        out_specs=pl.BlockSpec((tm, D_MODEL), lambda i: (i, 0)),
        out_shape=jax.ShapeDtypeStruct((t, D_MODEL), F32),
        scratch_shapes=[pltpu.VMEM((tm, D_MODEL), BF16)],
        compiler_params=_cparams(("parallel",)),
    )(y_hg, y_na, y_lru, y_ssd, u3, x2, mod_l, w_hg, w_na, w_lru, w_ssd, w_out)


def _ffn_kernel(x_ref, mod_ref, g_ref, w1_ref, w2_ref, gfin_ref, o_ref, h_scr, e_scr, acc_scr, *, final_norm):
    j = pl.program_id(1)
    nj = pl.num_programs(1)

    @pl.when(j == 0)
    def _():
        acc_scr[...] = jnp.zeros_like(acc_scr)
        x = x_ref[...]
        ms = jnp.mean(x * x, axis=-1, keepdims=True)
        y = x * lax.rsqrt(ms + EPS) * g_ref[...]
        mod = mod_ref[0]
        shift = mod[:, 3 * D_MODEL:4 * D_MODEL]
        scale = mod[:, 4 * D_MODEL:5 * D_MODEL]
        h_scr[...] = (y * (1.0 + scale) + shift).astype(BF16)

    hb = h_scr[...]
    tf = w1_ref.shape[1]
    hids = []

    def square(c):
        hid = jnp.maximum(hids[c], 0.0)
        e_scr[:, c * MXU_N:(c + 1) * MXU_N] = (hid * hid).astype(BF16)

    for c in range(tf // MXU_N):
        hids.append(jnp.dot(hb, w1_ref[:, c * MXU_N:(c + 1) * MXU_N], preferred_element_type=F32))
        if c > 0:
            square(c - 1)
    square(tf // MXU_N - 1)

    eb = e_scr[...]
    parts = []

    def accumulate(c):
        cols = slice(c * MXU_N, (c + 1) * MXU_N)
        acc_scr[:, cols] = acc_scr[:, cols] + parts[c]

    for c in range(D_MODEL // MXU_N):
        parts.append(jnp.dot(eb, w2_ref[:, c * MXU_N:(c + 1) * MXU_N], preferred_element_type=F32))
        if c > 0:
            accumulate(c - 1)
    accumulate(D_MODEL // MXU_N - 1)

    @pl.when(j == nj - 1)
    def _():
        gate = mod_ref[0][:, 5 * D_MODEL:6 * D_MODEL]
        xo = x_ref[...] + gate * acc_scr[...]
        if final_norm:
            ms = jnp.mean(xo * xo, axis=-1, keepdims=True)
            xo = xo * lax.rsqrt(ms + EPS) * gfin_ref[...]
        o_ref[...] = xo


def _ffn(x2, mod_l, g, w1, w2, g_final, seq, final_norm, tm=1024, tf=1024):
    t = x2.shape[0]
    tm = min(tm, seq)
    return pl.pallas_call(
        functools.partial(_ffn_kernel, final_norm=final_norm),
        grid=(t // tm, D_FF // tf),
        in_specs=[pl.BlockSpec((tm, D_MODEL), lambda i, j: (i, 0)),
                  pl.BlockSpec((1, 1, 6 * D_MODEL), lambda i, j: ((i * tm) // seq, 0, 0)),
                  pl.BlockSpec((1, D_MODEL), lambda i, j: (0, 0)),
                  pl.BlockSpec((D_MODEL, tf), lambda i, j: (0, j)),
                  pl.BlockSpec((tf, D_MODEL), lambda i, j: (j, 0)),
                  pl.BlockSpec((1, D_MODEL), lambda i, j: (0, 0))],
        out_specs=pl.BlockSpec((tm, D_MODEL), lambda i, j: (i, 0)),
        out_shape=jax.ShapeDtypeStruct((t, D_MODEL), F32),
        scratch_shapes=[pltpu.VMEM((tm, D_MODEL), BF16), pltpu.VMEM((tm, tf), BF16),
                        pltpu.VMEM((tm, D_MODEL), F32)],
        compiler_params=_cparams(("parallel", "arbitrary")),
    )(x2, mod_l, g, w1, w2, g_final)


W_IN_ORDER = ((512, 1536),
              (8224, 12320),
              (0, 512),
              (1536, 2560),
              (2560, 5120),
              (6144, 8192),
              (5120, 6144))
W_IN_DT = (8192, 8224)


def _split_w_in(w_in_l):
    main = jnp.concatenate([w_in_l[:, a:b] for a, b in W_IN_ORDER], axis=1).astype(BF16)
    dt = jnp.zeros((D_MODEL, LANE), BF16).at[:, :N_DT].set(w_in_l[:, W_IN_DT[0]:W_IN_DT[1]].astype(BF16))
    return main, dt


def kernel(x, c, w_ada, b_ada, g_mix, g_ffn, w_in, hg_lb, hg_norm, na_rpb, lru_conv_w, lru_conv_b, lru_gate_w, lru_gate_b, lru_lambda, ssd_conv_w, ssd_conv_b, ssd_dt_bias, ssd_a_log, ssd_skip, ssd_norm, w_br_hg, w_br_na, w_br_lru, w_br_ssd, w_out, w_ff1, w_ff2, g_final):
    bsz, seq, d = x.shape
    depth = w_in.shape[0]
    rows = seq // GRID_W
    assert d == D_MODEL and seq % LRU_TC == 0 and rows % NA_G == 0 and rows >= 2 * NA_G

    mod = _ada_mod(c, w_ada, b_ada).reshape(depth, bsz, 1, 6 * D_MODEL)
    lbp = _hg_lower_bounds(hg_lb)
    hg_sums_np, hg_masks_np = _hgrn_consts()
    hg_sums = jnp.asarray(hg_sums_np, BF16)
    hg_masks = jnp.asarray(hg_masks_np)
    shift = jnp.asarray(_conv_shift_matrix(), BF16)
    tri = hg_sums[:, :HG_CHUNK, :]
    expand = jnp.asarray(_ssd_expand_matrix(), BF16)
    gfin = g_final.reshape(1, D_MODEL)

    x2 = x.reshape(bsz * seq, D_MODEL)
    for l in range(depth):
        w_main, w_dt = _split_w_in(w_in[l])
        uz, u3, udt = _inproj(x2, mod[l], g_mix[l].reshape(1, D_MODEL), w_main, w_dt, seq)
        y_hg = _hgrn(u3, uz, lbp[:, l], hg_norm[l].reshape(1, -1), hg_sums, hg_masks, bsz, seq)
        y_na = _na(u3, _na_bias_table(na_rpb[l]), bsz, seq)
        gwd, gbd, lamd = _lru_weights(lru_gate_w[l], lru_gate_b[l], lru_lambda[l])
        y_lru = _lru(u3, lru_conv_w[l], lru_conv_b[l], gwd, gbd, lamd, shift, bsz, seq)
        y_ssd = _ssd(u3, udt, ssd_conv_w[l], ssd_conv_b[l], ssd_dt_bias[l], ssd_a_log[l],
                     ssd_skip[l], ssd_norm[l], shift, tri, expand, bsz, seq)
        x2 = _merge(y_hg, y_na, y_lru, y_ssd, u3, x2, mod[l],
                    w_br_hg[l].astype(BF16), w_br_na[l].astype(BF16), w_br_lru[l].astype(BF16),
                    w_br_ssd[l].astype(BF16), w_out[l].astype(BF16), seq)
        x2 = _ffn(x2, mod[l], g_ffn[l].reshape(1, D_MODEL), w_ff1[l].astype(BF16), w_ff2[l].astype(BF16),
                  gfin, seq, final_norm=(l == depth - 1))
    return x2.reshape(bsz, seq, D_MODEL)
```

```python
import functools

import numpy as np
import jax
import jax.numpy as jnp
from jax import lax
from jax.experimental import pallas as pl
from jax.experimental.pallas import tpu as pltpu

F32 = jnp.float32
BF16 = jnp.bfloat16

LANE = 128
SUB = 8
MXU_N = 256
D_MODEL = 1024
D_FF = 4 * D_MODEL
GRID_W = 64
EPS = 1e-6

HG_HEADS = 4
HG_HPS = 2
HG_CHUNK = 128
HG_LEVELS = (64, 32, 16, 8, 4, 2, 1)

NA_ROWS = 8
NA_COLS = 16
NA_HEADS = 8
NA_DH = 64
NA_G = 8
NA_GPS = 2

LRU_WIDTH = 512
LRU_BLOCKS = 8
LRU_BW = 64
LRU_C = 8.0
LRU_CW = 256
LRU_TC = 512
LRU_TINY = 1e-37

SSD_HEADS = 16
SSD_P = 64
SSD_GROUPS = 4
SSD_N = 128
SSD_Q = 128
SSD_SUB = 2
HALO = 16
CONV_ROWS = 128
CONV_TAPS = 4

BLK_MERGE = 0
BLK_HG_Q, BLK_HG_I, BLK_HG_G = 32, 36, 40
BLK_NA_Q, BLK_NA_K, BLK_NA_V = 44, 48, 52
BLK_LRU_X, BLK_LRU_G = 56, 60
BLK_SSD_X, BLK_SSD_B, BLK_SSD_C, BLK_SSD_Z = 64, 72, 76, 80
N_BLK = 88
N_ZBLK = 8
N_GATE = 32
N_MAIN = (N_ZBLK + N_BLK) * LANE
N_DT = 2 * SSD_HEADS

VMEM_LIMIT = 48 * 1024 * 1024


def _cparams(sem):
    return pltpu.CompilerParams(dimension_semantics=sem, vmem_limit_bytes=VMEM_LIMIT)


def _sigmoid(x):
    return 0.5 * jnp.tanh(0.5 * x) + 0.5


def _softplus(x):
    return jnp.maximum(x, 0.0) + jnp.log1p(jnp.exp(-jnp.abs(x)))


def _split_bf16(x):
    hi = x.astype(BF16)
    lo = (x - hi.astype(F32)).astype(BF16)
    return jnp.concatenate([hi, lo], axis=1)


def _join_f32(r):
    half = r.shape[1] // 2
    return r[:, :half] + r[:, half:]


def _conv_shift_matrix():
    r = CONV_ROWS
    offs = [j - CONV_TAPS // 2 for j in range(CONV_TAPS) if j != CONV_TAPS // 2]
    m = np.zeros((len(offs), r, r + 2 * HALO), np.float32)
    for i, o in enumerate(offs):
        m[i, np.arange(r), HALO + np.arange(r) + o] = 1.0
    return m.reshape(len(offs) * r, r + 2 * HALO)


def _conv_taps(win, shift, wts, bias):
    r = CONV_ROWS
    sh = jnp.dot(shift, win, preferred_element_type=F32)
    acc = bias + wts(CONV_TAPS // 2) * win[HALO:HALO + r].astype(F32)
    i = 0
    for j in range(CONV_TAPS):
        if j != CONV_TAPS // 2:
            acc = acc + wts(j) * sh[i * r:(i + 1) * r]
            i += 1
    return acc


def _ada_kernel(c_ref, w_ref, b_ref, o_ref):
    c = c_ref[...]
    ca = c * _sigmoid(c)
    o_ref[0] = jnp.dot(ca, w_ref[0], preferred_element_type=F32,
                       precision=lax.Precision.HIGHEST) + b_ref[0]


def _ada_mod(c, w_ada, b_ada):
    depth, d, d6 = w_ada.shape
    bsz = c.shape[0]
    cp = jnp.zeros((SUB, d), F32).at[:bsz].set(c)
    out = pl.pallas_call(
        _ada_kernel,
        grid=(depth, d6 // d),
        in_specs=[pl.BlockSpec((SUB, d), lambda l, j: (0, 0)),
                  pl.BlockSpec((1, d, d), lambda l, j: (l, 0, j)),
                  pl.BlockSpec((1, 1, d), lambda l, j: (l, 0, j))],
        out_specs=pl.BlockSpec((1, SUB, d), lambda l, j: (l, 0, j)),
        out_shape=jax.ShapeDtypeStruct((depth, SUB, d6), F32),
        compiler_params=_cparams(("parallel", "parallel")),
    )(cp, w_ada, b_ada.reshape(depth, 1, d6))
    return out[:, :bsz]


def _lb_kernel(hg_ref, o_ref):
    depth = hg_ref.shape[0]
    xs = [hg_ref[l] for l in range(depth)]
    m = xs[0]
    for x in xs[1:]:
        m = jnp.maximum(m, x)
    es = [jnp.exp(x - m) for x in xs]
    tot = es[0]
    for e in es[1:]:
        tot = tot + e
    cs = None
    first = None
    for l in range(depth):
        sm = es[l] / tot
        cs = sm if cs is None else cs + sm
        if l == 0:
            first = cs
        lb = cs - first
        o_ref[0, l] = lb
        o_ref[1, l] = jnp.log(lb)
        o_ref[2, l] = jnp.log1p(-lb)


def _hg_lower_bounds(hg_lb):
    two, depth, k = hg_lb.shape
    return pl.pallas_call(
        _lb_kernel,
        out_shape=jax.ShapeDtypeStruct((3, depth, two, k), F32),
    )(hg_lb.transpose(1, 0, 2))


def _inproj_kernel(x_ref, mod_ref, g_ref, w_ref, wdt_ref, uz_ref, u_ref, udt_ref, h_scr, *, tn, z_tiles, gate_tiles):
    j = pl.program_id(1)
    nb = tn // LANE

    @pl.when(j == 0)
    def _():
        x = x_ref[...]
        ms = jnp.mean(x * x, axis=-1, keepdims=True)
        y = x * lax.rsqrt(ms + EPS) * g_ref[...]
        mod = mod_ref[0]
        shift = mod[:, 0:D_MODEL]
        scale = mod[:, D_MODEL:2 * D_MODEL]
        hb = (y * (1.0 + scale) + shift).astype(BF16)
        h_scr[...] = hb
        udt_ref[...] = jnp.dot(hb, wdt_ref[...], preferred_element_type=F32)

    is_gate = jnp.logical_and(j >= z_tiles, j < z_tiles + gate_tiles)
    hb = h_scr[...]
    per = MXU_N // LANE
    accs = []

    def epilogue(c):
        val = jnp.where(is_gate, _sigmoid(accs[c]), accs[c]).astype(BF16)
        for k in range(per):
            u_ref[c * per + k] = val[:, k * LANE:(k + 1) * LANE]

    for c in range(tn // MXU_N):
        accs.append(jnp.dot(hb, w_ref[:, c * MXU_N:(c + 1) * MXU_N], preferred_element_type=F32))
        if c > 0:
            epilogue(c - 1)
    epilogue(tn // MXU_N - 1)

    @pl.when(j < z_tiles)
    def _():
        for c in range(tn // MXU_N):
            for k in range(per):
                uz_ref[c * per + k] = accs[c][:, k * LANE:(k + 1) * LANE]


def _inproj(x2, mod_l, g, w_main, w_dt, seq, tm=1024, tn=1024):
    t = x2.shape[0]
    tm = min(tm, seq)
    nb = tn // LANE
    assert N_ZBLK % nb == 0 and N_GATE % nb == 0
    z_tiles = N_ZBLK // nb
    kern = functools.partial(_inproj_kernel, tn=tn, z_tiles=z_tiles, gate_tiles=N_GATE // nb)
    return pl.pallas_call(
        kern,
        grid=(t // tm, N_MAIN // tn),
        in_specs=[pl.BlockSpec((tm, D_MODEL), lambda i, j: (i, 0)),
                  pl.BlockSpec((1, 1, 6 * D_MODEL), lambda i, j: ((i * tm) // seq, 0, 0)),
                  pl.BlockSpec((1, D_MODEL), lambda i, j: (0, 0)),
                  pl.BlockSpec((D_MODEL, tn), lambda i, j: (0, j)),
                  pl.BlockSpec((D_MODEL, LANE), lambda i, j: (0, 0))],
        out_specs=[pl.BlockSpec((nb, tm, LANE), lambda i, j: (jnp.minimum(j, z_tiles - 1), i, 0)),
                   pl.BlockSpec((nb, tm, LANE), lambda i, j: (jnp.maximum(j - z_tiles, 0), i, 0)),
                   pl.BlockSpec((tm, LANE), lambda i, j: (i, 0))],
        out_shape=[jax.ShapeDtypeStruct((N_ZBLK, t, LANE), F32),
                   jax.ShapeDtypeStruct((N_BLK, t, LANE), BF16),
                   jax.ShapeDtypeStruct((t, LANE), F32)],
        scratch_shapes=[pltpu.VMEM((tm, D_MODEL), BF16)],
        compiler_params=_cparams(("parallel", "arbitrary")),
    )(x2, mod_l, g, w_main, w_dt)


def _hgrn_consts():
    c = HG_CHUNK
    nlev = len(HG_LEVELS)
    t = np.arange(c)[:, None]
    r = np.arange(c)[None, :]
    sums = np.zeros((2, 1 + nlev, c, c), np.float32)
    masks = np.zeros((2, nlev + 1, c, c), np.float32)
    sums[0, 0] = r <= t
    sums[1, 0] = r >= t
    for li, m in enumerate(HG_LEVELS):
        grp = t // (2 * m)
        ref = grp * 2 * m + m - 1
        sums[0, 1 + li] = ((t > ref) & (r > ref) & (r <= t)) | ((t <= ref) & (r > t) & (r <= ref))
        ref = grp * 2 * m + m
        sums[1, 1 + li] = ((t < ref) & (r >= t) & (r < ref)) | ((t >= ref) & (r >= ref) & (r < t))
        own = ((t // (2 * m)) == (r // (2 * m))) & ((t % (2 * m)) >= m) & ((r % (2 * m)) < m)
        masks[0, li] = own
        masks[1, li] = own.T
    masks[:, nlev] = np.eye(c)
    return sums.reshape(2, (1 + nlev) * c, c), masks


def _hgrn_chunks(streams, sums_ref, msk_ref):
    c = HG_CHUNK
    nlev = len(HG_LEVELS)
    nt = (((1,), (1,)), ((), ()))

    keys, pieces = [], []
    for (q, z, v, lb, llb, l1m, st, rev) in streams:
        logsig = jnp.minimum(z, 0.0) - jnp.log(1.0 + jnp.exp(-jnp.abs(z)))
        keys.append((1.0 - lb) * jnp.exp(logsig - z))
        y2 = l1m + logsig
        lf = jnp.maximum(llb, y2) + jnp.log(1.0 + jnp.exp(-jnp.abs(llb - y2)))
        pieces.append(_split_bf16(lf))

    ex_alls = []
    for (q, z, v, lb, llb, l1m, st, rev), p in zip(streams, pieces):
        ex_alls.append(_join_f32(jnp.dot(sums_ref[1 if rev else 0], p, preferred_element_type=F32)))

    qfs = [s[0].astype(F32) for s in streams]
    accs = []
    for (q, z, v, lb, llb, l1m, st, rev), key in zip(streams, keys):
        d = 1 if rev else 0
        accs.append(lax.dot_general(q, key.astype(BF16), nt, preferred_element_type=F32) * msk_ref[d, nlev])
    for li in range(nlev):
        for si, ((q, z, v, lb, llb, l1m, st, rev), key) in enumerate(zip(streams, keys)):
            d = 1 if rev else 0
            ex = jnp.exp(ex_alls[si][(1 + li) * c:(2 + li) * c])
            sl = lax.dot_general((qfs[si] * ex).astype(BF16), (key * ex).astype(BF16), nt,
                                 preferred_element_type=F32)
            accs[si] = accs[si] + sl * msk_ref[d, li]

    outs = []
    for si, ((q, z, v, lb, llb, l1m, st, rev), key) in enumerate(zip(streams, keys)):
        b = ex_alls[si][0:c]
        o = jnp.dot(accs[si].astype(BF16), v, preferred_element_type=F32)
        qs = (qfs[si] * jnp.exp(b)).astype(BF16)
        o = o + lax.dot_general(qs, st.astype(BF16), nt, preferred_element_type=F32)
        bedge = b[0:1, :] if rev else b[c - 1:c, :]
        ks_ = (key * jnp.exp(bedge - b)).astype(BF16)
        vt = v.astype(F32).T.astype(BF16)
        st_new = st * jnp.exp(bedge) + jnp.dot(vt, ks_, preferred_element_type=F32)
        outs.append((o, st_new))
    return outs


def _hgrn_kernel(q_ref, zf_ref, zb_ref, v_ref, g_ref, lbp_ref, norm_ref, sums_ref, msk_ref, y_ref,
                 o_scr, st_scr, *, seq):
    c = HG_CHUNK
    n = seq // c
    half = n // 2

    def lanes(h):
        return slice(h * LANE, (h + 1) * LANE)

    def step(i, finalize):
        rf = pl.multiple_of(i * c, c)
        rb = pl.multiple_of((n - 1 - i) * c, c)
        streams = []
        for h in range(HG_HPS):
            for rev, r0, z_ref in ((False, rf, zf_ref), (True, rb, zb_ref)):
                d = 1 if rev else 0
                streams.append((q_ref[h, pl.ds(r0, c), :], z_ref[h, pl.ds(r0, c), :], v_ref[h, pl.ds(r0, c), :],
                                lbp_ref[0, d:d + 1, lanes(h)], lbp_ref[1, d:d + 1, lanes(h)],
                                lbp_ref[2, d:d + 1, lanes(h)], st_scr[2 * h + d], rev))
        outs = _hgrn_chunks(streams, sums_ref, msk_ref)
        for h in range(HG_HPS):
            for d, r0 in ((0, rf), (1, rb)):
                o, st = outs[2 * h + d]
                st_scr[2 * h + d] = st
                if not finalize:
                    o_scr[h, pl.ds(r0, c), :] = o
                else:
                    o = o + o_scr[h, pl.ds(r0, c), :]
                    o = o * lax.rsqrt(jnp.mean(o * o, axis=-1, keepdims=True) + EPS) * norm_ref[:, lanes(h)]
                    g = g_ref[h, pl.ds(r0, c), :].astype(F32)
                    y_ref[h, pl.ds(r0, c), :] = (o * (g * _sigmoid(g))).astype(BF16)

    def first_half(i, carry):
        step(i, False)
        return carry

    def second_half(i, carry):
        step(i, True)
        return carry

    st_scr[...] = jnp.zeros_like(st_scr)
    lax.fori_loop(0, half, first_half, 0)
    lax.fori_loop(half, n, second_half, 0)


def _hgrn(u3, uz, lbp_l, hg_norm_l, sums, masks, bsz, seq):
    hps = HG_HPS
    nlev = len(HG_LEVELS)
    assert (seq // HG_CHUNK) % 2 == 0

    def ublk(base):
        return pl.BlockSpec((hps, seq, LANE), lambda b, hp, base=base: (base // hps + hp, b, 0))

    def zblk(base):
        return pl.BlockSpec((hps, seq, LANE), lambda b, hp, base=base: (base // hps + hp, b, 0))

    kern = functools.partial(_hgrn_kernel, seq=seq)
    return pl.pallas_call(
        kern,
        grid=(bsz, HG_HEADS // hps),
        in_specs=[ublk(BLK_HG_Q), zblk(0), zblk(HG_HEADS), ublk(BLK_HG_I), ublk(BLK_HG_G),
                  pl.BlockSpec((3, 2, hps * LANE), lambda b, hp: (0, 0, hp)),
                  pl.BlockSpec((1, hps * LANE), lambda b, hp: (0, hp)),
                  pl.BlockSpec((2, (1 + nlev) * HG_CHUNK, HG_CHUNK), lambda b, hp: (0, 0, 0)),
                  pl.BlockSpec((2, nlev + 1, HG_CHUNK, HG_CHUNK), lambda b, hp: (0, 0, 0, 0))],
        out_specs=pl.BlockSpec((hps, seq, LANE), lambda b, hp: (hp, b, 0)),
        out_shape=jax.ShapeDtypeStruct((HG_HEADS, bsz * seq, LANE), BF16),
        scratch_shapes=[pltpu.VMEM((hps, seq, LANE), F32), pltpu.VMEM((2 * hps, LANE, LANE), F32)],
        compiler_params=_cparams(("parallel", "parallel")),
    )(u3, uz, uz, u3, u3, lbp_l, hg_norm_l, sums, masks)


NA_NEG = -1e30


def _na_bias_table(rpb_l):
    qc = np.arange(GRID_W)[:, None]
    kc = np.arange(GRID_W)[None, :]
    cstart = np.clip(qc - NA_COLS // 2, 0, GRID_W - NA_COLS)
    col_ok = (kc >= cstart) & (kc < cstart + NA_COLS)
    dc = np.clip(kc - qc, 1 - NA_COLS, NA_COLS - 1) + NA_COLS - 1
    onehot = (dc[None] == np.arange(2 * NA_COLS - 1)[:, None, None]).astype(np.float32)
    cols = jnp.einsum('hrc,cqk->hqrk', rpb_l.astype(F32), jnp.asarray(onehot),
                      precision=lax.Precision.HIGHEST)
    cols = jnp.where(col_ok[None, :, None, :], cols, NA_NEG)
    tab = jnp.stack([cols[:, :, NA_ROWS - 1 - s:2 * NA_ROWS - 1 - s] for s in range(NA_ROWS)], axis=1)
    return tab.reshape(NA_HEADS, NA_ROWS, GRID_W, NA_ROWS * GRID_W)


def _na_slot(variant, i):
    half = NA_ROWS // 2
    if variant == 0:
        return max(i - half, 0), min(i, half)
    if variant == 1:
        return i, half
    return (half + i, half) if i < half else (NA_G, i)


def _na_kernel(q_ref, k_ref, v_ref, tab_ref, o_ref, bias_scr, *, seq):
    rows = seq // GRID_W
    ngrp = rows // NA_G
    gq = NA_G * GRID_W
    kwin = 2 * NA_G * GRID_W
    kw = NA_ROWS * GRID_W
    nt = (((1,), (1,)), ((), ()))

    @pl.when(pl.program_id(1) == 0)
    def _():
        neg = jnp.full((GRID_W, LANE), NA_NEG, F32)
        for variant in range(3):
            for i in range(NA_G):
                a0, sft = _na_slot(variant, i)
                base = (a0 // 2) * LANE
                wid = min(kw + LANE, kwin - base)
                for h in range(2):
                    slab = jnp.concatenate([tab_ref[h, sft], neg], axis=1)
                    if a0 % 2:
                        slab = pltpu.roll(slab, GRID_W, 1)
                    pieces = [neg] * (base // LANE) + [slab[:, :wid]] + [neg] * ((kwin - base - wid) // LANE)
                    bias_scr[variant, h, i * GRID_W:(i + 1) * GRID_W, :] = jnp.concatenate(pieces, axis=1)

    head0 = lax.broadcasted_iota(jnp.int32, (gq, LANE), 1) < NA_DH
    zero = jnp.zeros((gq, LANE), BF16)

    def body(jj, carry):
        streams = []
        for g in range(NA_GPS):
            j = jj * NA_GPS + g
            variant = jnp.where(j == 0, 0, jnp.where(j == ngrp - 1, 2, 1))
            k0r = jnp.clip(j * NA_G - NA_ROWS // 2, 0, rows - 2 * NA_G)
            q0 = pl.multiple_of(j * gq, gq)
            k0 = pl.multiple_of(k0r * GRID_W, GRID_W)
            q = q_ref[0, pl.ds(q0, gq), :] * (NA_DH ** -0.5)
            kb = k_ref[0, pl.ds(k0, kwin), :]
            vb = v_ref[0, pl.ds(k0, kwin), :]
            streams.append((variant, 0, jnp.where(head0, q, zero), kb, vb, q0))
            streams.append((variant, 1, jnp.where(head0, zero, q), kb, vb, q0))
        ss = [lax.dot_general(qh, kb, nt, preferred_element_type=F32) for (_, _, qh, kb, _, _) in streams]
        ps, ls = [], []
        for (variant, h, _, _, _, _), s in zip(streams, ss):
            s = s + bias_scr[variant, h]
            p = jnp.exp(s - jnp.max(s, axis=-1, keepdims=True))
            ls.append(jnp.sum(p, axis=-1, keepdims=True))
            ps.append(p.astype(BF16))
        outs = [jnp.dot(p, vb, preferred_element_type=F32) for p, (_, _, _, _, vb, _) in zip(ps, streams)]
        for g in range(NA_GPS):
            q0 = streams[2 * g][5]
            o_ref[0, pl.ds(q0, gq), :] = jnp.where(head0, outs[2 * g] / ls[2 * g],
                                                   outs[2 * g + 1] / ls[2 * g + 1]).astype(BF16)
        return carry

    lax.fori_loop(0, ngrp // NA_GPS, body, 0)


def _na(u3, tab, bsz, seq):
    def ublk(base):
        return pl.BlockSpec((1, seq, LANE), lambda hp, b, base=base: (base + hp, b, 0))

    kern = functools.partial(_na_kernel, seq=seq)
    kw = NA_ROWS * GRID_W
    return pl.pallas_call(
        kern,
        grid=(NA_HEADS // 2, bsz),
        in_specs=[ublk(BLK_NA_Q), ublk(BLK_NA_K), ublk(BLK_NA_V),
                  pl.BlockSpec((2, NA_ROWS, GRID_W, kw), lambda hp, b: (hp, 0, 0, 0))],
        out_specs=pl.BlockSpec((1, seq, LANE), lambda hp, b: (hp, b, 0)),
        out_shape=jax.ShapeDtypeStruct((NA_HEADS // 2, bsz * seq, LANE), BF16),
        scratch_shapes=[pltpu.VMEM((3, 2, NA_G * GRID_W, 2 * NA_G * GRID_W), F32)],
        compiler_params=_cparams(("parallel", "arbitrary")),
    )(u3, u3, u3, tab)


def _gelu_tanh(x):
    return 0.5 * x * (1.0 + jnp.tanh(0.7978845608028654 * (x + 0.044715 * (x * x * x))))


def _lru_kernel(x_ref, gate_ref, cw_ref, cb_ref, gw_ref, gb_ref, lam_ref, shift_ref, y_ref,
                xpad, xf_scr, a_scr, u_scr, hf_scr, *, seq):
    w = LRU_CW
    tc = LRU_TC
    nch = seq // tc
    ngrp = seq // SUB
    nb = w // LANE
    r = CONV_ROWS

    zpad = jnp.zeros((HALO, w), BF16)
    xpad[0:HALO, :] = zpad
    xpad[seq + HALO:seq + 2 * HALO, :] = zpad
    for k in range(nb):
        xpad[HALO:seq + HALO, k * LANE:(k + 1) * LANE] = x_ref[k]

    sub = lax.broadcasted_iota(jnp.int32, (tc // SUB, SUB, w), 1)

    for d in range(2):
        rev = d == 1
        sp = _softplus(-lam_ref[0, d])

        def pass_a(i, carry, d=d, rev=rev, sp=sp):
            r0 = pl.multiple_of(i * tc, tc)
            if not rev:
                win = xpad[pl.ds(r0, tc + 2 * HALO), :]
                parts = [_conv_taps(win[s * r:(s + 1) * r + 2 * HALO], shift_ref[...],
                                    lambda j: cw_ref[j:j + 1, :], cb_ref[...]) for s in range(tc // r)]
                xf = jnp.concatenate(parts, axis=0)
                xf_scr[pl.ds(r0, tc), :] = xf
            else:
                xf = xf_scr[pl.ds(r0, tc), :]
            gts = jnp.dot(xf.astype(BF16), gw_ref[0, d], preferred_element_type=F32) + gb_ref[0, d]
            rg = _sigmoid(gts[:, 0:w])
            ig = _sigmoid(gts[:, w:2 * w])
            log_a = (-LRU_C) * rg * sp
            a = jnp.exp(log_a)
            y = -jnp.tanh(log_a) * (a * a + 1.0)
            u = y * lax.rsqrt(jnp.maximum(y, LRU_TINY)) * (ig * xf)
            a = a.reshape(tc // SUB, SUB, w)
            u = u.reshape(tc // SUB, SUB, w)
            for dd in (1, 2, 4):
                sh = (SUB - dd) if rev else dd
                ok = (sub < SUB - dd) if rev else (sub >= dd)
                a_s = pltpu.roll(a, sh, 1)
                u_s = pltpu.roll(u, sh, 1)
                u = jnp.where(ok, a * u_s + u, u)
                a = jnp.where(ok, a * a_s, a)
            a_scr[pl.ds(r0, tc), :] = a.reshape(tc, w)
            u_scr[pl.ds(r0, tc), :] = u.reshape(tc, w)
            return carry

        lax.fori_loop(0, nch, pass_a, 0)

        def pass_b(g, h, rev=rev):
            gi = (ngrp - 1 - g) if rev else g
            r0 = pl.multiple_of(gi * SUB, SUB)
            hp = h[0:1, :] if rev else h[SUB - 1:SUB, :]
            hn = u_scr[pl.ds(r0, SUB), :] + a_scr[pl.ds(r0, SUB), :] * hp
            if rev:
                u_scr[pl.ds(r0, SUB), :] = hn
            else:
                hf_scr[pl.ds(r0, SUB), :] = hn
            return hn

        lax.fori_loop(0, ngrp, pass_b, jnp.zeros((SUB, w), F32), unroll=8)

    def pass_c(i, carry):
        r0 = pl.multiple_of(i * tc, tc)
        hsum = hf_scr[pl.ds(r0, tc), :] + u_scr[pl.ds(r0, tc), :]
        for k in range(nb):
            g = gate_ref[k, pl.ds(r0, tc), :].astype(F32)
            y_ref[k, pl.ds(r0, tc), :] = (hsum[:, k * LANE:(k + 1) * LANE] * _gelu_tanh(g)).astype(BF16)
        return carry

    lax.fori_loop(0, nch, pass_c, 0)


def _lru_weights(gate_w_l, gate_b_l, lam_l):
    ncb = LRU_WIDTH // LRU_CW
    per = LRU_CW // LRU_BW
    gw = gate_w_l.astype(F32)
    eye = jnp.eye(per, dtype=F32)
    blocks = gw.reshape(2, 2, ncb, per, LRU_BW, LRU_BW)
    dense = jnp.einsum('dgcpio,pq->dgcpiqo', blocks, eye).reshape(2, 2, ncb, LRU_CW, LRU_CW)
    dense = dense.transpose(2, 0, 3, 1, 4).reshape(ncb, 2, LRU_CW, 2 * LRU_CW).astype(BF16)
    gb = gate_b_l.astype(F32).reshape(2, 2, ncb, LRU_CW).transpose(2, 0, 1, 3).reshape(ncb, 2, 1, 2 * LRU_CW)
    lam = lam_l.astype(F32).reshape(2, ncb, LRU_CW).transpose(1, 0, 2).reshape(ncb, 2, 1, LRU_CW)
    return dense, gb, lam


def _lru(u3, conv_w_l, conv_b_l, gwd, gbd, lamd, shift, bsz, seq):
    ncb = LRU_WIDTH // LRU_CW
    nb = LRU_CW // LANE
    kern = functools.partial(_lru_kernel, seq=seq)

    def ublk(base):
        return pl.BlockSpec((nb, seq, LANE), lambda b, cb, base=base: (base // nb + cb, b, 0))

    return pl.pallas_call(
        kern,
        grid=(bsz, ncb),
        in_specs=[ublk(BLK_LRU_X), ublk(BLK_LRU_G),
                  pl.BlockSpec((CONV_TAPS, LRU_CW), lambda b, cb: (0, cb)),
                  pl.BlockSpec((1, LRU_CW), lambda b, cb: (0, cb)),
                  pl.BlockSpec((1, 2, LRU_CW, 2 * LRU_CW), lambda b, cb: (cb, 0, 0, 0)),
                  pl.BlockSpec((1, 2, 1, 2 * LRU_CW), lambda b, cb: (cb, 0, 0, 0)),
                  pl.BlockSpec((1, 2, 1, LRU_CW), lambda b, cb: (cb, 0, 0, 0)),
                  pl.BlockSpec(shift.shape, lambda b, cb: (0, 0))],
        out_specs=pl.BlockSpec((nb, seq, LANE), lambda b, cb: (cb, b, 0)),
        out_shape=jax.ShapeDtypeStruct((LRU_WIDTH // LANE, bsz * seq, LANE), BF16),
        scratch_shapes=[pltpu.VMEM((seq + 2 * HALO, LRU_CW), BF16),
                        pltpu.VMEM((seq, LRU_CW), F32),
                        pltpu.VMEM((seq, LRU_CW), F32),
                        pltpu.VMEM((seq, LRU_CW), F32),
                        pltpu.VMEM((seq, LRU_CW), F32)],
        compiler_params=_cparams(("parallel", "parallel")),
    )(u3, u3, conv_w_l, conv_b_l.reshape(1, LRU_WIDTH), gwd, gbd, lamd, shift)


def _ssd_decay(dt_raw, dtb_ref, alog_ref, tri_ref):
    dt = _softplus(dt_raw + dtb_ref[...])
    a = dt * (-jnp.exp(alog_ref[...]))
    lane = lax.broadcasted_iota(jnp.int32, (SSD_Q, LANE), 1)
    ap = _split_bf16(a)
    pre = _join_f32(jnp.dot(tri_ref[0], ap, preferred_element_type=F32))
    suf = _join_f32(jnp.dot(tri_ref[1], ap, preferred_element_type=F32))
    return dt, jnp.where(lane < SSD_HEADS, pre, suf)


def _ssd_conv(cur_ref, prev_ref, next_ref, cw_ref, cb_ref, shift_ref, n, nlast, sc):
    q = SSD_Q
    nblk = cur_ref.shape[0]
    r0 = sc * q
    zero = jnp.zeros((HALO, LANE), BF16)
    wins = []
    for k in range(nblk):
        if sc == 0:
            pv = jnp.where(n > 0, prev_ref[k], zero)
        else:
            pv = cur_ref[k, r0 - HALO:r0, :]
        if sc == SSD_SUB - 1:
            nx = jnp.where(n < nlast, next_ref[k], zero)
        else:
            nx = cur_ref[k, r0 + q:r0 + q + HALO, :]
        wins.append(jnp.concatenate([pv, cur_ref[k, r0:r0 + q, :], nx], axis=0))
    win = jnp.concatenate(wins, axis=1)
    r = CONV_ROWS
    sh = jnp.dot(shift_ref[...], win, preferred_element_type=F32)
    xs = []
    for k in range(nblk):
        cols = slice(k * LANE, (k + 1) * LANE)
        acc = cb_ref[k] + cw_ref[CONV_TAPS // 2, k] * cur_ref[k, r0:r0 + q, :].astype(F32)
        i = 0
        for j in range(CONV_TAPS):
            if j != CONV_TAPS // 2:
                acc = acc + cw_ref[j, k] * sh[i * r:(i + 1) * r, cols]
                i += 1
        xs.append(acc * _sigmoid(acc))
    return xs


def _ssd_expand_matrix():
    nx = SSD_HEADS // 2
    m = np.zeros((2, LANE, nx * LANE), np.float32)
    for d in range(2):
        for j in range(nx):
            for half in range(2):
                m[d, d * SSD_HEADS + 2 * j + half, j * LANE + half * SSD_P:j * LANE + (half + 1) * SSD_P] = 1.0
    return np.concatenate([m, m], axis=1)


def _ssd_state_step(xh, bm, cm, dt, acc_, ex, h_scr, rev, expand_ref):
    q = SSD_Q
    d = 1 if rev else 0
    off = d * SSD_HEADS
    er = 0 if rev else q - 1
    edge = acc_[er:er + 1, :]
    lane = lax.broadcasted_iota(jnp.int32, (q, LANE), 1)
    if expand_ref is not None:
        mine = jnp.logical_and(lane >= off, lane < off + SSD_HEADS)
        wt = jnp.where(mine, jnp.exp(edge - acc_) * dt, 0.0)
        pex = jnp.dot(_split_bf16(ex), expand_ref[d], preferred_element_type=F32)
        pwt = jnp.dot(_split_bf16(wt), expand_ref[d], preferred_element_type=F32)
        pattern = lambda p, j: p[:, j * LANE:(j + 1) * LANE]
    else:
        wt = jnp.exp(edge - acc_) * dt
        pex, pwt = ex, wt
        lane_lo = lane < SSD_P

        def pattern(p, j):
            c0 = off + 2 * j
            return jnp.where(lane_lo, p[:, c0:c0 + 1], p[:, c0 + 1:c0 + 2])

    y_off = []
    per_grp = SSD_HEADS // SSD_GROUPS // 2
    for j in range(SSD_HEADS // 2):
        g = j // per_grp
        hj = h_scr[j]
        pe = pattern(pex, j)
        y_off.append(pe * jnp.dot(cm[g], hj.astype(BF16), preferred_element_type=F32))
        xt = (xh[j] * pattern(pwt, j)).astype(BF16)
        h_scr[j] = hj * pe[er:er + 1, :] + lax.dot_general(
            bm[g], xt, (((0,), (0,)), ((), ())), preferred_element_type=F32)
    return y_off


def _ssd_fwd_kernel(cur_ref, prev_ref, next_ref, dt_ref, cw_ref, cb_ref, dtb_ref, alog_ref,
                    shift_ref, tri_ref, yp_ref, xc_ref, h_scr, *, nsteps):
    q = SSD_Q
    n = pl.program_id(1)

    @pl.when(n == 0)
    def _():
        h_scr[...] = jnp.zeros_like(h_scr)

    nx = SSD_HEADS // 2
    lane_lo = lax.broadcasted_iota(jnp.int32, (q, LANE), 1) < SSD_P
    ti = lax.broadcasted_iota(jnp.int32, (q, q), 0)
    si = lax.broadcasted_iota(jnp.int32, (q, q), 1)
    lower = si <= ti
    upper = si >= ti
    nt = (((1,), (1,)), ((), ()))
    heads_per_grp = SSD_HEADS // SSD_GROUPS

    for sc in range(SSD_SUB):
        rows = slice(sc * q, (sc + 1) * q)
        dt, acc_ = _ssd_decay(dt_ref[rows, :], dtb_ref, alog_ref, tri_ref)
        ex = jnp.exp(acc_)
        acc_t = acc_.T
        dt_t = dt.T
        xs = _ssd_conv(cur_ref, prev_ref, next_ref, cw_ref, cb_ref, shift_ref, n, nsteps - 1, sc)
        for k in range(len(xs)):
            xc_ref[k, rows, :] = xs[k].astype(BF16)
        xh = xs[0:nx]
        bm = [t.astype(BF16) for t in xs[nx:nx + SSD_GROUPS]]
        cm = [t.astype(BF16) for t in xs[nx + SSD_GROUPS:nx + 2 * SSD_GROUPS]]

        y = []
        for j in range(nx):
            g = (2 * j) // heads_per_grp
            cb = lax.dot_general(cm[g], bm[g], nt, preferred_element_type=F32)
            parts = []
            for h in (2 * j, 2 * j + 1):
                hb = SSD_HEADS + h
                lf = jnp.where(lower, jnp.exp(acc_[:, h:h + 1] - acc_t[h:h + 1, :]), 0.0)
                lb = jnp.where(upper, jnp.exp(acc_[:, hb:hb + 1] - acc_t[hb:hb + 1, :]), 0.0)
                mh = (cb * (lf * dt_t[h:h + 1, :] + lb * dt_t[hb:hb + 1, :])).astype(BF16)
                parts.append(jnp.dot(mh, xh[j].astype(BF16), preferred_element_type=F32))
            y.append(jnp.where(lane_lo, parts[0], parts[1]))

        y_off = _ssd_state_step(xh, bm, cm, dt, acc_, ex, h_scr, False, None)
        for j in range(nx):
            yp_ref[j, rows, :] = y[j] + y_off[j]


def _ssd_bwd_kernel(xc_ref, dt_ref, dtb_ref, alog_ref, tri_ref, expand_ref, yp_ref, z_ref, skip_ref,
                    norm_ref, y_ref, h_scr):
    q = SSD_Q
    n = pl.program_id(1)

    @pl.when(n == 0)
    def _():
        h_scr[...] = jnp.zeros_like(h_scr)

    nx = SSD_HEADS // 2
    for sc in reversed(range(SSD_SUB)):
        rows = slice(sc * q, (sc + 1) * q)
        dt, acc_ = _ssd_decay(dt_ref[rows, :], dtb_ref, alog_ref, tri_ref)
        ex = jnp.exp(acc_)
        xh = [xc_ref[k, rows, :].astype(F32) for k in range(nx)]
        bm = [xc_ref[nx + g, rows, :] for g in range(SSD_GROUPS)]
        cm = [xc_ref[nx + SSD_GROUPS + g, rows, :] for g in range(SSD_GROUPS)]
        y_off = _ssd_state_step(xh, bm, cm, dt, acc_, ex, h_scr, True, expand_ref)

        ys = []
        ssq = jnp.zeros((q, 1), F32)
        for j in range(nx):
            z = z_ref[j, rows, :].astype(F32)
            yj = (yp_ref[j, rows, :] + y_off[j] + skip_ref[j] * xh[j]) * (z * _sigmoid(z))
            ssq = ssq + jnp.sum(yj * yj, axis=-1, keepdims=True)
            ys.append(yj)
        r = lax.rsqrt(ssq * (1.0 / (nx * LANE)) + EPS)
        for j in range(nx):
            y_ref[j, rows, :] = (ys[j] * r * norm_ref[j]).astype(BF16)


def _ssd(u3, udt, conv_w_l, conv_b_l, dt_bias_l, a_log_l, skip_l, norm_l, shift, tri, expand, bsz, seq):
    q = SSD_SUB * SSD_Q
    nchunks = seq // q
    nxbc = BLK_SSD_Z - BLK_SSD_X
    nx = SSD_HEADS // 2
    hb = q // HALO
    tb = (bsz * seq) // HALO

    cw = conv_w_l.astype(F32).reshape(CONV_TAPS, nxbc, 1, LANE)
    cb = conv_b_l.astype(F32).reshape(nxbc, 1, LANE)
    dtb = jnp.zeros((1, LANE), F32).at[0, :N_DT].set(dt_bias_l.astype(F32).reshape(-1))
    alog = jnp.zeros((1, LANE), F32).at[0, :N_DT].set(a_log_l.astype(F32).reshape(-1))
    skip = jnp.repeat(skip_l.astype(F32), SSD_P).reshape(nx, 1, LANE)
    norm = norm_l.astype(F32).reshape(nx, 1, LANE)

    def cur(b, n):
        return (BLK_SSD_X // nxbc, b * nchunks + n, 0)

    def prev(b, n):
        return (BLK_SSD_X // nxbc, jnp.maximum((b * nchunks + n) * hb - 1, 0), 0)

    def nxt(b, n):
        return (BLK_SSD_X // nxbc, jnp.minimum((b * nchunks + n + 1) * hb, tb - 1), 0)

    def const(arr):
        return pl.BlockSpec(arr.shape, lambda b, n: (0,) * arr.ndim)

    state = pltpu.VMEM((nx, SSD_N, LANE), F32)
    fwd_blk = lambda b, n: (0, b * nchunks + n, 0)
    ypart, xconv = pl.pallas_call(
        functools.partial(_ssd_fwd_kernel, nsteps=nchunks),
        grid=(bsz, nchunks),
        in_specs=[pl.BlockSpec((nxbc, q, LANE), cur),
                  pl.BlockSpec((nxbc, HALO, LANE), prev),
                  pl.BlockSpec((nxbc, HALO, LANE), nxt),
                  pl.BlockSpec((q, LANE), lambda b, n: (b * nchunks + n, 0)),
                  const(cw), const(cb), const(dtb), const(alog), const(shift), const(tri)],
        out_specs=[pl.BlockSpec((nx, q, LANE), fwd_blk), pl.BlockSpec((nxbc, q, LANE), fwd_blk)],
        out_shape=[jax.ShapeDtypeStruct((nx, bsz * seq, LANE), F32),
                   jax.ShapeDtypeStruct((nxbc, bsz * seq, LANE), BF16)],
        scratch_shapes=[state],
        compiler_params=_cparams(("parallel", "arbitrary")),
    )(u3, u3, u3, udt, cw, cb, dtb, alog, shift, tri)

    bwd_blk = lambda b, n: (0, b * nchunks + nchunks - 1 - n, 0)
    return pl.pallas_call(
        _ssd_bwd_kernel,
        grid=(bsz, nchunks),
        in_specs=[pl.BlockSpec((nxbc, q, LANE), bwd_blk),
                  pl.BlockSpec((q, LANE), lambda b, n: (b * nchunks + nchunks - 1 - n, 0)),
                  const(dtb), const(alog), const(tri), const(expand),
                  pl.BlockSpec((nx, q, LANE), bwd_blk),
                  pl.BlockSpec((nx, q, LANE), lambda b, n: (BLK_SSD_Z // nx, b * nchunks + nchunks - 1 - n, 0)),
                  const(skip), const(norm)],
        out_specs=pl.BlockSpec((nx, q, LANE), bwd_blk),
        out_shape=jax.ShapeDtypeStruct((nx, bsz * seq, LANE), BF16),
        scratch_shapes=[state],
        compiler_params=_cparams(("parallel", "arbitrary")),
    )(xconv, udt, dtb, alog, tri, expand, ypart, u3, skip, norm)


def _cat_blocks(ref, lo, n):
    return jnp.concatenate([ref[lo + k] for k in range(n)], axis=-1)


def _merge_kernel(yhg_ref, yna_ref, ylru_ref, yssd_ref, gt_ref, x_ref, mod_ref,
                  whg_ref, wna_ref, wlru_ref, wssd_ref, wout_ref, o_ref, m_scr):
    nd = D_MODEL // LANE
    per = MXU_N // LANE
    nchunk = D_MODEL // MXU_N
    branches = ((yhg_ref, whg_ref), (yna_ref, wna_ref), (ylru_ref, wlru_ref), (yssd_ref, wssd_ref))
    ys = [_cat_blocks(y_ref, 0, y_ref.shape[0]) for y_ref, _ in branches]

    prods = []

    def gate_sum(c):
        tot = None
        for i in range(len(branches)):
            term = _cat_blocks(gt_ref, nd * i + per * c, per).astype(F32) * prods[c][i]
            tot = term if tot is None else tot + term
        m_scr[:, c * MXU_N:(c + 1) * MXU_N] = tot.astype(BF16)

    for c in range(nchunk):
        cols = slice(c * MXU_N, (c + 1) * MXU_N)
        prods.append([jnp.dot(y, w_ref[:, cols], preferred_element_type=F32)
                      for y, (_, w_ref) in zip(ys, branches)])
        if c > 0:
            gate_sum(c - 1)
    gate_sum(nchunk - 1)

    mb = m_scr[...]
    gate = mod_ref[0][:, 2 * D_MODEL:3 * D_MODEL]
    outs = []

    def residual(c):
        cols = slice(c * MXU_N, (c + 1) * MXU_N)
        o_ref[:, cols] = x_ref[:, cols] + gate[:, cols] * outs[c]

    for c in range(nchunk):
        outs.append(jnp.dot(mb, wout_ref[:, c * MXU_N:(c + 1) * MXU_N], preferred_element_type=F32))
        if c > 0:
            residual(c - 1)
    residual(nchunk - 1)


def _merge(y_hg, y_na, y_lru, y_ssd, u3, x2, mod_l, w_hg, w_na, w_lru, w_ssd, w_out, seq, tm=512):
    t = x2.shape[0]
    tm = min(tm, seq)

    def yspec(arr):
        return pl.BlockSpec((arr.shape[0], tm, LANE), lambda i: (0, i, 0))

    def wspec(arr):
        return pl.BlockSpec(arr.shape, lambda i: (0, 0))

    return pl.pallas_call(
        _merge_kernel,
        grid=(t // tm,),
        in_specs=[yspec(y_hg), yspec(y_na), yspec(y_lru), yspec(y_ssd),
                  pl.BlockSpec((N_GATE, tm, LANE), lambda i: (BLK_MERGE // N_GATE, i, 0)),
                  pl.BlockSpec((tm, D_MODEL), lambda i: (i, 0)),
                  pl.BlockSpec((1, 1, 6 * D_MODEL), lambda i: ((i * tm) // seq, 0, 0)),
                  wspec(w_hg), wspec(w_na), wspec(w_lru), wspec(w_ssd), wspec(w_out)],
        out_specs=pl.BlockSpec((tm, D_MODEL), lambda i: (i, 0)),
        out_shape=jax.ShapeDtypeStruct((t, D_MODEL), F32),
        scratch_shapes=[pltpu.VMEM((tm, D_MODEL), BF16)],
        compiler_params=_cparams(("parallel",)),
    )(y_hg, y_na, y_lru, y_ssd, u3, x2, mod_l, w_hg, w_na, w_lru, w_ssd, w_out)


def _ffn_kernel(x_ref, mod_ref, g_ref, w1_ref, w2_ref, gfin_ref, o_ref, h_scr, e_scr, acc_scr, *, final_norm):
    j = pl.program_id(1)
    nj = pl.num_programs(1)

    @pl.when(j == 0)
    def _():
        acc_scr[...] = jnp.zeros_like(acc_scr)
        x = x_ref[...]
        ms = jnp.mean(x * x, axis=-1, keepdims=True)
        y = x * lax.rsqrt(ms + EPS) * g_ref[...]
        mod = mod_ref[0]
        shift = mod[:, 3 * D_MODEL:4 * D_MODEL]
        scale = mod[:, 4 * D_MODEL:5 * D_MODEL]
        h_scr[...] = (y * (1.0 + scale) + shift).astype(BF16)

    hb = h_scr[...]
    tf = w1_ref.shape[1]
    hids = []

    def square(c):
        hid = jnp.maximum(hids[c], 0.0)
        e_scr[:, c * MXU_N:(c + 1) * MXU_N] = (hid * hid).astype(BF16)

    for c in range(tf // MXU_N):
        hids.append(jnp.dot(hb, w1_ref[:, c * MXU_N:(c + 1) * MXU_N], preferred_element_type=F32))
        if c > 0:
            square(c - 1)
    square(tf // MXU_N - 1)

    eb = e_scr[...]
    parts = []

    def accumulate(c):
        cols = slice(c * MXU_N, (c + 1) * MXU_N)
        acc_scr[:, cols] = acc_scr[:, cols] + parts[c]

    for c in range(D_MODEL // MXU_N):
        parts.append(jnp.dot(eb, w2_ref[:, c * MXU_N:(c + 1) * MXU_N], preferred_element_type=F32))
        if c > 0:
            accumulate(c - 1)
    accumulate(D_MODEL // MXU_N - 1)

    @pl.when(j == nj - 1)
    def _():
        gate = mod_ref[0][:, 5 * D_MODEL:6 * D_MODEL]
        xo = x_ref[...] + gate * acc_scr[...]
        if final_norm:
            ms = jnp.mean(xo * xo, axis=-1, keepdims=True)
            xo = xo * lax.rsqrt(ms + EPS) * gfin_ref[...]
        o_ref[...] = xo


def _ffn(x2, mod_l, g, w1, w2, g_final, seq, final_norm, tm=1024, tf=1024):
    t = x2.shape[0]
    tm = min(tm, seq)
    return pl.pallas_call(
        functools.partial(_ffn_kernel, final_norm=final_norm),
        grid=(t // tm, D_FF // tf),
        in_specs=[pl.BlockSpec((tm, D_MODEL), lambda i, j: (i, 0)),
                  pl.BlockSpec((1, 1, 6 * D_MODEL), lambda i, j: ((i * tm) // seq, 0, 0)),
                  pl.BlockSpec((1, D_MODEL), lambda i, j: (0, 0)),
                  pl.BlockSpec((D_MODEL, tf), lambda i, j: (0, j)),
                  pl.BlockSpec((tf, D_MODEL), lambda i, j: (j, 0)),
                  pl.BlockSpec((1, D_MODEL), lambda i, j: (0, 0))],
        out_specs=pl.BlockSpec((tm, D_MODEL), lambda i, j: (i, 0)),
        out_shape=jax.ShapeDtypeStruct((t, D_MODEL), F32),
        scratch_shapes=[pltpu.VMEM((tm, D_MODEL), BF16), pltpu.VMEM((tm, tf), BF16),
                        pltpu.VMEM((tm, D_MODEL), F32)],
        compiler_params=_cparams(("parallel", "arbitrary")),
    )(x2, mod_l, g, w1, w2, g_final)


W_IN_ORDER = ((512, 1536),
              (8224, 12320),
              (0, 512),
              (1536, 2560),
              (2560, 5120),
              (6144, 8192),
              (5120, 6144))
W_IN_DT = (8192, 8224)


def _split_w_in(w_in_l):
    main = jnp.concatenate([w_in_l[:, a:b] for a, b in W_IN_ORDER], axis=1).astype(BF16)
    dt = jnp.zeros((D_MODEL, LANE), BF16).at[:, :N_DT].set(w_in_l[:, W_IN_DT[0]:W_IN_DT[1]].astype(BF16))
    return main, dt


def kernel(x, c, w_ada, b_ada, g_mix, g_ffn, w_in, hg_lb, hg_norm, na_rpb, lru_conv_w, lru_conv_b, lru_gate_w, lru_gate_b, lru_lambda, ssd_conv_w, ssd_conv_b, ssd_dt_bias, ssd_a_log, ssd_skip, ssd_norm, w_br_hg, w_br_na, w_br_lru, w_br_ssd, w_out, w_ff1, w_ff2, g_final):
    bsz, seq, d = x.shape
    depth = w_in.shape[0]
    rows = seq // GRID_W
    assert d == D_MODEL and seq % LRU_TC == 0 and rows % (NA_G * NA_GPS) == 0 and rows >= 2 * NA_G

    mod = _ada_mod(c, w_ada, b_ada).reshape(depth, bsz, 1, 6 * D_MODEL)
    lbp = _hg_lower_bounds(hg_lb)
    hg_sums_np, hg_masks_np = _hgrn_consts()
    hg_sums = jnp.asarray(hg_sums_np, BF16)
    hg_masks = jnp.asarray(hg_masks_np)
    shift = jnp.asarray(_conv_shift_matrix(), BF16)
    tri = hg_sums[:, :HG_CHUNK, :]
    expand = jnp.asarray(_ssd_expand_matrix(), BF16)
    gfin = g_final.reshape(1, D_MODEL)

    x2 = x.reshape(bsz * seq, D_MODEL)
    for l in range(depth):
        w_main, w_dt = _split_w_in(w_in[l])
        uz, u3, udt = _inproj(x2, mod[l], g_mix[l].reshape(1, D_MODEL), w_main, w_dt, seq)
        y_hg = _hgrn(u3, uz, lbp[:, l], hg_norm[l].reshape(1, -1), hg_sums, hg_masks, bsz, seq)
        y_na = _na(u3, _na_bias_table(na_rpb[l]), bsz, seq)
        gwd, gbd, lamd = _lru_weights(lru_gate_w[l], lru_gate_b[l], lru_lambda[l])
        y_lru = _lru(u3, lru_conv_w[l], lru_conv_b[l], gwd, gbd, lamd, shift, bsz, seq)
        y_ssd = _ssd(u3, udt, ssd_conv_w[l], ssd_conv_b[l], ssd_dt_bias[l], ssd_a_log[l],
                     ssd_skip[l], ssd_norm[l], shift, tri, expand, bsz, seq)
        x2 = _merge(y_hg, y_na, y_lru, y_ssd, u3, x2, mod[l],
                    w_br_hg[l].astype(BF16), w_br_na[l].astype(BF16), w_br_lru[l].astype(BF16),
                    w_br_ssd[l].astype(BF16), w_out[l].astype(BF16), seq)
        x2 = _ffn(x2, mod[l], g_ffn[l].reshape(1, D_MODEL), w_ff1[l].astype(BF16), w_ff2[l].astype(BF16),
                  gfin, seq, final_norm=(l == depth - 1))
    return x2.reshape(bsz, seq, D_MODEL)
```

```python
import functools

import numpy as np
import jax
import jax.numpy as jnp
from jax import lax
from jax.experimental import pallas as pl
from jax.experimental.pallas import tpu as pltpu

F32 = jnp.float32
BF16 = jnp.bfloat16

LANE = 128
SUB = 8
MXU_N = 256
D_MODEL = 1024
D_FF = 4 * D_MODEL
GRID_W = 64
EPS = 1e-6

HG_HEADS = 4
HG_HPS = 2
HG_CHUNK = 128
HG_LEVELS = (64, 32, 16, 8, 4, 2, 1)

NA_ROWS = 8
NA_COLS = 16
NA_HEADS = 8
NA_DH = 64
NA_G = 8
NA_GPS = 2

LRU_WIDTH = 512
LRU_BLOCKS = 8
LRU_BW = 64
LRU_C = 8.0
LRU_CW = 256
LRU_TC = 512
LRU_TINY = 1e-37

SSD_HEADS = 16
SSD_P = 64
SSD_GROUPS = 4
SSD_N = 128
SSD_Q = 128
SSD_SUB = 4
HALO = 16
CONV_ROWS = 128
CONV_TAPS = 4

BLK_MERGE = 0
BLK_ZCOPY = 32
BLK_HG_Q, BLK_HG_I, BLK_HG_G = 40, 44, 48
BLK_NA_Q, BLK_NA_K, BLK_NA_V = 52, 56, 60
BLK_SSD_X, BLK_SSD_B, BLK_SSD_C = 64, 72, 76
BLK_LRU_X, BLK_LRU_G = 80, 84
BLK_SSD_Z = 88
N_BLK = 96
N_XBC = 16
N_ZBLK = 8
N_GATE = 32
N_MAIN = N_BLK * LANE
N_DT = 2 * SSD_HEADS

VMEM_LIMIT = 48 * 1024 * 1024


def _cparams(sem):
    return pltpu.CompilerParams(dimension_semantics=sem, vmem_limit_bytes=VMEM_LIMIT)


def _sigmoid(x):
    return 0.5 * jnp.tanh(0.5 * x) + 0.5


def _softplus(x):
    return jnp.maximum(x, 0.0) + jnp.log1p(jnp.exp(-jnp.abs(x)))


def _split_bf16(x):
    hi = x.astype(BF16)
    lo = (x - hi.astype(F32)).astype(BF16)
    return jnp.concatenate([hi, lo], axis=1)


def _join_f32(r):
    half = r.shape[1] // 2
    return r[:, :half] + r[:, half:]


def _conv_shift_matrix():
    r = CONV_ROWS
    offs = [j - CONV_TAPS // 2 for j in range(CONV_TAPS) if j != CONV_TAPS // 2]
    m = np.zeros((len(offs), r, r + 2 * HALO), np.float32)
    for i, o in enumerate(offs):
        m[i, np.arange(r), HALO + np.arange(r) + o] = 1.0
    return m.reshape(len(offs) * r, r + 2 * HALO)


def _conv_taps(win, shift, wts, bias):
    r = CONV_ROWS
    sh = jnp.dot(shift, win, preferred_element_type=F32)
    acc = bias + wts(CONV_TAPS // 2) * win[HALO:HALO + r].astype(F32)
    i = 0
    for j in range(CONV_TAPS):
        if j != CONV_TAPS // 2:
            acc = acc + wts(j) * sh[i * r:(i + 1) * r]
            i += 1
    return acc


def _ada_kernel(c_ref, w_ref, b_ref, o_ref):
    c = c_ref[...]
    ca = c * _sigmoid(c)
    o_ref[0] = jnp.dot(ca, w_ref[0], preferred_element_type=F32,
                       precision=lax.Precision.HIGHEST) + b_ref[0]


def _ada_mod(c, w_ada, b_ada):
    depth, d, d6 = w_ada.shape
    bsz = c.shape[0]
    cp = jnp.zeros((SUB, d), F32).at[:bsz].set(c)
    out = pl.pallas_call(
        _ada_kernel,
        grid=(depth, d6 // d),
        in_specs=[pl.BlockSpec((SUB, d), lambda l, j: (0, 0)),
                  pl.BlockSpec((1, d, d), lambda l, j: (l, 0, j)),
                  pl.BlockSpec((1, 1, d), lambda l, j: (l, 0, j))],
        out_specs=pl.BlockSpec((1, SUB, d), lambda l, j: (l, 0, j)),
        out_shape=jax.ShapeDtypeStruct((depth, SUB, d6), F32),
        compiler_params=_cparams(("parallel", "parallel")),
    )(cp, w_ada, b_ada.reshape(depth, 1, d6))
    return out[:, :bsz]


def _lb_kernel(hg_ref, o_ref):
    depth = hg_ref.shape[0]
    xs = [hg_ref[l] for l in range(depth)]
    m = xs[0]
    for x in xs[1:]:
        m = jnp.maximum(m, x)
    es = [jnp.exp(x - m) for x in xs]
    tot = es[0]
    for e in es[1:]:
        tot = tot + e
    cs = None
    first = None
    for l in range(depth):
        sm = es[l] / tot
        cs = sm if cs is None else cs + sm
        if l == 0:
            first = cs
        lb = cs - first
        o_ref[0, l] = lb
        o_ref[1, l] = jnp.log(lb)
        o_ref[2, l] = jnp.log1p(-lb)


def _hg_lower_bounds(hg_lb):
    two, depth, k = hg_lb.shape
    return pl.pallas_call(
        _lb_kernel,
        out_shape=jax.ShapeDtypeStruct((3, depth, two, k), F32),
    )(hg_lb.transpose(1, 0, 2))


def _inproj_kernel(x_ref, mod_ref, g_ref, w_ref, wdt_ref, uz_ref, u_ref, udt_ref, h_scr, *, tn, gate_tiles):
    j = pl.program_id(1)
    nb = tn // LANE

    @pl.when(j == 0)
    def _():
        x = x_ref[...]
        ms = jnp.mean(x * x, axis=-1, keepdims=True)
        y = x * lax.rsqrt(ms + EPS) * g_ref[...]
        mod = mod_ref[0]
        shift = mod[:, 0:D_MODEL]
        scale = mod[:, D_MODEL:2 * D_MODEL]
        hb = (y * (1.0 + scale) + shift).astype(BF16)
        h_scr[...] = hb
        udt_ref[...] = jnp.dot(hb, wdt_ref[...], preferred_element_type=F32)

    is_gate = j < gate_tiles
    hb = h_scr[...]
    per = MXU_N // LANE
    accs = []

    def epilogue(c):
        val = jnp.where(is_gate, _sigmoid(accs[c]), accs[c]).astype(BF16)
        for k in range(per):
            u_ref[c * per + k] = val[:, k * LANE:(k + 1) * LANE]

    for c in range(tn // MXU_N):
        accs.append(jnp.dot(hb, w_ref[:, c * MXU_N:(c + 1) * MXU_N], preferred_element_type=F32))
        if c > 0:
            epilogue(c - 1)
    epilogue(tn // MXU_N - 1)

    @pl.when(j == gate_tiles)
    def _():
        for c in range(N_ZBLK // per):
            for k in range(per):
                uz_ref[c * per + k] = accs[c][:, k * LANE:(k + 1) * LANE]


def _inproj(x2, mod_l, g, w_main, w_dt, seq, tm=1024, tn=2048):
    t = x2.shape[0]
    tm = min(tm, seq)
    nb = tn // LANE
    assert N_GATE % nb == 0 and BLK_ZCOPY == N_GATE and N_ZBLK <= nb
    kern = functools.partial(_inproj_kernel, tn=tn, gate_tiles=N_GATE // nb)
    return pl.pallas_call(
        kern,
        grid=(t // tm, N_MAIN // tn),
        in_specs=[pl.BlockSpec((tm, D_MODEL), lambda i, j: (i, 0)),
                  pl.BlockSpec((1, 1, 6 * D_MODEL), lambda i, j: ((i * tm) // seq, 0, 0)),
                  pl.BlockSpec((1, D_MODEL), lambda i, j: (0, 0)),
                  pl.BlockSpec((D_MODEL, tn), lambda i, j: (0, j)),
                  pl.BlockSpec((D_MODEL, LANE), lambda i, j: (0, 0))],
        out_specs=[pl.BlockSpec((N_ZBLK, tm, LANE), lambda i, j: (0, i, 0)),
                   pl.BlockSpec((nb, tm, LANE), lambda i, j: (j, i, 0)),
                   pl.BlockSpec((tm, LANE), lambda i, j: (i, 0))],
        out_shape=[jax.ShapeDtypeStruct((N_ZBLK, t, LANE), F32),
                   jax.ShapeDtypeStruct((N_BLK, t, LANE), BF16),
                   jax.ShapeDtypeStruct((t, LANE), F32)],
        scratch_shapes=[pltpu.VMEM((tm, D_MODEL), BF16)],
        compiler_params=_cparams(("parallel", "arbitrary")),
    )(x2, mod_l, g, w_main, w_dt)


def _hgrn_consts():
    c = HG_CHUNK
    nlev = len(HG_LEVELS)
    t = np.arange(c)[:, None]
    r = np.arange(c)[None, :]
    sums = np.zeros((2, 1 + nlev, c, c), np.float32)
    masks = np.zeros((2, nlev + 1, c, c), np.float32)
    sums[0, 0] = r <= t
    sums[1, 0] = r >= t
    for li, m in enumerate(HG_LEVELS):
        grp = t // (2 * m)
        ref = grp * 2 * m + m - 1
        sums[0, 1 + li] = ((t > ref) & (r > ref) & (r <= t)) | ((t <= ref) & (r > t) & (r <= ref))
        ref = grp * 2 * m + m
        sums[1, 1 + li] = ((t < ref) & (r >= t) & (r < ref)) | ((t >= ref) & (r >= ref) & (r < t))
        own = ((t // (2 * m)) == (r // (2 * m))) & ((t % (2 * m)) >= m) & ((r % (2 * m)) < m)
        masks[0, li] = own
        masks[1, li] = own.T
    masks[:, nlev] = np.eye(c)
    return sums.reshape(2, (1 + nlev) * c, c), masks


def _hgrn_chunks(streams, sums_ref, msk_ref):
    c = HG_CHUNK
    nlev = len(HG_LEVELS)
    nt = (((1,), (1,)), ((), ()))

    keys, pieces = [], []
    for (q, z, v, lb, llb, l1m, st, rev) in streams:
        logsig = jnp.minimum(z, 0.0) - jnp.log(1.0 + jnp.exp(-jnp.abs(z)))
        keys.append((1.0 - lb) * jnp.exp(logsig - z))
        y2 = l1m + logsig
        lf = jnp.maximum(llb, y2) + jnp.log(1.0 + jnp.exp(-jnp.abs(llb - y2)))
        pieces.append(_split_bf16(lf))

    ex_alls = []
    for (q, z, v, lb, llb, l1m, st, rev), p in zip(streams, pieces):
        ex_alls.append(_join_f32(jnp.dot(sums_ref[1 if rev else 0], p, preferred_element_type=F32)))

    qfs = [s[0].astype(F32) for s in streams]
    accs = []
    for (q, z, v, lb, llb, l1m, st, rev), key in zip(streams, keys):
        d = 1 if rev else 0
        accs.append(lax.dot_general(q, key.astype(BF16), nt, preferred_element_type=F32) * msk_ref[d, nlev])
    for li in range(nlev):
        for si, ((q, z, v, lb, llb, l1m, st, rev), key) in enumerate(zip(streams, keys)):
            d = 1 if rev else 0
            ex = jnp.exp(ex_alls[si][(1 + li) * c:(2 + li) * c])
            sl = lax.dot_general((qfs[si] * ex).astype(BF16), (key * ex).astype(BF16), nt,
                                 preferred_element_type=F32)
            accs[si] = accs[si] + sl * msk_ref[d, li]

    outs = []
    for si, ((q, z, v, lb, llb, l1m, st, rev), key) in enumerate(zip(streams, keys)):
        b = ex_alls[si][0:c]
        o = jnp.dot(accs[si].astype(BF16), v, preferred_element_type=F32)
        qs = (qfs[si] * jnp.exp(b)).astype(BF16)
        o = o + lax.dot_general(qs, st.astype(BF16), nt, preferred_element_type=F32)
        bedge = b[0:1, :] if rev else b[c - 1:c, :]
        ks_ = (key * jnp.exp(bedge - b)).astype(BF16)
        vt = v.astype(F32).T.astype(BF16)
        st_new = st * jnp.exp(bedge) + jnp.dot(vt, ks_, preferred_element_type=F32)
        outs.append((o, st_new))
    return outs


def _hgrn_kernel(q_ref, zf_ref, zb_ref, v_ref, g_ref, lbp_ref, norm_ref, sums_ref, msk_ref, y_ref,
                 o_scr, st_scr, *, seq):
    c = HG_CHUNK
    n = seq // c
    half = n // 2

    def lanes(h):
        return slice(h * LANE, (h + 1) * LANE)

    def step(i, finalize):
        rf = pl.multiple_of(i * c, c)
        rb = pl.multiple_of((n - 1 - i) * c, c)
        streams = []
        for h in range(HG_HPS):
            for rev, r0, z_ref in ((False, rf, zf_ref), (True, rb, zb_ref)):
                d = 1 if rev else 0
                streams.append((q_ref[h, pl.ds(r0, c), :], z_ref[h, pl.ds(r0, c), :], v_ref[h, pl.ds(r0, c), :],
                                lbp_ref[0, d:d + 1, lanes(h)], lbp_ref[1, d:d + 1, lanes(h)],
                                lbp_ref[2, d:d + 1, lanes(h)], st_scr[2 * h + d], rev))
        outs = _hgrn_chunks(streams, sums_ref, msk_ref)
        for h in range(HG_HPS):
            for d, r0 in ((0, rf), (1, rb)):
                o, st = outs[2 * h + d]
                st_scr[2 * h + d] = st
                if not finalize:
                    o_scr[h, pl.ds(r0, c), :] = o
                else:
                    o = o + o_scr[h, pl.ds(r0, c), :]
                    o = o * lax.rsqrt(jnp.mean(o * o, axis=-1, keepdims=True) + EPS) * norm_ref[:, lanes(h)]
                    g = g_ref[h, pl.ds(r0, c), :].astype(F32)
                    y_ref[h, pl.ds(r0, c), :] = (o * (g * _sigmoid(g))).astype(BF16)

    def first_half(i, carry):
        step(i, False)
        return carry

    def second_half(i, carry):
        step(i, True)
        return carry

    st_scr[...] = jnp.zeros_like(st_scr)
    lax.fori_loop(0, half, first_half, 0)
    lax.fori_loop(half, n, second_half, 0)


def _hgrn(u3, uz, lbp_l, hg_norm_l, sums, masks, bsz, seq):
    hps = HG_HPS
    nlev = len(HG_LEVELS)
    assert (seq // HG_CHUNK) % 2 == 0

    def ublk(base):
        return pl.BlockSpec((hps, seq, LANE), lambda b, hp, base=base: (base // hps + hp, b, 0))

    def zblk(base):
        return pl.BlockSpec((hps, seq, LANE), lambda b, hp, base=base: (base // hps + hp, b, 0))

    kern = functools.partial(_hgrn_kernel, seq=seq)
    return pl.pallas_call(
        kern,
        grid=(bsz, HG_HEADS // hps),
        in_specs=[ublk(BLK_HG_Q), zblk(0), zblk(HG_HEADS), ublk(BLK_HG_I), ublk(BLK_HG_G),
                  pl.BlockSpec((3, 2, hps * LANE), lambda b, hp: (0, 0, hp)),
                  pl.BlockSpec((1, hps * LANE), lambda b, hp: (0, hp)),
                  pl.BlockSpec((2, (1 + nlev) * HG_CHUNK, HG_CHUNK), lambda b, hp: (0, 0, 0)),
                  pl.BlockSpec((2, nlev + 1, HG_CHUNK, HG_CHUNK), lambda b, hp: (0, 0, 0, 0))],
        out_specs=pl.BlockSpec((hps, seq, LANE), lambda b, hp: (hp, b, 0)),
        out_shape=jax.ShapeDtypeStruct((HG_HEADS, bsz * seq, LANE), BF16),
        scratch_shapes=[pltpu.VMEM((hps, seq, LANE), F32), pltpu.VMEM((2 * hps, LANE, LANE), F32)],
        compiler_params=_cparams(("parallel", "parallel")),
    )(u3, uz, uz, u3, u3, lbp_l, hg_norm_l, sums, masks)


NA_NEG = -1e30


def _na_bias_table(rpb_l):
    qc = np.arange(GRID_W)[:, None]
    kc = np.arange(GRID_W)[None, :]
    cstart = np.clip(qc - NA_COLS // 2, 0, GRID_W - NA_COLS)
    col_ok = (kc >= cstart) & (kc < cstart + NA_COLS)
    dc = np.clip(kc - qc, 1 - NA_COLS, NA_COLS - 1) + NA_COLS - 1
    onehot = (dc[None] == np.arange(2 * NA_COLS - 1)[:, None, None]).astype(np.float32)
    cols = jnp.einsum('hrc,cqk->hqrk', rpb_l.astype(F32), jnp.asarray(onehot),
                      precision=lax.Precision.HIGHEST)
    cols = jnp.where(col_ok[None, :, None, :], cols, NA_NEG)
    tab = jnp.stack([cols[:, :, NA_ROWS - 1 - s:2 * NA_ROWS - 1 - s] for s in range(NA_ROWS)], axis=1)
    return tab.reshape(NA_HEADS, NA_ROWS, GRID_W, NA_ROWS * GRID_W)


def _na_slot(variant, i):
    half = NA_ROWS // 2
    if variant == 0:
        return max(i - half, 0), min(i, half)
    if variant == 1:
        return i, half
    return (half + i, half) if i < half else (NA_G, i)


def _na_kernel(q_ref, k_ref, v_ref, tab_ref, o_ref, bias_scr, *, seq):
    rows = seq // GRID_W
    ngrp = rows // NA_G
    gq = NA_G * GRID_W
    kwin = 2 * NA_G * GRID_W
    kw = NA_ROWS * GRID_W
    nt = (((1,), (1,)), ((), ()))

    @pl.when(pl.program_id(1) == 0)
    def _():
        neg = jnp.full((GRID_W, LANE), NA_NEG, F32)
        for variant in range(3):
            for i in range(NA_G):
                a0, sft = _na_slot(variant, i)
                base = (a0 // 2) * LANE
                wid = min(kw + LANE, kwin - base)
                for h in range(2):
                    slab = jnp.concatenate([tab_ref[h, sft], neg], axis=1)
                    if a0 % 2:
                        slab = pltpu.roll(slab, GRID_W, 1)
                    pieces = [neg] * (base // LANE) + [slab[:, :wid]] + [neg] * ((kwin - base - wid) // LANE)
                    bias_scr[variant, h, i * GRID_W:(i + 1) * GRID_W, :] = jnp.concatenate(pieces, axis=1)

    head0 = lax.broadcasted_iota(jnp.int32, (gq, LANE), 1) < NA_DH
    zero = jnp.zeros((gq, LANE), BF16)

    def body(jj, carry):
        streams = []
        for g in range(NA_GPS):
            j = jj * NA_GPS + g
            variant = jnp.where(j == 0, 0, jnp.where(j == ngrp - 1, 2, 1))
            k0r = jnp.clip(j * NA_G - NA_ROWS // 2, 0, rows - 2 * NA_G)
            q0 = pl.multiple_of(j * gq, gq)
            k0 = pl.multiple_of(k0r * GRID_W, GRID_W)
            q = q_ref[0, pl.ds(q0, gq), :] * (NA_DH ** -0.5)
            kb = k_ref[0, pl.ds(k0, kwin), :]
            vb = v_ref[0, pl.ds(k0, kwin), :]
            streams.append((variant, 0, jnp.where(head0, q, zero), kb, vb, q0))
            streams.append((variant, 1, jnp.where(head0, zero, q), kb, vb, q0))
        ss = [lax.dot_general(qh, kb, nt, preferred_element_type=F32) for (_, _, qh, kb, _, _) in streams]
        ps, ls = [], []
        for (variant, h, _, _, _, _), s in zip(streams, ss):
            s = s + bias_scr[variant, h]
            p = jnp.exp(s - jnp.max(s, axis=-1, keepdims=True))
            ls.append(jnp.sum(p, axis=-1, keepdims=True))
            ps.append(p.astype(BF16))
        outs = [jnp.dot(p, vb, preferred_element_type=F32) for p, (_, _, _, _, vb, _) in zip(ps, streams)]
        for g in range(NA_GPS):
            q0 = streams[2 * g][5]
            o_ref[0, pl.ds(q0, gq), :] = jnp.where(head0, outs[2 * g] / ls[2 * g],
                                                   outs[2 * g + 1] / ls[2 * g + 1]).astype(BF16)
        return carry

    lax.fori_loop(0, ngrp // NA_GPS, body, 0)


def _na(u3, tab, bsz, seq):
    def ublk(base):
        return pl.BlockSpec((1, seq, LANE), lambda hp, b, base=base: (base + hp, b, 0))

    kern = functools.partial(_na_kernel, seq=seq)
    kw = NA_ROWS * GRID_W
    return pl.pallas_call(
        kern,
        grid=(NA_HEADS // 2, bsz),
        in_specs=[ublk(BLK_NA_Q), ublk(BLK_NA_K), ublk(BLK_NA_V),
                  pl.BlockSpec((2, NA_ROWS, GRID_W, kw), lambda hp, b: (hp, 0, 0, 0))],
        out_specs=pl.BlockSpec((1, seq, LANE), lambda hp, b: (hp, b, 0)),
        out_shape=jax.ShapeDtypeStruct((NA_HEADS // 2, bsz * seq, LANE), BF16),
        scratch_shapes=[pltpu.VMEM((3, 2, NA_G * GRID_W, 2 * NA_G * GRID_W), F32)],
        compiler_params=_cparams(("parallel", "arbitrary")),
    )(u3, u3, u3, tab)


def _gelu_tanh(x):
    return 0.5 * x * (1.0 + jnp.tanh(0.7978845608028654 * (x + 0.044715 * (x * x * x))))


def _lru_kernel(x_ref, gate_ref, cw_ref, cb_ref, gw_ref, gb_ref, lam_ref, shift_ref, y_ref,
                xpad, xf_scr, a_scr, u_scr, hf_scr, *, seq):
    w = LRU_CW
    tc = LRU_TC
    nch = seq // tc
    ngrp = seq // SUB
    nb = w // LANE
    r = CONV_ROWS

    zpad = jnp.zeros((HALO, w), BF16)
    xpad[0:HALO, :] = zpad
    xpad[seq + HALO:seq + 2 * HALO, :] = zpad
    for k in range(nb):
        xpad[HALO:seq + HALO, k * LANE:(k + 1) * LANE] = x_ref[k]

    sub = lax.broadcasted_iota(jnp.int32, (tc // SUB, SUB, w), 1)

    for d in range(2):
        rev = d == 1
        sp = _softplus(-lam_ref[0, d])

        def pass_a(i, carry, d=d, rev=rev, sp=sp):
            r0 = pl.multiple_of(i * tc, tc)
            if not rev:
                win = xpad[pl.ds(r0, tc + 2 * HALO), :]
                parts = [_conv_taps(win[s * r:(s + 1) * r + 2 * HALO], shift_ref[...],
                                    lambda j: cw_ref[j:j + 1, :], cb_ref[...]) for s in range(tc // r)]
                xf = jnp.concatenate(parts, axis=0)
                xf_scr[pl.ds(r0, tc), :] = xf
            else:
                xf = xf_scr[pl.ds(r0, tc), :]
            gts = jnp.dot(xf.astype(BF16), gw_ref[0, d], preferred_element_type=F32) + gb_ref[0, d]
            rg = _sigmoid(gts[:, 0:w])
            ig = _sigmoid(gts[:, w:2 * w])
            log_a = (-LRU_C) * rg * sp
            a = jnp.exp(log_a)
            y = -jnp.tanh(log_a) * (a * a + 1.0)
            u = y * lax.rsqrt(jnp.maximum(y, LRU_TINY)) * (ig * xf)
            a = a.reshape(tc // SUB, SUB, w)
            u = u.reshape(tc // SUB, SUB, w)
            for dd in (1, 2, 4):
                sh = (SUB - dd) if rev else dd
                ok = (sub < SUB - dd) if rev else (sub >= dd)
                a_s = pltpu.roll(a, sh, 1)
                u_s = pltpu.roll(u, sh, 1)
                u = jnp.where(ok, a * u_s + u, u)
                a = jnp.where(ok, a * a_s, a)
            a_scr[pl.ds(r0, tc), :] = a.reshape(tc, w)
            u_scr[pl.ds(r0, tc), :] = u.reshape(tc, w)
            return carry

        lax.fori_loop(0, nch, pass_a, 0)

        def pass_b(g, h, rev=rev):
            gi = (ngrp - 1 - g) if rev else g
            r0 = pl.multiple_of(gi * SUB, SUB)
            hp = h[0:1, :] if rev else h[SUB - 1:SUB, :]
            hn = u_scr[pl.ds(r0, SUB), :] + a_scr[pl.ds(r0, SUB), :] * hp
            if rev:
                u_scr[pl.ds(r0, SUB), :] = hn
            else:
                hf_scr[pl.ds(r0, SUB), :] = hn
            return hn

        lax.fori_loop(0, ngrp, pass_b, jnp.zeros((SUB, w), F32), unroll=8)

    def pass_c(i, carry):
        r0 = pl.multiple_of(i * tc, tc)
        hsum = hf_scr[pl.ds(r0, tc), :] + u_scr[pl.ds(r0, tc), :]
        for k in range(nb):
            g = gate_ref[k, pl.ds(r0, tc), :].astype(F32)
            y_ref[k, pl.ds(r0, tc), :] = (hsum[:, k * LANE:(k + 1) * LANE] * _gelu_tanh(g)).astype(BF16)
        return carry

    lax.fori_loop(0, nch, pass_c, 0)


def _lru_weights(gate_w_l, gate_b_l, lam_l):
    ncb = LRU_WIDTH // LRU_CW
    per = LRU_CW // LRU_BW
    gw = gate_w_l.astype(F32)
    eye = jnp.eye(per, dtype=F32)
    blocks = gw.reshape(2, 2, ncb, per, LRU_BW, LRU_BW)
    dense = jnp.einsum('dgcpio,pq->dgcpiqo', blocks, eye).reshape(2, 2, ncb, LRU_CW, LRU_CW)
    dense = dense.transpose(2, 0, 3, 1, 4).reshape(ncb, 2, LRU_CW, 2 * LRU_CW).astype(BF16)
    gb = gate_b_l.astype(F32).reshape(2, 2, ncb, LRU_CW).transpose(2, 0, 1, 3).reshape(ncb, 2, 1, 2 * LRU_CW)
    lam = lam_l.astype(F32).reshape(2, ncb, LRU_CW).transpose(1, 0, 2).reshape(ncb, 2, 1, LRU_CW)
    return dense, gb, lam


def _lru(u3, conv_w_l, conv_b_l, gwd, gbd, lamd, shift, bsz, seq):
    ncb = LRU_WIDTH // LRU_CW
    nb = LRU_CW // LANE
    kern = functools.partial(_lru_kernel, seq=seq)

    def ublk(base):
        return pl.BlockSpec((nb, seq, LANE), lambda b, cb, base=base: (base // nb + cb, b, 0))

    return pl.pallas_call(
        kern,
        grid=(bsz, ncb),
        in_specs=[ublk(BLK_LRU_X), ublk(BLK_LRU_G),
                  pl.BlockSpec((CONV_TAPS, LRU_CW), lambda b, cb: (0, cb)),
                  pl.BlockSpec((1, LRU_CW), lambda b, cb: (0, cb)),
                  pl.BlockSpec((1, 2, LRU_CW, 2 * LRU_CW), lambda b, cb: (cb, 0, 0, 0)),
                  pl.BlockSpec((1, 2, 1, 2 * LRU_CW), lambda b, cb: (cb, 0, 0, 0)),
                  pl.BlockSpec((1, 2, 1, LRU_CW), lambda b, cb: (cb, 0, 0, 0)),
                  pl.BlockSpec(shift.shape, lambda b, cb: (0, 0))],
        out_specs=pl.BlockSpec((nb, seq, LANE), lambda b, cb: (cb, b, 0)),
        out_shape=jax.ShapeDtypeStruct((LRU_WIDTH // LANE, bsz * seq, LANE), BF16),
        scratch_shapes=[pltpu.VMEM((seq + 2 * HALO, LRU_CW), BF16),
                        pltpu.VMEM((seq, LRU_CW), F32),
                        pltpu.VMEM((seq, LRU_CW), F32),
                        pltpu.VMEM((seq, LRU_CW), F32),
                        pltpu.VMEM((seq, LRU_CW), F32)],
        compiler_params=_cparams(("parallel", "parallel")),
    )(u3, u3, conv_w_l, conv_b_l.reshape(1, LRU_WIDTH), gwd, gbd, lamd, shift)


def _ssd_decay(dt_raw, dtb_ref, alog_ref, tri_ref):
    dt = _softplus(dt_raw + dtb_ref[...])
    a = dt * (-jnp.exp(alog_ref[...]))
    lane = lax.broadcasted_iota(jnp.int32, (SSD_Q, LANE), 1)
    ap = _split_bf16(a)
    pre = _join_f32(jnp.dot(tri_ref[0], ap, preferred_element_type=F32))
    suf = _join_f32(jnp.dot(tri_ref[1], ap, preferred_element_type=F32))
    return dt, jnp.where(lane < SSD_HEADS, pre, suf)


def _ssd_conv(cur_ref, prev_ref, next_ref, cw_ref, cb_ref, shift_ref, n, nlast, sc):
    q = SSD_Q
    nblk = cur_ref.shape[0]
    r0 = sc * q
    zero = jnp.zeros((HALO, LANE), BF16)
    wins = []
    for k in range(nblk):
        if sc == 0:
            pv = jnp.where(n > 0, prev_ref[k], zero)
        else:
            pv = cur_ref[k, r0 - HALO:r0, :]
        if sc == SSD_SUB - 1:
            nx = jnp.where(n < nlast, next_ref[k], zero)
        else:
            nx = cur_ref[k, r0 + q:r0 + q + HALO, :]
        wins.append(jnp.concatenate([pv, cur_ref[k, r0:r0 + q, :], nx], axis=0))
    win = jnp.concatenate(wins, axis=1)
    r = CONV_ROWS
    sh = jnp.dot(shift_ref[...], win, preferred_element_type=F32)
    xs = []
    for k in range(nblk):
        cols = slice(k * LANE, (k + 1) * LANE)
        acc = cb_ref[k] + cw_ref[CONV_TAPS // 2, k] * cur_ref[k, r0:r0 + q, :].astype(F32)
        i = 0
        for j in range(CONV_TAPS):
            if j != CONV_TAPS // 2:
                acc = acc + cw_ref[j, k] * sh[i * r:(i + 1) * r, cols]
                i += 1
        xs.append(acc * _sigmoid(acc))
    return xs


def _ssd_expand_matrix():
    nx = SSD_HEADS // 2
    m = np.zeros((2, LANE, nx * LANE), np.float32)
    for d in range(2):
        for j in range(nx):
            for half in range(2):
                m[d, d * SSD_HEADS + 2 * j + half, j * LANE + half * SSD_P:j * LANE + (half + 1) * SSD_P] = 1.0
    return np.concatenate([m, m], axis=1)


def _ssd_state_step(xh, bm, cm, dt, acc_, ex, h_scr, rev, expand_ref):
    q = SSD_Q
    d = 1 if rev else 0
    off = d * SSD_HEADS
    er = 0 if rev else q - 1
    edge = acc_[er:er + 1, :]
    lane = lax.broadcasted_iota(jnp.int32, (q, LANE), 1)
    if expand_ref is not None:
        mine = jnp.logical_and(lane >= off, lane < off + SSD_HEADS)
        wt = jnp.where(mine, jnp.exp(edge - acc_) * dt, 0.0)
        pex = jnp.dot(_split_bf16(ex), expand_ref[d], preferred_element_type=F32)
        pwt = jnp.dot(_split_bf16(wt), expand_ref[d], preferred_element_type=F32)
        pattern = lambda p, j: p[:, j * LANE:(j + 1) * LANE]
    else:
        wt = jnp.exp(edge - acc_) * dt
        pex, pwt = ex, wt
        lane_lo = lane < SSD_P

        def pattern(p, j):
            c0 = off + 2 * j
            return jnp.where(lane_lo, p[:, c0:c0 + 1], p[:, c0 + 1:c0 + 2])

    y_off = []
    per_grp = SSD_HEADS // SSD_GROUPS // 2
    for j in range(SSD_HEADS // 2):
        g = j // per_grp
        hj = h_scr[j]
        pe = pattern(pex, j)
        y_off.append(pe * jnp.dot(cm[g], hj.astype(BF16), preferred_element_type=F32))
        xt = (xh[j] * pattern(pwt, j)).astype(BF16)
        h_scr[j] = hj * pe[er:er + 1, :] + lax.dot_general(
            bm[g], xt, (((0,), (0,)), ((), ())), preferred_element_type=F32)
    return y_off


def _ssd_fwd_kernel(cur_ref, prev_ref, next_ref, dt_ref, cw_ref, cb_ref, dtb_ref, alog_ref,
                    shift_ref, tri_ref, yp_ref, xc_ref, h_scr, *, nsteps):
    q = SSD_Q
    n = pl.program_id(1)

    @pl.when(n == 0)
    def _():
        h_scr[...] = jnp.zeros_like(h_scr)

    nx = SSD_HEADS // 2
    lane_lo = lax.broadcasted_iota(jnp.int32, (q, LANE), 1) < SSD_P
    ti = lax.broadcasted_iota(jnp.int32, (q, q), 0)
    si = lax.broadcasted_iota(jnp.int32, (q, q), 1)
    lower = si <= ti
    upper = si >= ti
    nt = (((1,), (1,)), ((), ()))
    heads_per_grp = SSD_HEADS // SSD_GROUPS

    for sc in range(SSD_SUB):
        rows = slice(sc * q, (sc + 1) * q)
        dt, acc_ = _ssd_decay(dt_ref[rows, :], dtb_ref, alog_ref, tri_ref)
        ex = jnp.exp(acc_)
        acc_t = acc_.T
        dt_t = dt.T
        xs = _ssd_conv(cur_ref, prev_ref, next_ref, cw_ref, cb_ref, shift_ref, n, nsteps - 1, sc)
        for k in range(len(xs)):
            xc_ref[k, rows, :] = xs[k].astype(BF16)
        xh = xs[0:nx]
        bm = [t.astype(BF16) for t in xs[nx:nx + SSD_GROUPS]]
        cm = [t.astype(BF16) for t in xs[nx + SSD_GROUPS:nx + 2 * SSD_GROUPS]]

        y = []
        for j in range(nx):
            g = (2 * j) // heads_per_grp
            cb = lax.dot_general(cm[g], bm[g], nt, preferred_element_type=F32)
            parts = []
            for h in (2 * j, 2 * j + 1):
                hb = SSD_HEADS + h
                lf = jnp.where(lower, jnp.exp(acc_[:, h:h + 1] - acc_t[h:h + 1, :]), 0.0)
                lb = jnp.where(upper, jnp.exp(acc_[:, hb:hb + 1] - acc_t[hb:hb + 1, :]), 0.0)
                mh = (cb * (lf * dt_t[h:h + 1, :] + lb * dt_t[hb:hb + 1, :])).astype(BF16)
                parts.append(jnp.dot(mh, xh[j].astype(BF16), preferred_element_type=F32))
            y.append(jnp.where(lane_lo, parts[0], parts[1]))

        y_off = _ssd_state_step(xh, bm, cm, dt, acc_, ex, h_scr, False, None)
        for j in range(nx):
            yp_ref[j, rows, :] = y[j] + y_off[j]


def _ssd_bwd_kernel(xc_ref, dt_ref, dtb_ref, alog_ref, tri_ref, expand_ref, yp_ref, z_ref, skip_ref,
                    norm_ref, y_ref, h_scr):
    q = SSD_Q
    n = pl.program_id(1)

    @pl.when(n == 0)
    def _():
        h_scr[...] = jnp.zeros_like(h_scr)

    nx = SSD_HEADS // 2
    for sc in reversed(range(SSD_SUB)):
        rows = slice(sc * q, (sc + 1) * q)
        dt, acc_ = _ssd_decay(dt_ref[rows, :], dtb_ref, alog_ref, tri_ref)
        ex = jnp.exp(acc_)
        xh = [xc_ref[k, rows, :].astype(F32) for k in range(nx)]
        bm = [xc_ref[nx + g, rows, :] for g in range(SSD_GROUPS)]
        cm = [xc_ref[nx + SSD_GROUPS + g, rows, :] for g in range(SSD_GROUPS)]
        y_off = _ssd_state_step(xh, bm, cm, dt, acc_, ex, h_scr, True, expand_ref)

        ys = []
        ssq = jnp.zeros((q, 1), F32)
        for j in range(nx):
            z = z_ref[j, rows, :].astype(F32)
            yj = (yp_ref[j, rows, :] + y_off[j] + skip_ref[j] * xh[j]) * (z * _sigmoid(z))
            ssq = ssq + jnp.sum(yj * yj, axis=-1, keepdims=True)
            ys.append(yj)
        r = lax.rsqrt(ssq * (1.0 / (nx * LANE)) + EPS)
        for j in range(nx):
            y_ref[j, rows, :] = (ys[j] * r * norm_ref[j]).astype(BF16)


def _ssd(u3, udt, conv_w_l, conv_b_l, dt_bias_l, a_log_l, skip_l, norm_l, shift, tri, expand, bsz, seq):
    q = SSD_SUB * SSD_Q
    nchunks = seq // q
    nxbc = N_XBC
    nx = SSD_HEADS // 2
    hb = q // HALO
    tb = (bsz * seq) // HALO

    cw = conv_w_l.astype(F32).reshape(CONV_TAPS, nxbc, 1, LANE)
    cb = conv_b_l.astype(F32).reshape(nxbc, 1, LANE)
    dtb = jnp.zeros((1, LANE), F32).at[0, :N_DT].set(dt_bias_l.astype(F32).reshape(-1))
    alog = jnp.zeros((1, LANE), F32).at[0, :N_DT].set(a_log_l.astype(F32).reshape(-1))
    skip = jnp.repeat(skip_l.astype(F32), SSD_P).reshape(nx, 1, LANE)
    norm = norm_l.astype(F32).reshape(nx, 1, LANE)

    def cur(b, n):
        return (BLK_SSD_X // nxbc, b * nchunks + n, 0)

    def prev(b, n):
        return (BLK_SSD_X // nxbc, jnp.maximum((b * nchunks + n) * hb - 1, 0), 0)

    def nxt(b, n):
        return (BLK_SSD_X // nxbc, jnp.minimum((b * nchunks + n + 1) * hb, tb - 1), 0)

    def const(arr):
        return pl.BlockSpec(arr.shape, lambda b, n: (0,) * arr.ndim)

    state = pltpu.VMEM((nx, SSD_N, LANE), F32)
    fwd_blk = lambda b, n: (0, b * nchunks + n, 0)
    ypart, xconv = pl.pallas_call(
        functools.partial(_ssd_fwd_kernel, nsteps=nchunks),
        grid=(bsz, nchunks),
        in_specs=[pl.BlockSpec((nxbc, q, LANE), cur),
                  pl.BlockSpec((nxbc, HALO, LANE), prev),
                  pl.BlockSpec((nxbc, HALO, LANE), nxt),
                  pl.BlockSpec((q, LANE), lambda b, n: (b * nchunks + n, 0)),
                  const(cw), const(cb), const(dtb), const(alog), const(shift), const(tri)],
        out_specs=[pl.BlockSpec((nx, q, LANE), fwd_blk), pl.BlockSpec((nxbc, q, LANE), fwd_blk)],
        out_shape=[jax.ShapeDtypeStruct((nx, bsz * seq, LANE), F32),
                   jax.ShapeDtypeStruct((nxbc, bsz * seq, LANE), BF16)],
        scratch_shapes=[state],
        compiler_params=_cparams(("parallel", "arbitrary")),
    )(u3, u3, u3, udt, cw, cb, dtb, alog, shift, tri)

    bwd_blk = lambda b, n: (0, b * nchunks + nchunks - 1 - n, 0)
    return pl.pallas_call(
        _ssd_bwd_kernel,
        grid=(bsz, nchunks),
        in_specs=[pl.BlockSpec((nxbc, q, LANE), bwd_blk),
                  pl.BlockSpec((q, LANE), lambda b, n: (b * nchunks + nchunks - 1 - n, 0)),
                  const(dtb), const(alog), const(tri), const(expand),
                  pl.BlockSpec((nx, q, LANE), bwd_blk),
                  pl.BlockSpec((nx, q, LANE), lambda b, n: (BLK_SSD_Z // nx, b * nchunks + nchunks - 1 - n, 0)),
                  const(skip), const(norm)],
        out_specs=pl.BlockSpec((nx, q, LANE), bwd_blk),
        out_shape=jax.ShapeDtypeStruct((nx, bsz * seq, LANE), BF16),
        scratch_shapes=[state],
        compiler_params=_cparams(("parallel", "arbitrary")),
    )(xconv, udt, dtb, alog, tri, expand, ypart, u3, skip, norm)


def _cat_blocks(ref, lo, n):
    return jnp.concatenate([ref[lo + k] for k in range(n)], axis=-1)


def _merge_kernel(yhg_ref, yna_ref, ylru_ref, yssd_ref, gt_ref, x_ref, mod_ref,
                  whg_ref, wna_ref, wlru_ref, wssd_ref, wout_ref, o_ref, m_scr):
    nd = D_MODEL // LANE
    per = MXU_N // LANE
    nchunk = D_MODEL // MXU_N
    branches = ((yhg_ref, whg_ref), (yna_ref, wna_ref), (ylru_ref, wlru_ref), (yssd_ref, wssd_ref))
    ys = [_cat_blocks(y_ref, 0, y_ref.shape[0]) for y_ref, _ in branches]

    prods = []

    def gate_sum(c):
        tot = None
        for i in range(len(branches)):
            term = _cat_blocks(gt_ref, nd * i + per * c, per).astype(F32) * prods[c][i]
            tot = term if tot is None else tot + term
        m_scr[:, c * MXU_N:(c + 1) * MXU_N] = tot.astype(BF16)

    for c in range(nchunk):
        cols = slice(c * MXU_N, (c + 1) * MXU_N)
        prods.append([jnp.dot(y, w_ref[:, cols], preferred_element_type=F32)
                      for y, (_, w_ref) in zip(ys, branches)])
        if c > 0:
            gate_sum(c - 1)
    gate_sum(nchunk - 1)

    mb = m_scr[...]
    gate = mod_ref[0][:, 2 * D_MODEL:3 * D_MODEL]
    outs = []

    def residual(c):
        cols = slice(c * MXU_N, (c + 1) * MXU_N)
        o_ref[:, cols] = x_ref[:, cols] + gate[:, cols] * outs[c]

    for c in range(nchunk):
        outs.append(jnp.dot(mb, wout_ref[:, c * MXU_N:(c + 1) * MXU_N], preferred_element_type=F32))
        if c > 0:
            residual(c - 1)
    residual(nchunk - 1)


def _merge(y_hg, y_na, y_lru, y_ssd, u3, x2, mod_l, w_hg, w_na, w_lru, w_ssd, w_out, seq, tm=512):
    t = x2.shape[0]
    tm = min(tm, seq)

    def yspec(arr):
        return pl.BlockSpec((arr.shape[0], tm, LANE), lambda i: (0, i, 0))

    def wspec(arr):
        return pl.BlockSpec(arr.shape, lambda i: (0, 0))

    return pl.pallas_call(
        _merge_kernel,
        grid=(t // tm,),
        in_specs=[yspec(y_hg), yspec(y_na), yspec(y_lru), yspec(y_ssd),
                  pl.BlockSpec((N_GATE, tm, LANE), lambda i: (BLK_MERGE // N_GATE, i, 0)),
                  pl.BlockSpec((tm, D_MODEL), lambda i: (i, 0)),
                  pl.BlockSpec((1, 1, 6 * D_MODEL), lambda i: ((i * tm) // seq, 0, 0)),
                  wspec(w_hg), wspec(w_na), wspec(w_lru), wspec(w_ssd), wspec(w_out)],
        out_specs=pl.BlockSpec((tm, D_MODEL), lambda i: (i, 0)),
        out_shape=jax.ShapeDtypeStruct((t, D_MODEL), F32),
        scratch_shapes=[pltpu.VMEM((tm, D_MODEL), BF16)],
        compiler_params=_cparams(("parallel",)),
    )(y_hg, y_na, y_lru, y_ssd, u3, x2, mod_l, w_hg, w_na, w_lru, w_ssd, w_out)


def _ffn_kernel(x_ref, mod_ref, g_ref, w1_ref, w2_ref, gfin_ref, o_ref, h_scr, e_scr, acc_scr, *, final_norm):
    j = pl.program_id(1)
    nj = pl.num_programs(1)

    @pl.when(j == 0)
    def _():
        acc_scr[...] = jnp.zeros_like(acc_scr)
        x = x_ref[...]
        ms = jnp.mean(x * x, axis=-1, keepdims=True)
        y = x * lax.rsqrt(ms + EPS) * g_ref[...]
        mod = mod_ref[0]
        shift = mod[:, 3 * D_MODEL:4 * D_MODEL]
        scale = mod[:, 4 * D_MODEL:5 * D_MODEL]
        h_scr[...] = (y * (1.0 + scale) + shift).astype(BF16)

    hb = h_scr[...]
    tf = w1_ref.shape[1]
    hids = []

    def square(c):
        hid = jnp.maximum(hids[c], 0.0)
        e_scr[:, c * MXU_N:(c + 1) * MXU_N] = (hid * hid).astype(BF16)

    for c in range(tf // MXU_N):
        hids.append(jnp.dot(hb, w1_ref[:, c * MXU_N:(c + 1) * MXU_N], preferred_element_type=F32))
        if c > 0:
            square(c - 1)
    square(tf // MXU_N - 1)

    eb = e_scr[...]
    parts = []

    def accumulate(c):
        cols = slice(c * MXU_N, (c + 1) * MXU_N)
        acc_scr[:, cols] = acc_scr[:, cols] + parts[c]

    for c in range(D_MODEL // MXU_N):
        parts.append(jnp.dot(eb, w2_ref[:, c * MXU_N:(c + 1) * MXU_N], preferred_element_type=F32))
        if c > 0:
            accumulate(c - 1)
    accumulate(D_MODEL // MXU_N - 1)

    @pl.when(j == nj - 1)
    def _():
        gate = mod_ref[0][:, 5 * D_MODEL:6 * D_MODEL]
        xo = x_ref[...] + gate * acc_scr[...]
        if final_norm:
            ms = jnp.mean(xo * xo, axis=-1, keepdims=True)
            xo = xo * lax.rsqrt(ms + EPS) * gfin_ref[...]
        o_ref[...] = xo


def _ffn(x2, mod_l, g, w1, w2, g_final, seq, final_norm, tm=1024, tf=1024):
    t = x2.shape[0]
    tm = min(tm, seq)
    return pl.pallas_call(
        functools.partial(_ffn_kernel, final_norm=final_norm),
        grid=(t // tm, D_FF // tf),
        in_specs=[pl.BlockSpec((tm, D_MODEL), lambda i, j: (i, 0)),
                  pl.BlockSpec((1, 1, 6 * D_MODEL), lambda i, j: ((i * tm) // seq, 0, 0)),
                  pl.BlockSpec((1, D_MODEL), lambda i, j: (0, 0)),
                  pl.BlockSpec((D_MODEL, tf), lambda i, j: (0, j)),
                  pl.BlockSpec((tf, D_MODEL), lambda i, j: (j, 0)),
                  pl.BlockSpec((1, D_MODEL), lambda i, j: (0, 0))],
        out_specs=pl.BlockSpec((tm, D_MODEL), lambda i, j: (i, 0)),
        out_shape=jax.ShapeDtypeStruct((t, D_MODEL), F32),
        scratch_shapes=[pltpu.VMEM((tm, D_MODEL), BF16), pltpu.VMEM((tm, tf), BF16),
                        pltpu.VMEM((tm, D_MODEL), F32)],
        compiler_params=_cparams(("parallel", "arbitrary")),
    )(x2, mod_l, g, w1, w2, g_final)


W_IN_ORDER = ((8224, 12320),
              (512, 1536),
              (0, 512),
              (1536, 2560),
              (2560, 4096),
              (6144, 8192),
              (4096, 5120),
              (5120, 6144))
W_IN_DT = (8192, 8224)


def _split_w_in(w_in_l):
    main = jnp.concatenate([w_in_l[:, a:b] for a, b in W_IN_ORDER], axis=1).astype(BF16)
    dt = jnp.zeros((D_MODEL, LANE), BF16).at[:, :N_DT].set(w_in_l[:, W_IN_DT[0]:W_IN_DT[1]].astype(BF16))
    return main, dt


def kernel(x, c, w_ada, b_ada, g_mix, g_ffn, w_in, hg_lb, hg_norm, na_rpb, lru_conv_w, lru_conv_b, lru_gate_w, lru_gate_b, lru_lambda, ssd_conv_w, ssd_conv_b, ssd_dt_bias, ssd_a_log, ssd_skip, ssd_norm, w_br_hg, w_br_na, w_br_lru, w_br_ssd, w_out, w_ff1, w_ff2, g_final):
    bsz, seq, d = x.shape
    depth = w_in.shape[0]
    rows = seq // GRID_W
    assert d == D_MODEL and seq % LRU_TC == 0 and rows % (NA_G * NA_GPS) == 0 and rows >= 2 * NA_G

    mod = _ada_mod(c, w_ada, b_ada).reshape(depth, bsz, 1, 6 * D_MODEL)
    lbp = _hg_lower_bounds(hg_lb)
    hg_sums_np, hg_masks_np = _hgrn_consts()
    hg_sums = jnp.asarray(hg_sums_np, BF16)
    hg_masks = jnp.asarray(hg_masks_np)
    shift = jnp.asarray(_conv_shift_matrix(), BF16)
    tri = hg_sums[:, :HG_CHUNK, :]
    expand = jnp.asarray(_ssd_expand_matrix(), BF16)
    gfin = g_final.reshape(1, D_MODEL)

    x2 = x.reshape(bsz * seq, D_MODEL)
    for l in range(depth):
        w_main, w_dt = _split_w_in(w_in[l])
        uz, u3, udt = _inproj(x2, mod[l], g_mix[l].reshape(1, D_MODEL), w_main, w_dt, seq)
        y_hg = _hgrn(u3, uz, lbp[:, l], hg_norm[l].reshape(1, -1), hg_sums, hg_masks, bsz, seq)
        y_na = _na(u3, _na_bias_table(na_rpb[l]), bsz, seq)
        gwd, gbd, lamd = _lru_weights(lru_gate_w[l], lru_gate_b[l], lru_lambda[l])
        y_lru = _lru(u3, lru_conv_w[l], lru_conv_b[l], gwd, gbd, lamd, shift, bsz, seq)
        y_ssd = _ssd(u3, udt, ssd_conv_w[l], ssd_conv_b[l], ssd_dt_bias[l], ssd_a_log[l],
                     ssd_skip[l], ssd_norm[l], shift, tri, expand, bsz, seq)
        x2 = _merge(y_hg, y_na, y_lru, y_ssd, u3, x2, mod[l],
                    w_br_hg[l].astype(BF16), w_br_na[l].astype(BF16), w_br_lru[l].astype(BF16),
                    w_br_ssd[l].astype(BF16), w_out[l].astype(BF16), seq)
        x2 = _ffn(x2, mod[l], g_ffn[l].reshape(1, D_MODEL), w_ff1[l].astype(BF16), w_ff2[l].astype(BF16),
                  gfin, seq, final_norm=(l == depth - 1))
    return x2.reshape(bsz, seq, D_MODEL)
```

```python
import functools

import numpy as np
import jax
import jax.numpy as jnp
from jax import lax
from jax.experimental import pallas as pl
from jax.experimental.pallas import tpu as pltpu

F32 = jnp.float32
BF16 = jnp.bfloat16

LANE = 128
SUB = 8
MXU_N = 256
D_MODEL = 1024
D_FF = 4 * D_MODEL
GRID_W = 64
EPS = 1e-6

HG_HEADS = 4
HG_HPS = 2
HG_CHUNK = 128
HG_LEVELS = (64, 32, 16, 8, 4, 2, 1)
HG_MXU_LEVELS = tuple(m for m in HG_LEVELS if 2 * m <= SUB)

NA_ROWS = 8
NA_COLS = 16
NA_HEADS = 8
NA_DH = 64
NA_G = 8
NA_GPS = 2

LRU_WIDTH = 512
LRU_BLOCKS = 8
LRU_BW = 64
LRU_C = 8.0
LRU_CW = 256
LRU_TC = 512
LRU_TINY = 1e-37

SSD_HEADS = 16
SSD_P = 64
SSD_GROUPS = 4
SSD_N = 128
SSD_Q = 128
SSD_SUB = 4
HALO = 16
CONV_ROWS = 128
CONV_TAPS = 4

BLK_MERGE = 0
BLK_ZCOPY = 32
BLK_HG_Q, BLK_HG_I, BLK_HG_G = 40, 44, 48
BLK_NA_Q, BLK_NA_K, BLK_NA_V = 52, 56, 60
BLK_SSD_X, BLK_SSD_B, BLK_SSD_C = 64, 72, 76
BLK_LRU_X, BLK_LRU_G = 80, 84
BLK_SSD_Z = 88
N_BLK = 96
N_XBC = 16
N_ZBLK = 8
N_GATE = 32
N_MAIN = N_BLK * LANE
N_DT = 2 * SSD_HEADS

VMEM_LIMIT = 48 * 1024 * 1024


def _cparams(sem):
    return pltpu.CompilerParams(dimension_semantics=sem, vmem_limit_bytes=VMEM_LIMIT)


def _sigmoid(x):
    return 0.5 * jnp.tanh(0.5 * x) + 0.5


def _softplus(x):
    return jnp.maximum(x, 0.0) + jnp.log1p(jnp.exp(-jnp.abs(x)))


def _split_bf16(x):
    hi = x.astype(BF16)
    lo = (x - hi.astype(F32)).astype(BF16)
    return jnp.concatenate([hi, lo], axis=1)


def _join_f32(r):
    half = r.shape[1] // 2
    return r[:, :half] + r[:, half:]


def _conv_shift_matrix():
    r = CONV_ROWS
    offs = [j - CONV_TAPS // 2 for j in range(CONV_TAPS) if j != CONV_TAPS // 2]
    m = np.zeros((len(offs), r, r + 2 * HALO), np.float32)
    for i, o in enumerate(offs):
        m[i, np.arange(r), HALO + np.arange(r) + o] = 1.0
    return m.reshape(len(offs) * r, r + 2 * HALO)


def _conv_taps(win, shift, wts, bias):
    r = CONV_ROWS
    sh = jnp.dot(shift, win, preferred_element_type=F32)
    acc = bias + wts(CONV_TAPS // 2) * win[HALO:HALO + r].astype(F32)
    i = 0
    for j in range(CONV_TAPS):
        if j != CONV_TAPS // 2:
            acc = acc + wts(j) * sh[i * r:(i + 1) * r]
            i += 1
    return acc


def _ada_kernel(c_ref, w_ref, b_ref, o_ref):
    c = c_ref[...]
    ca = c * _sigmoid(c)
    o_ref[0] = jnp.dot(ca, w_ref[0], preferred_element_type=F32,
                       precision=lax.Precision.HIGHEST) + b_ref[0]


def _ada_mod(c, w_ada, b_ada):
    depth, d, d6 = w_ada.shape
    bsz = c.shape[0]
    cp = jnp.zeros((SUB, d), F32).at[:bsz].set(c)
    out = pl.pallas_call(
        _ada_kernel,
        grid=(depth, d6 // d),
        in_specs=[pl.BlockSpec((SUB, d), lambda l, j: (0, 0)),
                  pl.BlockSpec((1, d, d), lambda l, j: (l, 0, j)),
                  pl.BlockSpec((1, 1, d), lambda l, j: (l, 0, j))],
        out_specs=pl.BlockSpec((1, SUB, d), lambda l, j: (l, 0, j)),
        out_shape=jax.ShapeDtypeStruct((depth, SUB, d6), F32),
        compiler_params=_cparams(("parallel", "parallel")),
    )(cp, w_ada, b_ada.reshape(depth, 1, d6))
    return out[:, :bsz]


def _lb_kernel(hg_ref, o_ref):
    depth = hg_ref.shape[0]
    xs = [hg_ref[l] for l in range(depth)]
    m = xs[0]
    for x in xs[1:]:
        m = jnp.maximum(m, x)
    es = [jnp.exp(x - m) for x in xs]
    tot = es[0]
    for e in es[1:]:
        tot = tot + e
    cs = None
    first = None
    for l in range(depth):
        sm = es[l] / tot
        cs = sm if cs is None else cs + sm
        if l == 0:
            first = cs
        lb = cs - first
        o_ref[0, l] = lb
        o_ref[1, l] = jnp.log(lb)
        o_ref[2, l] = jnp.log1p(-lb)


def _hg_lower_bounds(hg_lb):
    two, depth, k = hg_lb.shape
    return pl.pallas_call(
        _lb_kernel,
        out_shape=jax.ShapeDtypeStruct((3, depth, two, k), F32),
    )(hg_lb.transpose(1, 0, 2))


def _inproj_kernel(x_ref, mod_ref, g_ref, w_ref, wdt_ref, uz_ref, u_ref, udt_ref, h_scr, *, tn, gate_tiles):
    j = pl.program_id(1)
    nb = tn // LANE

    @pl.when(j == 0)
    def _():
        x = x_ref[...]
        ms = jnp.mean(x * x, axis=-1, keepdims=True)
        y = x * lax.rsqrt(ms + EPS) * g_ref[...]
        mod = mod_ref[0]
        shift = mod[:, 0:D_MODEL]
        scale = mod[:, D_MODEL:2 * D_MODEL]
        hb = (y * (1.0 + scale) + shift).astype(BF16)
        h_scr[...] = hb
        udt_ref[...] = jnp.dot(hb, wdt_ref[...], preferred_element_type=F32)

    is_gate = j < gate_tiles
    hb = h_scr[...]
    per = MXU_N // LANE
    accs = []

    def epilogue(c):
        val = jnp.where(is_gate, _sigmoid(accs[c]), accs[c]).astype(BF16)
        for k in range(per):
            u_ref[c * per + k] = val[:, k * LANE:(k + 1) * LANE]

    for c in range(tn // MXU_N):
        accs.append(jnp.dot(hb, w_ref[:, c * MXU_N:(c + 1) * MXU_N], preferred_element_type=F32))
        if c > 0:
            epilogue(c - 1)
    epilogue(tn // MXU_N - 1)

    @pl.when(j == gate_tiles)
    def _():
        for c in range(N_ZBLK // per):
            for k in range(per):
                uz_ref[c * per + k] = accs[c][:, k * LANE:(k + 1) * LANE]


def _inproj(x2, mod_l, g, w_main, w_dt, seq, tm=1024, tn=2048):
    t = x2.shape[0]
    tm = min(tm, seq)
    nb = tn // LANE
    assert N_GATE % nb == 0 and BLK_ZCOPY == N_GATE and N_ZBLK <= nb
    kern = functools.partial(_inproj_kernel, tn=tn, gate_tiles=N_GATE // nb)
    return pl.pallas_call(
        kern,
        grid=(t // tm, N_MAIN // tn),
        in_specs=[pl.BlockSpec((tm, D_MODEL), lambda i, j: (i, 0)),
                  pl.BlockSpec((1, 1, 6 * D_MODEL), lambda i, j: ((i * tm) // seq, 0, 0)),
                  pl.BlockSpec((1, D_MODEL), lambda i, j: (0, 0)),
                  pl.BlockSpec((D_MODEL, tn), lambda i, j: (0, j)),
                  pl.BlockSpec((D_MODEL, LANE), lambda i, j: (0, 0))],
        out_specs=[pl.BlockSpec((N_ZBLK, tm, LANE), lambda i, j: (0, i, 0)),
                   pl.BlockSpec((nb, tm, LANE), lambda i, j: (j, i, 0)),
                   pl.BlockSpec((tm, LANE), lambda i, j: (i, 0))],
        out_shape=[jax.ShapeDtypeStruct((N_ZBLK, t, LANE), F32),
                   jax.ShapeDtypeStruct((N_BLK, t, LANE), BF16),
                   jax.ShapeDtypeStruct((t, LANE), F32)],
        scratch_shapes=[pltpu.VMEM((tm, D_MODEL), BF16)],
        compiler_params=_cparams(("parallel", "arbitrary")),
    )(x2, mod_l, g, w_main, w_dt)


def _hgrn_consts():
    c = HG_CHUNK
    nlev = len(HG_LEVELS)
    t = np.arange(c)[:, None]
    r = np.arange(c)[None, :]
    sums = np.zeros((2, 1 + len(HG_MXU_LEVELS), c, c), np.float32)
    masks = np.zeros((2, nlev + 1, c, c), np.float32)
    sums[0, 0] = r <= t
    sums[1, 0] = r >= t
    for li, m in enumerate(HG_LEVELS):
        grp = t // (2 * m)
        if m in HG_MXU_LEVELS:
            i = 1 + HG_MXU_LEVELS.index(m)
            ref = grp * 2 * m + m - 1
            sums[0, i] = ((t > ref) & (r > ref) & (r <= t)) | ((t <= ref) & (r > t) & (r <= ref))
            ref = grp * 2 * m + m
            sums[1, i] = ((t < ref) & (r >= t) & (r < ref)) | ((t >= ref) & (r >= ref) & (r < t))
        own = ((t // (2 * m)) == (r // (2 * m))) & ((t % (2 * m)) >= m) & ((r % (2 * m)) < m)
        masks[0, li] = own
        masks[1, li] = own.T
    masks[:, nlev] = np.eye(c)
    return sums.reshape(2, (1 + len(HG_MXU_LEVELS)) * c, c), masks


def _hgrn_chunks(streams, sums_ref, msk_ref):
    c = HG_CHUNK
    nlev = len(HG_LEVELS)
    nt = (((1,), (1,)), ((), ()))

    keys, pieces = [], []
    for (q, z, v, lb, llb, l1m, st, rev) in streams:
        logsig = jnp.minimum(z, 0.0) - jnp.log(1.0 + jnp.exp(-jnp.abs(z)))
        keys.append((1.0 - lb) * jnp.exp(logsig - z))
        y2 = l1m + logsig
        lf = jnp.maximum(llb, y2) + jnp.log(1.0 + jnp.exp(-jnp.abs(llb - y2)))
        pieces.append(_split_bf16(lf))

    ex_alls = []
    for (q, z, v, lb, llb, l1m, st, rev), p in zip(streams, pieces):
        ex_alls.append(_join_f32(jnp.dot(sums_ref[1 if rev else 0], p, preferred_element_type=F32)))

    qfs = [s[0].astype(F32) for s in streams]
    accs = []
    for (q, z, v, lb, llb, l1m, st, rev), key in zip(streams, keys):
        d = 1 if rev else 0
        accs.append(lax.dot_general(q, key.astype(BF16), nt, preferred_element_type=F32) * msk_ref[d, nlev])
    for li in range(nlev):
        m = HG_LEVELS[li]
        for si, ((q, z, v, lb, llb, l1m, st, rev), key) in enumerate(zip(streams, keys)):
            d = 1 if rev else 0
            if m in HG_MXU_LEVELS:
                i = 1 + HG_MXU_LEVELS.index(m)
                ex = jnp.exp(ex_alls[si][i * c:(i + 1) * c])
            else:
                b = ex_alls[si][0:c]
                b3 = b.reshape(c // (2 * m), 2 * m, LANE)
                off = m if rev else m - 1
                bref = jnp.broadcast_to(b3[:, off:off + 1, :], b3.shape).reshape(c, LANE)
                ex = jnp.exp(-jnp.abs(b - bref))
            sl = lax.dot_general((qfs[si] * ex).astype(BF16), (key * ex).astype(BF16), nt,
                                 preferred_element_type=F32)
            accs[si] = accs[si] + sl * msk_ref[d, li]

    outs = []
    for si, ((q, z, v, lb, llb, l1m, st, rev), key) in enumerate(zip(streams, keys)):
        b = ex_alls[si][0:c]
        o = jnp.dot(accs[si].astype(BF16), v, preferred_element_type=F32)
        qs = (qfs[si] * jnp.exp(b)).astype(BF16)
        o = o + lax.dot_general(qs, st.astype(BF16), nt, preferred_element_type=F32)
        bedge = b[0:1, :] if rev else b[c - 1:c, :]
        ks_ = (key * jnp.exp(bedge - b)).astype(BF16)
        vt = v.astype(F32).T.astype(BF16)
        st_new = st * jnp.exp(bedge) + jnp.dot(vt, ks_, preferred_element_type=F32)
        outs.append((o, st_new))
    return outs


def _hgrn_kernel(q_ref, zf_ref, zb_ref, v_ref, g_ref, lbp_ref, norm_ref, sums_ref, msk_ref, y_ref,
                 o_scr, st_scr, *, seq):
    c = HG_CHUNK
    n = seq // c
    half = n // 2

    def lanes(h):
        return slice(h * LANE, (h + 1) * LANE)

    def step(i, finalize):
        rf = pl.multiple_of(i * c, c)
        rb = pl.multiple_of((n - 1 - i) * c, c)
        streams = []
        for h in range(HG_HPS):
            for rev, r0, z_ref in ((False, rf, zf_ref), (True, rb, zb_ref)):
                d = 1 if rev else 0
                streams.append((q_ref[h, pl.ds(r0, c), :], z_ref[h, pl.ds(r0, c), :], v_ref[h, pl.ds(r0, c), :],
                                lbp_ref[0, d:d + 1, lanes(h)], lbp_ref[1, d:d + 1, lanes(h)],
                                lbp_ref[2, d:d + 1, lanes(h)], st_scr[2 * h + d], rev))
        outs = _hgrn_chunks(streams, sums_ref, msk_ref)
        for h in range(HG_HPS):
            for d, r0 in ((0, rf), (1, rb)):
                o, st = outs[2 * h + d]
                st_scr[2 * h + d] = st
                if not finalize:
                    o_scr[h, pl.ds(r0, c), :] = o
                else:
                    o = o + o_scr[h, pl.ds(r0, c), :]
                    o = o * lax.rsqrt(jnp.mean(o * o, axis=-1, keepdims=True) + EPS) * norm_ref[:, lanes(h)]
                    g = g_ref[h, pl.ds(r0, c), :].astype(F32)
                    y_ref[h, pl.ds(r0, c), :] = (o * (g * _sigmoid(g))).astype(BF16)

    def first_half(i, carry):
        step(i, False)
        return carry

    def second_half(i, carry):
        step(i, True)
        return carry

    st_scr[...] = jnp.zeros_like(st_scr)
    lax.fori_loop(0, half, first_half, 0)
    lax.fori_loop(half, n, second_half, 0)


def _hgrn(u3, uz, lbp_l, hg_norm_l, sums, masks, bsz, seq):
    hps = HG_HPS
    nlev = len(HG_LEVELS)
    assert (seq // HG_CHUNK) % 2 == 0

    def ublk(base):
        return pl.BlockSpec((hps, seq, LANE), lambda b, hp, base=base: (base // hps + hp, b, 0))

    def zblk(base):
        return pl.BlockSpec((hps, seq, LANE), lambda b, hp, base=base: (base // hps + hp, b, 0))

    kern = functools.partial(_hgrn_kernel, seq=seq)
    return pl.pallas_call(
        kern,
        grid=(bsz, HG_HEADS // hps),
        in_specs=[ublk(BLK_HG_Q), zblk(0), zblk(HG_HEADS), ublk(BLK_HG_I), ublk(BLK_HG_G),
                  pl.BlockSpec((3, 2, hps * LANE), lambda b, hp: (0, 0, hp)),
                  pl.BlockSpec((1, hps * LANE), lambda b, hp: (0, hp)),
                  pl.BlockSpec((2, (1 + len(HG_MXU_LEVELS)) * HG_CHUNK, HG_CHUNK), lambda b, hp: (0, 0, 0)),
                  pl.BlockSpec((2, nlev + 1, HG_CHUNK, HG_CHUNK), lambda b, hp: (0, 0, 0, 0))],
        out_specs=pl.BlockSpec((hps, seq, LANE), lambda b, hp: (hp, b, 0)),
        out_shape=jax.ShapeDtypeStruct((HG_HEADS, bsz * seq, LANE), BF16),
        scratch_shapes=[pltpu.VMEM((hps, seq, LANE), F32), pltpu.VMEM((2 * hps, LANE, LANE), F32)],
        compiler_params=_cparams(("parallel", "parallel")),
    )(u3, uz, uz, u3, u3, lbp_l, hg_norm_l, sums, masks)


NA_NEG = -1e30


def _na_bias_table(rpb_l):
    qc = np.arange(GRID_W)[:, None]
    kc = np.arange(GRID_W)[None, :]
    cstart = np.clip(qc - NA_COLS // 2, 0, GRID_W - NA_COLS)
    col_ok = (kc >= cstart) & (kc < cstart + NA_COLS)
    dc = np.clip(kc - qc, 1 - NA_COLS, NA_COLS - 1) + NA_COLS - 1
    onehot = (dc[None] == np.arange(2 * NA_COLS - 1)[:, None, None]).astype(np.float32)
    cols = jnp.einsum('hrc,cqk->hqrk', rpb_l.astype(F32), jnp.asarray(onehot),
                      precision=lax.Precision.HIGHEST)
    cols = jnp.where(col_ok[None, :, None, :], cols, NA_NEG)
    tab = jnp.stack([cols[:, :, NA_ROWS - 1 - s:2 * NA_ROWS - 1 - s] for s in range(NA_ROWS)], axis=1)
    return tab.reshape(NA_HEADS, NA_ROWS, GRID_W, NA_ROWS * GRID_W)


def _na_slot(variant, i):
    half = NA_ROWS // 2
    if variant == 0:
        return max(i - half, 0), min(i, half)
    if variant == 1:
        return i, half
    return (half + i, half) if i < half else (NA_G, i)


def _na_kernel(q_ref, k_ref, v_ref, tab_ref, o_ref, bias_scr, *, seq):
    rows = seq // GRID_W
    ngrp = rows // NA_G
    gq = NA_G * GRID_W
    kwin = 2 * NA_G * GRID_W
    kw = NA_ROWS * GRID_W
    nt = (((1,), (1,)), ((), ()))

    @pl.when(pl.program_id(1) == 0)
    def _():
        neg = jnp.full((GRID_W, LANE), NA_NEG, F32)
        for variant in range(3):
            for i in range(NA_G):
                a0, sft = _na_slot(variant, i)
                base = (a0 // 2) * LANE
                wid = min(kw + LANE, kwin - base)
                for h in range(2):
                    slab = jnp.concatenate([tab_ref[h, sft], neg], axis=1)
                    if a0 % 2:
                        slab = pltpu.roll(slab, GRID_W, 1)
                    pieces = [neg] * (base // LANE) + [slab[:, :wid]] + [neg] * ((kwin - base - wid) // LANE)
                    bias_scr[variant, h, i * GRID_W:(i + 1) * GRID_W, :] = jnp.concatenate(pieces, axis=1)

    head0 = lax.broadcasted_iota(jnp.int32, (gq, LANE), 1) < NA_DH
    zero = jnp.zeros((gq, LANE), BF16)

    def body(jj, carry):
        streams = []
        for g in range(NA_GPS):
            j = jj * NA_GPS + g
            variant = jnp.where(j == 0, 0, jnp.where(j == ngrp - 1, 2, 1))
            k0r = jnp.clip(j * NA_G - NA_ROWS // 2, 0, rows - 2 * NA_G)
            q0 = pl.multiple_of(j * gq, gq)
            k0 = pl.multiple_of(k0r * GRID_W, GRID_W)
            q = q_ref[0, pl.ds(q0, gq), :] * (NA_DH ** -0.5)
            kb = k_ref[0, pl.ds(k0, kwin), :]
            vb = v_ref[0, pl.ds(k0, kwin), :]
            streams.append((variant, 0, jnp.where(head0, q, zero), kb, vb, q0))
            streams.append((variant, 1, jnp.where(head0, zero, q), kb, vb, q0))
        ss = [lax.dot_general(qh, kb, nt, preferred_element_type=F32) for (_, _, qh, kb, _, _) in streams]
        ps, ls = [], []
        for (variant, h, _, _, _, _), s in zip(streams, ss):
            s = s + bias_scr[variant, h]
            p = jnp.exp(s - jnp.max(s, axis=-1, keepdims=True))
            ls.append(jnp.sum(p, axis=-1, keepdims=True))
            ps.append(p.astype(BF16))
        outs = [jnp.dot(p, vb, preferred_element_type=F32) for p, (_, _, _, _, vb, _) in zip(ps, streams)]
        for g in range(NA_GPS):
            q0 = streams[2 * g][5]
            o_ref[0, pl.ds(q0, gq), :] = jnp.where(head0, outs[2 * g] / ls[2 * g],
                                                   outs[2 * g + 1] / ls[2 * g + 1]).astype(BF16)
        return carry

    lax.fori_loop(0, ngrp // NA_GPS, body, 0)


def _na(u3, tab, bsz, seq):
    def ublk(base):
        return pl.BlockSpec((1, seq, LANE), lambda hp, b, base=base: (base + hp, b, 0))

    kern = functools.partial(_na_kernel, seq=seq)
    kw = NA_ROWS * GRID_W
    return pl.pallas_call(
        kern,
        grid=(NA_HEADS // 2, bsz),
        in_specs=[ublk(BLK_NA_Q), ublk(BLK_NA_K), ublk(BLK_NA_V),
                  pl.BlockSpec((2, NA_ROWS, GRID_W, kw), lambda hp, b: (hp, 0, 0, 0))],
        out_specs=pl.BlockSpec((1, seq, LANE), lambda hp, b: (hp, b, 0)),
        out_shape=jax.ShapeDtypeStruct((NA_HEADS // 2, bsz * seq, LANE), BF16),
        scratch_shapes=[pltpu.VMEM((3, 2, NA_G * GRID_W, 2 * NA_G * GRID_W), F32)],
        compiler_params=_cparams(("parallel", "arbitrary")),
    )(u3, u3, u3, tab)


def _gelu_tanh(x):
    return 0.5 * x * (1.0 + jnp.tanh(0.7978845608028654 * (x + 0.044715 * (x * x * x))))


def _lru_kernel(x_ref, gate_ref, cw_ref, cb_ref, gw_ref, gb_ref, lam_ref, shift_ref, y_ref,
                xpad, xf_scr, a_scr, u_scr, hf_scr, *, seq):
    w = LRU_CW
    tc = LRU_TC
    nch = seq // tc
    ngrp = seq // SUB
    nb = w // LANE
    r = CONV_ROWS

    zpad = jnp.zeros((HALO, w), BF16)
    xpad[0:HALO, :] = zpad
    xpad[seq + HALO:seq + 2 * HALO, :] = zpad
    for k in range(nb):
        xpad[HALO:seq + HALO, k * LANE:(k + 1) * LANE] = x_ref[k]

    sub = lax.broadcasted_iota(jnp.int32, (tc // SUB, SUB, w), 1)

    for d in range(2):
        rev = d == 1
        sp = _softplus(-lam_ref[0, d])

        def pass_a(i, carry, d=d, rev=rev, sp=sp):
            r0 = pl.multiple_of(i * tc, tc)
            if not rev:
                win = xpad[pl.ds(r0, tc + 2 * HALO), :]
                parts = [_conv_taps(win[s * r:(s + 1) * r + 2 * HALO], shift_ref[...],
                                    lambda j: cw_ref[j:j + 1, :], cb_ref[...]) for s in range(tc // r)]
                xf = jnp.concatenate(parts, axis=0)
                xf_scr[pl.ds(r0, tc), :] = xf
            else:
                xf = xf_scr[pl.ds(r0, tc), :]
            gts = jnp.dot(xf.astype(BF16), gw_ref[0, d], preferred_element_type=F32) + gb_ref[0, d]
            rg = _sigmoid(gts[:, 0:w])
            ig = _sigmoid(gts[:, w:2 * w])
            log_a = (-LRU_C) * rg * sp
            a = jnp.exp(log_a)
            y = -jnp.tanh(log_a) * (a * a + 1.0)
            u = y * lax.rsqrt(jnp.maximum(y, LRU_TINY)) * (ig * xf)
            a = a.reshape(tc // SUB, SUB, w)
            u = u.reshape(tc // SUB, SUB, w)
            for dd in (1, 2, 4):
                sh = (SUB - dd) if rev else dd
                ok = (sub < SUB - dd) if rev else (sub >= dd)
                a_s = pltpu.roll(a, sh, 1)
                u_s = pltpu.roll(u, sh, 1)
                u = jnp.where(ok, a * u_s + u, u)
                a = jnp.where(ok, a * a_s, a)
            a_scr[pl.ds(r0, tc), :] = a.reshape(tc, w)
            u_scr[pl.ds(r0, tc), :] = u.reshape(tc, w)
            return carry

        lax.fori_loop(0, nch, pass_a, 0)

        def pass_b(g, h, rev=rev):
            gi = (ngrp - 1 - g) if rev else g
            r0 = pl.multiple_of(gi * SUB, SUB)
            hp = h[0:1, :] if rev else h[SUB - 1:SUB, :]
            hn = u_scr[pl.ds(r0, SUB), :] + a_scr[pl.ds(r0, SUB), :] * hp
            if rev:
                u_scr[pl.ds(r0, SUB), :] = hn
            else:
                hf_scr[pl.ds(r0, SUB), :] = hn
            return hn

        lax.fori_loop(0, ngrp, pass_b, jnp.zeros((SUB, w), F32), unroll=8)

    def pass_c(i, carry):
        r0 = pl.multiple_of(i * tc, tc)
        hsum = hf_scr[pl.ds(r0, tc), :] + u_scr[pl.ds(r0, tc), :]
        for k in range(nb):
            g = gate_ref[k, pl.ds(r0, tc), :].astype(F32)
            y_ref[k, pl.ds(r0, tc), :] = (hsum[:, k * LANE:(k + 1) * LANE] * _gelu_tanh(g)).astype(BF16)
        return carry

    lax.fori_loop(0, nch, pass_c, 0)


def _lru_weights(gate_w_l, gate_b_l, lam_l):
    ncb = LRU_WIDTH // LRU_CW
    per = LRU_CW // LRU_BW
    gw = gate_w_l.astype(F32)
    eye = jnp.eye(per, dtype=F32)
    blocks = gw.reshape(2, 2, ncb, per, LRU_BW, LRU_BW)
    dense = jnp.einsum('dgcpio,pq->dgcpiqo', blocks, eye).reshape(2, 2, ncb, LRU_CW, LRU_CW)
    dense = dense.transpose(2, 0, 3, 1, 4).reshape(ncb, 2, LRU_CW, 2 * LRU_CW).astype(BF16)
    gb = gate_b_l.astype(F32).reshape(2, 2, ncb, LRU_CW).transpose(2, 0, 1, 3).reshape(ncb, 2, 1, 2 * LRU_CW)
    lam = lam_l.astype(F32).reshape(2, ncb, LRU_CW).transpose(1, 0, 2).reshape(ncb, 2, 1, LRU_CW)
    return dense, gb, lam


def _lru(u3, conv_w_l, conv_b_l, gwd, gbd, lamd, shift, bsz, seq):
    ncb = LRU_WIDTH // LRU_CW
    nb = LRU_CW // LANE
    kern = functools.partial(_lru_kernel, seq=seq)

    def ublk(base):
        return pl.BlockSpec((nb, seq, LANE), lambda b, cb, base=base: (base // nb + cb, b, 0))

    return pl.pallas_call(
        kern,
        grid=(bsz, ncb),
        in_specs=[ublk(BLK_LRU_X), ublk(BLK_LRU_G),
                  pl.BlockSpec((CONV_TAPS, LRU_CW), lambda b, cb: (0, cb)),
                  pl.BlockSpec((1, LRU_CW), lambda b, cb: (0, cb)),
                  pl.BlockSpec((1, 2, LRU_CW, 2 * LRU_CW), lambda b, cb: (cb, 0, 0, 0)),
                  pl.BlockSpec((1, 2, 1, 2 * LRU_CW), lambda b, cb: (cb, 0, 0, 0)),
                  pl.BlockSpec((1, 2, 1, LRU_CW), lambda b, cb: (cb, 0, 0, 0)),
                  pl.BlockSpec(shift.shape, lambda b, cb: (0, 0))],
        out_specs=pl.BlockSpec((nb, seq, LANE), lambda b, cb: (cb, b, 0)),
        out_shape=jax.ShapeDtypeStruct((LRU_WIDTH // LANE, bsz * seq, LANE), BF16),
        scratch_shapes=[pltpu.VMEM((seq + 2 * HALO, LRU_CW), BF16),
                        pltpu.VMEM((seq, LRU_CW), F32),
                        pltpu.VMEM((seq, LRU_CW), F32),
                        pltpu.VMEM((seq, LRU_CW), F32),
                        pltpu.VMEM((seq, LRU_CW), F32)],
        compiler_params=_cparams(("parallel", "parallel")),
    )(u3, u3, conv_w_l, conv_b_l.reshape(1, LRU_WIDTH), gwd, gbd, lamd, shift)


def _ssd_decay(dt_raw, dtb_ref, alog_ref, tri_ref):
    dt = _softplus(dt_raw + dtb_ref[...])
    a = dt * (-jnp.exp(alog_ref[...]))
    lane = lax.broadcasted_iota(jnp.int32, (SSD_Q, LANE), 1)
    ap = _split_bf16(a)
    pre = _join_f32(jnp.dot(tri_ref[0], ap, preferred_element_type=F32))
    suf = _join_f32(jnp.dot(tri_ref[1], ap, preferred_element_type=F32))
    return dt, jnp.where(lane < SSD_HEADS, pre, suf)


def _ssd_conv(cur_ref, prev_ref, next_ref, cw_ref, cb_ref, shift_ref, n, nlast, sc):
    q = SSD_Q
    nblk = cur_ref.shape[0]
    r0 = sc * q
    zero = jnp.zeros((HALO, LANE), BF16)
    wins = []
    for k in range(nblk):
        if sc == 0:
            pv = jnp.where(n > 0, prev_ref[k], zero)
        else:
            pv = cur_ref[k, r0 - HALO:r0, :]
        if sc == SSD_SUB - 1:
            nx = jnp.where(n < nlast, next_ref[k], zero)
        else:
            nx = cur_ref[k, r0 + q:r0 + q + HALO, :]
        wins.append(jnp.concatenate([pv, cur_ref[k, r0:r0 + q, :], nx], axis=0))
    win = jnp.concatenate(wins, axis=1)
    r = CONV_ROWS
    sh = jnp.dot(shift_ref[...], win, preferred_element_type=F32)
    xs = []
    for k in range(nblk):
        cols = slice(k * LANE, (k + 1) * LANE)
        acc = cb_ref[k] + cw_ref[CONV_TAPS // 2, k] * cur_ref[k, r0:r0 + q, :].astype(F32)
        i = 0
        for j in range(CONV_TAPS):
            if j != CONV_TAPS // 2:
                acc = acc + cw_ref[j, k] * sh[i * r:(i + 1) * r, cols]
                i += 1
        xs.append(acc * _sigmoid(acc))
    return xs


def _ssd_expand_matrix():
    nx = SSD_HEADS // 2
    m = np.zeros((2, LANE, nx * LANE), np.float32)
    for d in range(2):
        for j in range(nx):
            for half in range(2):
                m[d, d * SSD_HEADS + 2 * j + half, j * LANE + half * SSD_P:j * LANE + (half + 1) * SSD_P] = 1.0
    return np.concatenate([m, m], axis=1)


def _ssd_state_step(xh, bm, cm, dt, acc_, ex, h_scr, rev, expand_ref):
    q = SSD_Q
    d = 1 if rev else 0
    off = d * SSD_HEADS
    er = 0 if rev else q - 1
    edge = acc_[er:er + 1, :]
    lane = lax.broadcasted_iota(jnp.int32, (q, LANE), 1)
    if expand_ref is not None:
        mine = jnp.logical_and(lane >= off, lane < off + SSD_HEADS)
        wt = jnp.where(mine, jnp.exp(edge - acc_) * dt, 0.0)
        pex = jnp.dot(_split_bf16(ex), expand_ref[d], preferred_element_type=F32)
        pwt = jnp.dot(_split_bf16(wt), expand_ref[d], preferred_element_type=F32)
        pattern = lambda p, j: p[:, j * LANE:(j + 1) * LANE]
    else:
        wt = jnp.exp(edge - acc_) * dt
        pex, pwt = ex, wt
        lane_lo = lane < SSD_P

        def pattern(p, j):
            c0 = off + 2 * j
            return jnp.where(lane_lo, p[:, c0:c0 + 1], p[:, c0 + 1:c0 + 2])

    y_off = []
    per_grp = SSD_HEADS // SSD_GROUPS // 2
    for j in range(SSD_HEADS // 2):
        g = j // per_grp
        hj = h_scr[j]
        pe = pattern(pex, j)
        y_off.append(pe * jnp.dot(cm[g], hj.astype(BF16), preferred_element_type=F32))
        xt = (xh[j] * pattern(pwt, j)).astype(BF16)
        h_scr[j] = hj * pe[er:er + 1, :] + lax.dot_general(
            bm[g], xt, (((0,), (0,)), ((), ())), preferred_element_type=F32)
    return y_off


def _ssd_fwd_kernel(cur_ref, prev_ref, next_ref, dt_ref, cw_ref, cb_ref, dtb_ref, alog_ref,
                    shift_ref, tri_ref, yp_ref, xc_ref, h_scr, *, nsteps):
    q = SSD_Q
    n = pl.program_id(1)

    @pl.when(n == 0)
    def _():
        h_scr[...] = jnp.zeros_like(h_scr)

    nx = SSD_HEADS // 2
    lane_lo = lax.broadcasted_iota(jnp.int32, (q, LANE), 1) < SSD_P
    ti = lax.broadcasted_iota(jnp.int32, (q, q), 0)
    si = lax.broadcasted_iota(jnp.int32, (q, q), 1)
    lower = si <= ti
    upper = si >= ti
    nt = (((1,), (1,)), ((), ()))
    heads_per_grp = SSD_HEADS // SSD_GROUPS

    for sc in range(SSD_SUB):
        rows = slice(sc * q, (sc + 1) * q)
        dt, acc_ = _ssd_decay(dt_ref[rows, :], dtb_ref, alog_ref, tri_ref)
        ex = jnp.exp(acc_)
        acc_t = acc_.T
        dt_t = dt.T
        xs = _ssd_conv(cur_ref, prev_ref, next_ref, cw_ref, cb_ref, shift_ref, n, nsteps - 1, sc)
        for k in range(len(xs)):
            xc_ref[k, rows, :] = xs[k].astype(BF16)
        xh = xs[0:nx]
        bm = [t.astype(BF16) for t in xs[nx:nx + SSD_GROUPS]]
        cm = [t.astype(BF16) for t in xs[nx + SSD_GROUPS:nx + 2 * SSD_GROUPS]]

        y = []
        for j in range(nx):
            g = (2 * j) // heads_per_grp
            cb = lax.dot_general(cm[g], bm[g], nt, preferred_element_type=F32)
            parts = []
            for h in (2 * j, 2 * j + 1):
                hb = SSD_HEADS + h
                lf = jnp.where(lower, jnp.exp(acc_[:, h:h + 1] - acc_t[h:h + 1, :]), 0.0)
                lb = jnp.where(upper, jnp.exp(acc_[:, hb:hb + 1] - acc_t[hb:hb + 1, :]), 0.0)
                mh = (cb * (lf * dt_t[h:h + 1, :] + lb * dt_t[hb:hb + 1, :])).astype(BF16)
                parts.append(jnp.dot(mh, xh[j].astype(BF16), preferred_element_type=F32))
            y.append(jnp.where(lane_lo, parts[0], parts[1]))

        y_off = _ssd_state_step(xh, bm, cm, dt, acc_, ex, h_scr, False, None)
        for j in range(nx):
            yp_ref[j, rows, :] = y[j] + y_off[j]


def _ssd_bwd_kernel(xc_ref, dt_ref, dtb_ref, alog_ref, tri_ref, expand_ref, yp_ref, z_ref, skip_ref,
                    norm_ref, y_ref, h_scr):
    q = SSD_Q
    n = pl.program_id(1)

    @pl.when(n == 0)
    def _():
        h_scr[...] = jnp.zeros_like(h_scr)

    nx = SSD_HEADS // 2
    for sc in reversed(range(SSD_SUB)):
        rows = slice(sc * q, (sc + 1) * q)
        dt, acc_ = _ssd_decay(dt_ref[rows, :], dtb_ref, alog_ref, tri_ref)
        ex = jnp.exp(acc_)
        xh = [xc_ref[k, rows, :].astype(F32) for k in range(nx)]
        bm = [xc_ref[nx + g, rows, :] for g in range(SSD_GROUPS)]
        cm = [xc_ref[nx + SSD_GROUPS + g, rows, :] for g in range(SSD_GROUPS)]
        y_off = _ssd_state_step(xh, bm, cm, dt, acc_, ex, h_scr, True, expand_ref)

        ys = []
        ssq = jnp.zeros((q, 1), F32)
        for j in range(nx):
            z = z_ref[j, rows, :].astype(F32)
            yj = (yp_ref[j, rows, :] + y_off[j] + skip_ref[j] * xh[j]) * (z * _sigmoid(z))
            ssq = ssq + jnp.sum(yj * yj, axis=-1, keepdims=True)
            ys.append(yj)
        r = lax.rsqrt(ssq * (1.0 / (nx * LANE)) + EPS)
        for j in range(nx):
            y_ref[j, rows, :] = (ys[j] * r * norm_ref[j]).astype(BF16)


def _ssd(u3, udt, conv_w_l, conv_b_l, dt_bias_l, a_log_l, skip_l, norm_l, shift, tri, expand, bsz, seq):
    q = SSD_SUB * SSD_Q
    nchunks = seq // q
    nxbc = N_XBC
    nx = SSD_HEADS // 2
    hb = q // HALO
    tb = (bsz * seq) // HALO

    cw = conv_w_l.astype(F32).reshape(CONV_TAPS, nxbc, 1, LANE)
    cb = conv_b_l.astype(F32).reshape(nxbc, 1, LANE)
    dtb = jnp.zeros((1, LANE), F32).at[0, :N_DT].set(dt_bias_l.astype(F32).reshape(-1))
    alog = jnp.zeros((1, LANE), F32).at[0, :N_DT].set(a_log_l.astype(F32).reshape(-1))
    skip = jnp.repeat(skip_l.astype(F32), SSD_P).reshape(nx, 1, LANE)
    norm = norm_l.astype(F32).reshape(nx, 1, LANE)

    def cur(b, n):
        return (BLK_SSD_X // nxbc, b * nchunks + n, 0)

    def prev(b, n):
        return (BLK_SSD_X // nxbc, jnp.maximum((b * nchunks + n) * hb - 1, 0), 0)

    def nxt(b, n):
        return (BLK_SSD_X // nxbc, jnp.minimum((b * nchunks + n + 1) * hb, tb - 1), 0)

    def const(arr):
        return pl.BlockSpec(arr.shape, lambda b, n: (0,) * arr.ndim)

    state = pltpu.VMEM((nx, SSD_N, LANE), F32)
    fwd_blk = lambda b, n: (0, b * nchunks + n, 0)
    ypart, xconv = pl.pallas_call(
        functools.partial(_ssd_fwd_kernel, nsteps=nchunks),
        grid=(bsz, nchunks),
        in_specs=[pl.BlockSpec((nxbc, q, LANE), cur),
                  pl.BlockSpec((nxbc, HALO, LANE), prev),
                  pl.BlockSpec((nxbc, HALO, LANE), nxt),
                  pl.BlockSpec((q, LANE), lambda b, n: (b * nchunks + n, 0)),
                  const(cw), const(cb), const(dtb), const(alog), const(shift), const(tri)],
        out_specs=[pl.BlockSpec((nx, q, LANE), fwd_blk), pl.BlockSpec((nxbc, q, LANE), fwd_blk)],
        out_shape=[jax.ShapeDtypeStruct((nx, bsz * seq, LANE), F32),
                   jax.ShapeDtypeStruct((nxbc, bsz * seq, LANE), BF16)],
        scratch_shapes=[state],
        compiler_params=_cparams(("parallel", "arbitrary")),
    )(u3, u3, u3, udt, cw, cb, dtb, alog, shift, tri)

    bwd_blk = lambda b, n: (0, b * nchunks + nchunks - 1 - n, 0)
    return pl.pallas_call(
        _ssd_bwd_kernel,
        grid=(bsz, nchunks),
        in_specs=[pl.BlockSpec((nxbc, q, LANE), bwd_blk),
                  pl.BlockSpec((q, LANE), lambda b, n: (b * nchunks + nchunks - 1 - n, 0)),
                  const(dtb), const(alog), const(tri), const(expand),
                  pl.BlockSpec((nx, q, LANE), bwd_blk),
                  pl.BlockSpec((nx, q, LANE), lambda b, n: (BLK_SSD_Z // nx, b * nchunks + nchunks - 1 - n, 0)),
                  const(skip), const(norm)],
        out_specs=pl.BlockSpec((nx, q, LANE), bwd_blk),
        out_shape=jax.ShapeDtypeStruct((nx, bsz * seq, LANE), BF16),
        scratch_shapes=[state],
        compiler_params=_cparams(("parallel", "arbitrary")),
    )(xconv, udt, dtb, alog, tri, expand, ypart, u3, skip, norm)


def _cat_blocks(ref, lo, n):
    return jnp.concatenate([ref[lo + k] for k in range(n)], axis=-1)


def _merge_kernel(yhg_ref, yna_ref, ylru_ref, yssd_ref, gt_ref, x_ref, mod_ref,
                  whg_ref, wna_ref, wlru_ref, wssd_ref, wout_ref, o_ref, m_scr):
    nd = D_MODEL // LANE
    per = MXU_N // LANE
    nchunk = D_MODEL // MXU_N
    branches = ((yhg_ref, whg_ref), (yna_ref, wna_ref), (ylru_ref, wlru_ref), (yssd_ref, wssd_ref))
    ys = [_cat_blocks(y_ref, 0, y_ref.shape[0]) for y_ref, _ in branches]

    prods = []

    def gate_sum(c):
        tot = None
        for i in range(len(branches)):
            term = _cat_blocks(gt_ref, nd * i + per * c, per).astype(F32) * prods[c][i]
            tot = term if tot is None else tot + term
        m_scr[:, c * MXU_N:(c + 1) * MXU_N] = tot.astype(BF16)

    for c in range(nchunk):
        cols = slice(c * MXU_N, (c + 1) * MXU_N)
        prods.append([jnp.dot(y, w_ref[:, cols], preferred_element_type=F32)
                      for y, (_, w_ref) in zip(ys, branches)])
        if c > 0:
            gate_sum(c - 1)
    gate_sum(nchunk - 1)

    mb = m_scr[...]
    gate = mod_ref[0][:, 2 * D_MODEL:3 * D_MODEL]
    outs = []

    def residual(c):
        cols = slice(c * MXU_N, (c + 1) * MXU_N)
        o_ref[:, cols] = x_ref[:, cols] + gate[:, cols] * outs[c]

    for c in range(nchunk):
        outs.append(jnp.dot(mb, wout_ref[:, c * MXU_N:(c + 1) * MXU_N], preferred_element_type=F32))
        if c > 0:
            residual(c - 1)
    residual(nchunk - 1)


def _merge(y_hg, y_na, y_lru, y_ssd, u3, x2, mod_l, w_hg, w_na, w_lru, w_ssd, w_out, seq, tm=512):
    t = x2.shape[0]
    tm = min(tm, seq)

    def yspec(arr):
        return pl.BlockSpec((arr.shape[0], tm, LANE), lambda i: (0, i, 0))

    def wspec(arr):
        return pl.BlockSpec(arr.shape, lambda i: (0, 0))

    return pl.pallas_call(
        _merge_kernel,
        grid=(t // tm,),
        in_specs=[yspec(y_hg), yspec(y_na), yspec(y_lru), yspec(y_ssd),
                  pl.BlockSpec((N_GATE, tm, LANE), lambda i: (BLK_MERGE // N_GATE, i, 0)),
                  pl.BlockSpec((tm, D_MODEL), lambda i: (i, 0)),
                  pl.BlockSpec((1, 1, 6 * D_MODEL), lambda i: ((i * tm) // seq, 0, 0)),
                  wspec(w_hg), wspec(w_na), wspec(w_lru), wspec(w_ssd), wspec(w_out)],
        out_specs=pl.BlockSpec((tm, D_MODEL), lambda i: (i, 0)),
        out_shape=jax.ShapeDtypeStruct((t, D_MODEL), F32),
        scratch_shapes=[pltpu.VMEM((tm, D_MODEL), BF16)],
        compiler_params=_cparams(("parallel",)),
    )(y_hg, y_na, y_lru, y_ssd, u3, x2, mod_l, w_hg, w_na, w_lru, w_ssd, w_out)


def _ffn_kernel(x_ref, mod_ref, g_ref, w1_ref, w2_ref, gfin_ref, o_ref, h_scr, e_scr, acc_scr, *, final_norm):
    j = pl.program_id(1)
    nj = pl.num_programs(1)

    @pl.when(j == 0)
    def _():
        acc_scr[...] = jnp.zeros_like(acc_scr)
        x = x_ref[...]
        ms = jnp.mean(x * x, axis=-1, keepdims=True)
        y = x * lax.rsqrt(ms + EPS) * g_ref[...]
        mod = mod_ref[0]
        shift = mod[:, 3 * D_MODEL:4 * D_MODEL]
        scale = mod[:, 4 * D_MODEL:5 * D_MODEL]
        h_scr[...] = (y * (1.0 + scale) + shift).astype(BF16)

    hb = h_scr[...]
    tf = w1_ref.shape[1]
    hids = []

    def square(c):
        hid = jnp.maximum(hids[c], 0.0)
        e_scr[:, c * MXU_N:(c + 1) * MXU_N] = (hid * hid).astype(BF16)

    for c in range(tf // MXU_N):
        hids.append(jnp.dot(hb, w1_ref[:, c * MXU_N:(c + 1) * MXU_N], preferred_element_type=F32))
        if c > 0:
            square(c - 1)
    square(tf // MXU_N - 1)

    eb = e_scr[...]
    parts = []

    def accumulate(c):
        cols = slice(c * MXU_N, (c + 1) * MXU_N)
        acc_scr[:, cols] = acc_scr[:, cols] + parts[c]

    for c in range(D_MODEL // MXU_N):
        parts.append(jnp.dot(eb, w2_ref[:, c * MXU_N:(c + 1) * MXU_N], preferred_element_type=F32))
        if c > 0:
            accumulate(c - 1)
    accumulate(D_MODEL // MXU_N - 1)

    @pl.when(j == nj - 1)
    def _():
        gate = mod_ref[0][:, 5 * D_MODEL:6 * D_MODEL]
        xo = x_ref[...] + gate * acc_scr[...]
        if final_norm:
            ms = jnp.mean(xo * xo, axis=-1, keepdims=True)
            xo = xo * lax.rsqrt(ms + EPS) * gfin_ref[...]
        o_ref[...] = xo


def _ffn(x2, mod_l, g, w1, w2, g_final, seq, final_norm, tm=1024, tf=1024):
    t = x2.shape[0]
    tm = min(tm, seq)
    return pl.pallas_call(
        functools.partial(_ffn_kernel, final_norm=final_norm),
        grid=(t // tm, D_FF // tf),
        in_specs=[pl.BlockSpec((tm, D_MODEL), lambda i, j: (i, 0)),
                  pl.BlockSpec((1, 1, 6 * D_MODEL), lambda i, j: ((i * tm) // seq, 0, 0)),
                  pl.BlockSpec((1, D_MODEL), lambda i, j: (0, 0)),
                  pl.BlockSpec((D_MODEL, tf), lambda i, j: (0, j)),
                  pl.BlockSpec((tf, D_MODEL), lambda i, j: (j, 0)),
                  pl.BlockSpec((1, D_MODEL), lambda i, j: (0, 0))],
        out_specs=pl.BlockSpec((tm, D_MODEL), lambda i, j: (i, 0)),
        out_shape=jax.ShapeDtypeStruct((t, D_MODEL), F32),
        scratch_shapes=[pltpu.VMEM((tm, D_MODEL), BF16), pltpu.VMEM((tm, tf), BF16),
                        pltpu.VMEM((tm, D_MODEL), F32)],
        compiler_params=_cparams(("parallel", "arbitrary")),
    )(x2, mod_l, g, w1, w2, g_final)


W_IN_ORDER = ((8224, 12320),
              (512, 1536),
              (0, 512),
              (1536, 2560),
              (2560, 4096),
              (6144, 8192),
              (4096, 5120),
              (5120, 6144))
W_IN_DT = (8192, 8224)


def _split_w_in(w_in_l):
    main = jnp.concatenate([w_in_l[:, a:b] for a, b in W_IN_ORDER], axis=1).astype(BF16)
    dt = jnp.zeros((D_MODEL, LANE), BF16).at[:, :N_DT].set(w_in_l[:, W_IN_DT[0]:W_IN_DT[1]].astype(BF16))
    return main, dt


def kernel(x, c, w_ada, b_ada, g_mix, g_ffn, w_in, hg_lb, hg_norm, na_rpb, lru_conv_w, lru_conv_b, lru_gate_w, lru_gate_b, lru_lambda, ssd_conv_w, ssd_conv_b, ssd_dt_bias, ssd_a_log, ssd_skip, ssd_norm, w_br_hg, w_br_na, w_br_lru, w_br_ssd, w_out, w_ff1, w_ff2, g_final):
    bsz, seq, d = x.shape
    depth = w_in.shape[0]
    rows = seq // GRID_W
    assert d == D_MODEL and seq % LRU_TC == 0 and rows % (NA_G * NA_GPS) == 0 and rows >= 2 * NA_G

    mod = _ada_mod(c, w_ada, b_ada).reshape(depth, bsz, 1, 6 * D_MODEL)
    lbp = _hg_lower_bounds(hg_lb)
    hg_sums_np, hg_masks_np = _hgrn_consts()
    hg_sums = jnp.asarray(hg_sums_np, BF16)
    hg_masks = jnp.asarray(hg_masks_np)
    shift = jnp.asarray(_conv_shift_matrix(), BF16)
    tri = hg_sums[:, :HG_CHUNK, :]
    expand = jnp.asarray(_ssd_expand_matrix(), BF16)
    gfin = g_final.reshape(1, D_MODEL)

    x2 = x.reshape(bsz * seq, D_MODEL)
    for l in range(depth):
        w_main, w_dt = _split_w_in(w_in[l])
        uz, u3, udt = _inproj(x2, mod[l], g_mix[l].reshape(1, D_MODEL), w_main, w_dt, seq)
        y_hg = _hgrn(u3, uz, lbp[:, l], hg_norm[l].reshape(1, -1), hg_sums, hg_masks, bsz, seq)
        y_na = _na(u3, _na_bias_table(na_rpb[l]), bsz, seq)
        gwd, gbd, lamd = _lru_weights(lru_gate_w[l], lru_gate_b[l], lru_lambda[l])
        y_lru = _lru(u3, lru_conv_w[l], lru_conv_b[l], gwd, gbd, lamd, shift, bsz, seq)
        y_ssd = _ssd(u3, udt, ssd_conv_w[l], ssd_conv_b[l], ssd_dt_bias[l], ssd_a_log[l],
                     ssd_skip[l], ssd_norm[l], shift, tri, expand, bsz, seq)
        x2 = _merge(y_hg, y_na, y_lru, y_ssd, u3, x2, mod[l],
                    w_br_hg[l].astype(BF16), w_br_na[l].astype(BF16), w_br_lru[l].astype(BF16),
                    w_br_ssd[l].astype(BF16), w_out[l].astype(BF16), seq)
        x2 = _ffn(x2, mod[l], g_ffn[l].reshape(1, D_MODEL), w_ff1[l].astype(BF16), w_ff2[l].astype(BF16),
                  gfin, seq, final_norm=(l == depth - 1))
    return x2.reshape(bsz, seq, D_MODEL)
```
